```python
import jax
import jax.numpy as jnp
from jax import lax
import numpy as np

D_MODEL = 4096
BATCH = 4
SEQ = 2048
DEPTH = 1
DEC_BATCH = 128
DEC_SEQ = 4
PAST_LEN = 16384
PAGE_SIZE = 128

GLA_HEADS = 4
GLA_VAL = D_MODEL // 2
GLA_KEY = GLA_VAL // 2
GLA_DK = GLA_KEY // GLA_HEADS
GLA_DV = GLA_VAL // GLA_HEADS
GLA_RANK = 16
GLA_TAU = 16.0
GLA_CHUNK = 64
CONV_DIM = D_MODEL // 2
CONV_WIDTH = 31
N_MEM = 256
CA_HEADS = 4
CA_HEAD_DIM = D_MODEL // 16
CA_DIM = CA_HEADS * CA_HEAD_DIM
N_GROUPS = 4
EXPERTS_PER_GROUP = 8
N_EXPERTS = N_GROUPS * EXPERTS_PER_GROUP
TOP_K_IN_GROUP = 2
D_EXPERT = D_MODEL // 8
EPS = 1e-6

OFF_Q = 0
OFF_K = OFF_Q + GLA_KEY
OFF_V = OFF_K + GLA_KEY
OFF_R = OFF_V + GLA_VAL
OFF_A = OFF_R + GLA_VAL
OFF_U = OFF_A + GLA_RANK
OFF_G = OFF_U + 2 * CONV_DIM
IN_COLS = OFF_G + 2 * D_MODEL

kernel_name = 'gla_conformer_parallel_hmoe_decoder_step'

F32 = jnp.float32


def _rmsnorm(x, g):
    xf = x.astype(F32)
    y = xf * lax.rsqrt(jnp.mean(xf * xf, axis=-1, keepdims=True) + EPS)
    return (y * g.astype(F32)).astype(x.dtype)


def _layernorm(xf, g, b):
    mu = jnp.mean(xf, axis=-1, keepdims=True)
    xc = xf - mu
    var = jnp.mean(xc * xc, axis=-1, keepdims=True)
    return xc * lax.rsqrt(var + EPS) * g.astype(F32) + b.astype(F32)


def _gla(q, k, v, log_a, h0):
    B, T, H, _ = q.shape
    C = min(GLA_CHUNK, T)
    pad = (-T) % C
    n = (T + pad) // C

    def blocks(t):
        t = jnp.pad(t.astype(F32), ((0, 0), (0, pad), (0, 0), (0, 0)))
        return t.reshape(B, n, C, H, t.shape[-1]).transpose(0, 3, 1, 2, 4)

    q, k, v, log_a = blocks(q), blocks(k), blocks(v), blocks(log_a)
    b = jnp.cumsum(log_a, axis=3)
    b_last = b[:, :, :, -1:, :]
    q_e = q * jnp.exp(b)
    k_e = k * jnp.exp(-b)
    causal = jnp.tril(jnp.ones((C, C), dtype=bool))
    scores = jnp.where(causal, jnp.einsum('bhncd,bhnsd->bhncs', q_e, k_e), 0.0)
    o_intra = jnp.einsum('bhncs,bhnsv->bhncv', scores, v)
    d_state = jnp.einsum('bhnsd,bhnsv->bhndv', k * jnp.exp(b_last - b), v)
    decay = jnp.exp(b_last[:, :, :, 0, :])

    def step(S, inp):
        dec, ds = inp
        return dec[..., None] * S + ds, S

    S_T, S_prev = lax.scan(step, h0.astype(F32),
                           (jnp.moveaxis(decay, 2, 0), jnp.moveaxis(d_state, 2, 0)))
    S_prev = jnp.moveaxis(S_prev, 0, 2)
    o = o_intra + jnp.einsum('bhncd,bhndv->bhncv', q_e, S_prev)
    o = o.transpose(0, 2, 3, 1, 4).reshape(B, n * C, H, -1)[:, :T]
    return o, S_T


def _mixer(h, h0, buf, w_in, w_alpha_up, b_alpha, gla_norm_g, w_branch_a,
           conv_dw_w, conv_dw_b, conv_ln_g, conv_ln_b, w_branch_b, w_out):
    B, T, _ = h.shape
    z = h @ w_in
    q = z[..., OFF_Q:OFF_K].reshape(B, T, GLA_HEADS, GLA_DK) * (GLA_DK ** -0.5)
    k = z[..., OFF_K:OFF_V].reshape(B, T, GLA_HEADS, GLA_DK)
    v = z[..., OFF_V:OFF_R].reshape(B, T, GLA_HEADS, GLA_DV)
    r = z[..., OFF_R:OFF_A]
    a_low = z[..., OFF_A:OFF_U]
    u = z[..., OFF_U:OFF_G]
    gates = jax.nn.sigmoid(z[..., OFF_G:].astype(F32))
    log_a = jax.nn.log_sigmoid((a_low @ w_alpha_up + b_alpha).astype(F32)) / GLA_TAU
    log_a = log_a.reshape(B, T, GLA_HEADS, GLA_DK)
    o, h_T = _gla(q, k, v, log_a, h0)
    mu = jnp.mean(o, axis=-1, keepdims=True)
    oc = o - mu
    o_n = oc * lax.rsqrt(jnp.mean(oc * oc, axis=-1, keepdims=True) + EPS)
    o_n = o_n * gla_norm_g.astype(F32).reshape(GLA_HEADS, GLA_DV)
    o_g = o_n.reshape(B, T, GLA_VAL) * jax.nn.silu(r.astype(F32))
    branch_a = o_g.astype(h.dtype) @ w_branch_a
    ug = u[..., :CONV_DIM] * jax.nn.sigmoid(u[..., CONV_DIM:])
    ext = jnp.concatenate([buf.astype(ug.dtype), ug], axis=1)
    conv = lax.conv_general_dilated(ext, conv_dw_w[:, None, :].astype(ext.dtype),
                                    window_strides=(1,), padding='VALID',
                                    dimension_numbers=('NWC', 'WIO', 'NWC'),
                                    feature_group_count=CONV_DIM) + conv_dw_b
    c = jax.nn.silu(_layernorm(conv.astype(F32), conv_ln_g, conv_ln_b))
    branch_b = c.astype(h.dtype) @ w_branch_b
    merged = gates[..., :D_MODEL] * branch_a + gates[..., D_MODEL:] * branch_b
    y = merged.astype(h.dtype) @ w_out
    new_buf = ext[:, -(CONV_WIDTH - 1):]
    return y, h_T, new_buf


def _mem_kv(mem, norm_mem_g, w_ca_k, w_ca_v):
    B, M, _ = mem.shape
    m = _rmsnorm(mem, norm_mem_g)
    k = (m @ w_ca_k).reshape(B, M, CA_HEADS, CA_HEAD_DIM)
    v = (m @ w_ca_v).reshape(B, M, CA_HEADS, CA_HEAD_DIM)
    return k, v


def _cross_attn(h, mem_k, mem_v, w_ca_q, w_ca_o):
    B, T, _ = h.shape
    q = (h @ w_ca_q).reshape(B, T, CA_HEADS, CA_HEAD_DIM)
    s = jnp.einsum('bthd,bshd->bhts', q.astype(F32), mem_k.astype(F32)) * (CA_HEAD_DIM ** -0.5)
    p = jax.nn.softmax(s, axis=-1)
    o = jnp.einsum('bhts,bshd->bthd', p, mem_v.astype(F32)).reshape(B, T, CA_DIM)
    return o.astype(h.dtype) @ w_ca_o


def _hmoe(h, w_router_group, b_router_group, w_router_expert, b_router_expert,
          w_exp_gate, w_exp_up, w_exp_down):
    B, T, _ = h.shape
    hf = h.astype(F32)
    pg = jax.nn.softmax(hf @ w_router_group.astype(F32) + b_router_group.astype(F32), axis=-1)
    gsel = jnp.argmax(pg, axis=-1)
    pg_top = jnp.max(pg, axis=-1, keepdims=True)
    le = (hf @ w_router_expert.astype(F32) + b_router_expert.astype(F32)).reshape(
        B, T, N_GROUPS, EXPERTS_PER_GROUP)
    le_sel = jnp.einsum('btge,btg->bte', le, jax.nn.one_hot(gsel, N_GROUPS, dtype=F32))
    pe = jax.nn.softmax(le_sel, axis=-1)
    top_v, top_i = lax.top_k(pe, TOP_K_IN_GROUP)
    top_v = top_v / jnp.sum(top_v, axis=-1, keepdims=True)
    eid = gsel[..., None] * EXPERTS_PER_GROUP + top_i
    gate = jnp.sum(jax.nn.one_hot(eid, N_EXPERTS, dtype=F32) * (pg_top * top_v)[..., None],
                   axis=-2)
    out = jnp.zeros((B, T, D_MODEL), F32)
    for grp in range(N_GROUPS):
        sl = slice(grp * EXPERTS_PER_GROUP, (grp + 1) * EXPERTS_PER_GROUP)
        a = jnp.einsum('btd,edf->btef', h, w_exp_gate[sl])
        u = jnp.einsum('btd,edf->btef', h, w_exp_up[sl])
        hid = (jax.nn.silu(a) * u * gate[..., sl, None].astype(h.dtype)).astype(h.dtype)
        out = out + jnp.einsum('btef,efd->btd', hid, w_exp_down[sl])
    return out.astype(h.dtype)


def _block(x, mem_k, mem_v, h0, buf, norm_mix_g, w_in, w_alpha_up, b_alpha, gla_norm_g,
           w_branch_a, conv_dw_w, conv_dw_b, conv_ln_g, conv_ln_b, w_branch_b, w_out,
           norm_ca_g, w_ca_q, w_ca_o, norm_ffn_g, w_router_group, b_router_group,
           w_router_expert, b_router_expert, w_exp_gate, w_exp_up, w_exp_down):
    y, h_T, new_buf = _mixer(_rmsnorm(x, norm_mix_g), h0, buf, w_in, w_alpha_up, b_alpha,
                             gla_norm_g, w_branch_a, conv_dw_w, conv_dw_b, conv_ln_g,
                             conv_ln_b, w_branch_b, w_out)
    x = x + y
    x = x + _cross_attn(_rmsnorm(x, norm_ca_g), mem_k, mem_v, w_ca_q, w_ca_o)
    x = x + _hmoe(_rmsnorm(x, norm_ffn_g), w_router_group, b_router_group, w_router_expert,
                  b_router_expert, w_exp_gate, w_exp_up, w_exp_down)
    return x, h_T, new_buf


def setup_inputs(seed: int = 0) -> dict:
    key = jax.random.key(seed)
    ks = iter(jax.random.split(key, 40))

    def nrm(shape, scale):
        return jax.random.normal(next(ks), shape, F32) * scale

    def gain(shape):
        return 1.0 + nrm(shape, 0.02)

    L = DEPTH
    return {
        'x_prompt': nrm((BATCH, SEQ, D_MODEL), 1.0),
        'x_sample': nrm((DEC_BATCH, DEC_SEQ, D_MODEL), 1.0),
        'mem_prompt': nrm((BATCH, N_MEM, D_MODEL), 1.0),
        'state_gla': nrm((L, DEC_BATCH, GLA_HEADS, GLA_DK, GLA_DV), 0.5),
        'state_conv': nrm((L, DEC_BATCH, CONV_WIDTH - 1, CONV_DIM), 0.5),
        'cache_mem_k': nrm((L, DEC_BATCH, N_MEM, CA_HEADS, CA_HEAD_DIM), 1.0),
        'cache_mem_v': nrm((L, DEC_BATCH, N_MEM, CA_HEADS, CA_HEAD_DIM), 1.0),
        'norm_mix_g': gain((L, D_MODEL)),
        'w_in': nrm((L, D_MODEL, IN_COLS), D_MODEL ** -0.5),
        'w_alpha_up': nrm((L, GLA_RANK, GLA_KEY), GLA_RANK ** -0.5),
        'b_alpha': nrm((L, GLA_KEY), 0.1),
        'gla_norm_g': gain((L, GLA_VAL)),
        'w_branch_a': nrm((L, GLA_VAL, D_MODEL), GLA_VAL ** -0.5),
        'conv_dw_w': nrm((L, CONV_WIDTH, CONV_DIM), CONV_WIDTH ** -0.5),
        'conv_dw_b': nrm((L, CONV_DIM), 0.02),
        'conv_ln_g': gain((L, CONV_DIM)),
        'conv_ln_b': nrm((L, CONV_DIM), 0.02),
        'w_branch_b': nrm((L, CONV_DIM, D_MODEL), CONV_DIM ** -0.5),
        'w_out': nrm((L, D_MODEL, D_MODEL), D_MODEL ** -0.5),
        'norm_ca_g': gain((L, D_MODEL)),
        'norm_mem_g': gain((L, D_MODEL)),
        'w_ca_q': nrm((L, D_MODEL, CA_DIM), D_MODEL ** -0.5),
        'w_ca_k': nrm((L, D_MODEL, CA_DIM), D_MODEL ** -0.5),
        'w_ca_v': nrm((L, D_MODEL, CA_DIM), D_MODEL ** -0.5),
        'w_ca_o': nrm((L, CA_DIM, D_MODEL), CA_DIM ** -0.5),
        'norm_ffn_g': gain((L, D_MODEL)),
        'w_router_group': nrm((L, D_MODEL, N_GROUPS), D_MODEL ** -0.5),
        'b_router_group': nrm((L, N_GROUPS), 0.01),
        'w_router_expert': nrm((L, D_MODEL, N_EXPERTS), D_MODEL ** -0.5),
        'b_router_expert': nrm((L, N_EXPERTS), 0.01),
        'w_exp_gate': nrm((L, N_EXPERTS, D_MODEL, D_EXPERT), D_MODEL ** -0.5),
        'w_exp_up': nrm((L, N_EXPERTS, D_MODEL, D_EXPERT), D_MODEL ** -0.5),
        'w_exp_down': nrm((L, N_EXPERTS, D_EXPERT, D_MODEL), D_EXPERT ** -0.5),
        'norm_final_g': gain((D_MODEL,)),
    }


def reference(x_prompt, x_sample, mem_prompt, state_gla, state_conv, cache_mem_k, cache_mem_v,
              norm_mix_g, w_in, w_alpha_up, b_alpha, gla_norm_g, w_branch_a, conv_dw_w,
              conv_dw_b, conv_ln_g, conv_ln_b, w_branch_b, w_out, norm_ca_g, norm_mem_g,
              w_ca_q, w_ca_k, w_ca_v, w_ca_o, norm_ffn_g, w_router_group, b_router_group,
              w_router_expert, b_router_expert, w_exp_gate, w_exp_up, w_exp_down,
              norm_final_g):
    bp = x_prompt.shape[0]
    xp, xs = x_prompt, x_sample
    gla_p, conv_p, mk_p, mv_p, gla_s, conv_s = [], [], [], [], [], []
    for l in range(DEPTH):
        lw = dict(norm_mix_g=norm_mix_g[l], w_in=w_in[l], w_alpha_up=w_alpha_up[l],
                  b_alpha=b_alpha[l], gla_norm_g=gla_norm_g[l], w_branch_a=w_branch_a[l],
                  conv_dw_w=conv_dw_w[l], conv_dw_b=conv_dw_b[l], conv_ln_g=conv_ln_g[l],
                  conv_ln_b=conv_ln_b[l], w_branch_b=w_branch_b[l], w_out=w_out[l],
                  norm_ca_g=norm_ca_g[l], w_ca_q=w_ca_q[l], w_ca_o=w_ca_o[l],
                  norm_ffn_g=norm_ffn_g[l], w_router_group=w_router_group[l],
                  b_router_group=b_router_group[l], w_router_expert=w_router_expert[l],
                  b_router_expert=b_router_expert[l], w_exp_gate=w_exp_gate[l],
                  w_exp_up=w_exp_up[l], w_exp_down=w_exp_down[l])
        mk, mv = _mem_kv(mem_prompt, norm_mem_g[l], w_ca_k[l], w_ca_v[l])
        h0_p = jnp.zeros((bp, GLA_HEADS, GLA_DK, GLA_DV), F32)
        buf_p = jnp.zeros((bp, CONV_WIDTH - 1, CONV_DIM), x_prompt.dtype)
        xp, hp, bfp = _block(xp, mk, mv, h0_p, buf_p, **lw)
        xs, hs, bfs = _block(xs, cache_mem_k[l], cache_mem_v[l], state_gla[l], state_conv[l], **lw)
        gla_p.append(hp.astype(state_gla.dtype))
        conv_p.append(bfp.astype(state_conv.dtype))
        mk_p.append(mk.astype(cache_mem_k.dtype))
        mv_p.append(mv.astype(cache_mem_v.dtype))
        gla_s.append(hs.astype(state_gla.dtype))
        conv_s.append(bfs.astype(state_conv.dtype))
    y_prompt = _rmsnorm(xp, norm_final_g)
    y_sample = _rmsnorm(xs, norm_final_g)
    return (y_prompt, y_sample, jnp.stack(gla_p), jnp.stack(conv_p), jnp.stack(mk_p),
            jnp.stack(mv_p), jnp.stack(gla_s), jnp.stack(conv_s))
```

```python
import functools
import math

import jax
import jax.numpy as jnp
from jax import lax
from jax.experimental import pallas as pl
from jax.experimental.pallas import tpu as pltpu

F32 = jnp.float32
BF16 = jnp.bfloat16

EPS = 1e-6
GLA_TAU = 16.0
GLA_CHUNK = 64
TOP_K_IN_GROUP = 2
LANES = 128
SUBLANES = 8
MIB = 1 << 20


def _pick(n, pref, mult):
    for d in range(min(pref, n), 0, -1):
        if n % d == 0 and d % mult == 0:
            return d
    return n


def _params(semantics, vmem_mib):
    return pltpu.CompilerParams(dimension_semantics=semantics, vmem_limit_bytes=vmem_mib * MIB)


def _sigmoid(x):
    return 1.0 / (1.0 + jnp.exp(-x))


def _silu(x):
    return x * _sigmoid(x)


def _dot(a, b):
    return jnp.dot(a, b, preferred_element_type=F32)


def _dot_nt(a, b):
    return lax.dot_general(a, b, (((1,), (1,)), ((), ())), preferred_element_type=F32)


def _dot_tn(a, b):
    return lax.dot_general(a, b, (((0,), (0,)), ((), ())), preferred_element_type=F32)


def _split_bf16(x, parts):
    out = []
    for _ in range(parts - 1):
        p = x.astype(BF16)
        out.append(p)
        x = x - p.astype(F32)
    out.append(x.astype(BF16))
    return out


def _rmsnorm_rows(x, g):
    return x * lax.rsqrt(jnp.mean(x * x, axis=-1, keepdims=True) + EPS) * g


def _rmsnorm_kernel(x_ref, g_ref, o_ref):
    o_ref[...] = _rmsnorm_rows(x_ref[...], g_ref[...]).astype(o_ref.dtype)


def _rmsnorm(x, g, out_dtype):
    t, d = x.shape
    tm = _pick(t, 128, 16)
    return pl.pallas_call(
        _rmsnorm_kernel,
        grid=(t // tm,),
        in_specs=[pl.BlockSpec((tm, d), lambda i: (i, 0)), pl.BlockSpec((1, d), lambda i: (0, 0))],
        out_specs=pl.BlockSpec((tm, d), lambda i: (i, 0)),
        out_shape=jax.ShapeDtypeStruct((t, d), out_dtype),
        compiler_params=_params(("parallel",), 32),
        name="rmsnorm",
    )(x, g.reshape(1, d))


def _ep_identity(accs, extras):
    return accs[0]


def _ep_glu(accs, extras):
    return accs[0] * _sigmoid(accs[1])


def _ep_residual(accs, extras):
    return extras[0] + accs[0]


def _ep_gated_merge(accs, extras):
    return _sigmoid(accs[2]) * accs[0] + _sigmoid(accs[3]) * accs[1]


def _mm_kernel(*refs, x_of_w, n_x, n_e, epilogue):
    n_w = len(x_of_w)
    xs = refs[:n_x]
    ws = refs[n_x:n_x + n_w]
    es = refs[n_x + n_w:n_x + n_w + n_e]
    o_ref = refs[-1]
    accs = [_dot(xs[x_of_w[i]][...], ws[i][...]) for i in range(n_w)]
    o_ref[...] = epilogue(accs, [e[...] for e in es]).astype(o_ref.dtype)


def _mm(xs, ws, x_of_w, epilogue, out_dtype, extras=(), tm_pref=512, tn_pref=512, name="mm"):
    t = xs[0].shape[0]
    n = ws[0].shape[1]
    tm = _pick(t, tm_pref, 16)
    tn = _pick(n, tn_pref, LANES)
    in_specs = [pl.BlockSpec((tm, x.shape[1]), lambda i, j: (i, 0)) for x in xs]
    in_specs += [pl.BlockSpec((w.shape[0], tn), lambda i, j: (0, j)) for w in ws]
    in_specs += [pl.BlockSpec((tm, tn), lambda i, j: (i, j)) for _ in extras]
    block_bytes = sum(tm * x.shape[1] * x.dtype.itemsize for x in xs)
    block_bytes += sum(w.shape[0] * tn * w.dtype.itemsize for w in ws)
    block_bytes += sum(tm * tn * e.dtype.itemsize for e in extras) + tm * tn * jnp.dtype(out_dtype).itemsize
    acc_bytes = len(ws) * tm * tn * 4
    vmem = (2 * block_bytes + 2 * acc_bytes) // MIB + 4
    return pl.pallas_call(
        functools.partial(_mm_kernel, x_of_w=tuple(x_of_w), n_x=len(xs), n_e=len(extras), epilogue=epilogue),
        grid=(t // tm, n // tn),
        in_specs=in_specs,
        out_specs=pl.BlockSpec((tm, tn), lambda i, j: (i, j)),
        out_shape=jax.ShapeDtypeStruct((t, n), out_dtype),
        compiler_params=_params(("parallel", "arbitrary"), vmem),
        name=name,
    )(*xs, *ws, *extras)


def _log_sigmoid(x):
    return jnp.minimum(x, 0.0) - jnp.log1p(jnp.exp(-jnp.abs(x)))


def _gla_kernel(*refs, heads, seq_rows, has_h0, scale):
    if has_h0:
        q_ref, k_ref, v_ref, r_ref, a_ref, wup_ref, bal_ref, gn_ref, h0_ref, og_ref, st_ref = refs
    else:
        q_ref, k_ref, v_ref, r_ref, a_ref, wup_ref, bal_ref, gn_ref, og_ref, st_ref = refs
    rows, key = q_ref.shape
    val = v_ref.shape[1]
    dk, dv = key // heads, val // heads
    nseq = rows // seq_rows

    @pl.when(pl.program_id(1) == 0)
    def _():
        if has_h0:
            st_ref[...] = h0_ref[...]
        else:
            st_ref[...] = jnp.zeros(st_ref.shape, F32)

    la = _dot(a_ref[...].astype(BF16), wup_ref[...]) + bal_ref[...]
    log_a = _log_sigmoid(la) * (1.0 / GLA_TAU)

    ri = lax.broadcasted_iota(jnp.int32, (rows, rows), 0)
    ci = lax.broadcasted_iota(jnp.int32, (rows, rows), 1)
    causal = ci <= ri
    if nseq > 1:
        shift = seq_rows.bit_length() - 1
        causal = jnp.logical_and(causal, (ri >> shift) == (ci >> shift))
        row_seq = lax.broadcasted_iota(jnp.int32, (rows, 1), 0) >> shift
    tri = jnp.where(causal, 1.0, 0.0).astype(BF16)
    b = sum(_dot(tri, part) for part in _split_bf16(log_a, 3))

    q = q_ref[...] * scale
    k = k_ref[...]
    v = v_ref[...]
    r = r_ref[...]
    gn = gn_ref[...]
    for h in range(heads):
        ks = slice(h * dk, (h + 1) * dk)
        vs = slice(h * dv, (h + 1) * dv)
        bh = b[:, ks]
        q_e = (q[:, ks] * jnp.exp(bh)).astype(BF16)
        k_e = (k[:, ks] * jnp.exp(-bh)).astype(BF16)
        vh = v[:, vs].astype(BF16)
        scores = jnp.where(causal, _dot_nt(q_e, k_e), 0.0)
        o = _dot(scores.astype(BF16), vh)
        for s in range(nseq):
            b_last = bh[(s + 1) * seq_rows - 1:(s + 1) * seq_rows, :]
            k_s = k[:, ks] * jnp.exp(b_last - bh)
            if nseq > 1:
                in_seq = row_seq == s
                k_s = jnp.where(in_seq, k_s, 0.0)
            state = st_ref[s, h]
            o_inter = _dot(q_e, state.astype(BF16))
            o = o + (jnp.where(in_seq, o_inter, 0.0) if nseq > 1 else o_inter)
            d_state = _dot_tn(k_s.astype(BF16), vh)
            decay_col = jnp.broadcast_to(jnp.exp(b_last), (LANES, dk)).T[:, 0:1]
            st_ref[s, h] = decay_col * state + d_state
        mu = jnp.mean(o, axis=-1, keepdims=True)
        oc = o - mu
        o_n = oc * lax.rsqrt(jnp.mean(oc * oc, axis=-1, keepdims=True) + EPS) * gn[:, vs]
        og_ref[:, vs] = (o_n * _silu(r[:, vs])).astype(og_ref.dtype)


def _gla(z, a_low, w_up, b_alpha, gn, h0, *, row0, n_seq, seq_len, heads, dk, dv, name):
    key, val = heads * dk, heads * dv
    assert val == 2 * key and z.shape[1] == 2 * key + 2 * val
    chunk = min(GLA_CHUNK, seq_len)
    assert seq_len % chunk == 0
    if chunk % SUBLANES == 0:
        rows, n_chunks = chunk, seq_len // chunk
    else:
        assert SUBLANES % seq_len == 0 and chunk == seq_len
        rows, n_chunks = SUBLANES, 1
    nseq = rows // chunk
    assert n_seq % nseq == 0 and row0 % rows == 0
    blk0 = row0 // rows
    row_map = lambda g, c: blk0 + g * n_chunks + c
    in_specs = [
        pl.BlockSpec((rows, key), lambda g, c: (row_map(g, c), 0)),
        pl.BlockSpec((rows, key), lambda g, c: (row_map(g, c), 1)),
        pl.BlockSpec((rows, val), lambda g, c: (row_map(g, c), 1)),
        pl.BlockSpec((rows, val), lambda g, c: (row_map(g, c), 2)),
        pl.BlockSpec((rows, LANES), lambda g, c: (row_map(g, c), 0)),
        pl.BlockSpec((LANES, key), lambda g, c: (0, 0)),
        pl.BlockSpec((1, key), lambda g, c: (0, 0)),
        pl.BlockSpec((1, val), lambda g, c: (0, 0)),
    ]
    args = [z, z, z, z, a_low, w_up, b_alpha.reshape(1, key), gn.reshape(1, val)]
    state_spec = pl.BlockSpec((nseq, heads, dk, dv), lambda g, c: (g, 0, 0, 0))
    if h0 is not None:
        in_specs.append(state_spec)
        args.append(h0)
    n_rows = n_seq * seq_len
    return pl.pallas_call(
        functools.partial(_gla_kernel, heads=heads, seq_rows=chunk, has_h0=h0 is not None, scale=dk ** -0.5),
        grid=(n_seq // nseq, n_chunks),
        in_specs=in_specs,
        out_specs=[pl.BlockSpec((rows, val), lambda g, c: (g * n_chunks + c, 0)), state_spec],
        out_shape=[jax.ShapeDtypeStruct((n_rows, val), BF16),
                   jax.ShapeDtypeStruct((n_seq, heads, dk, dv), F32)],
        compiler_params=_params(("parallel", "arbitrary"), 48),
        name=name,
    )(*args)


CONV_LANES = 512
CONV_ROWS = 32
HIST_ROWS = 32


def _ln_silu(x, g, b):
    mu = jnp.mean(x, axis=-1, keepdims=True)
    xc = x - mu
    var = jnp.mean(xc * xc, axis=-1, keepdims=True)
    return _silu(xc * lax.rsqrt(var + EPS) * g + b)


def _conv_strip(ext_ref, w_ref, row_start, n_rows, lanes, width):
    acc = jnp.zeros((n_rows, lanes.stop - lanes.start), F32)
    for j in range(width):
        acc = acc + ext_ref[row_start + j:row_start + j + n_rows, lanes] * w_ref[j:j + 1, lanes]
    return acc


def _conv_prompt_kernel(u_ref, w_ref, wb_ref, g_ref, b_ref, c_ref, ext_ref, conv_ref, *, width):
    tb, cd = u_ref.shape
    first = HIST_ROWS - (width - 1)

    @pl.when(pl.program_id(1) == 0)
    def _():
        ext_ref[0:HIST_ROWS, :] = jnp.zeros((HIST_ROWS, cd), F32)

    ext_ref[HIST_ROWS:HIST_ROWS + tb, :] = u_ref[...]
    for r0 in range(0, tb, CONV_ROWS):
        for c0 in range(0, cd, CONV_LANES):
            lanes = slice(c0, c0 + CONV_LANES)
            conv_ref[r0:r0 + CONV_ROWS, lanes] = _conv_strip(ext_ref, w_ref, first + r0, CONV_ROWS, lanes, width)
    ext_ref[0:HIST_ROWS, :] = ext_ref[tb:tb + HIST_ROWS, :]
    c_ref[...] = _ln_silu(conv_ref[...] + wb_ref[...], g_ref[...], b_ref[...]).astype(c_ref.dtype)


def _conv_prompt(ug, w, wb, g, b, *, n_seq, seq_len):
    width, cd = w.shape
    assert width - 1 <= HIST_ROWS and cd % CONV_LANES == 0
    tb = _pick(seq_len, 64, CONV_ROWS)
    assert tb % CONV_ROWS == 0 and tb >= HIST_ROWS
    n_blk = seq_len // tb
    vec = pl.BlockSpec((1, cd), lambda s, t: (0, 0))
    return pl.pallas_call(
        functools.partial(_conv_prompt_kernel, width=width),
        grid=(n_seq, n_blk),
        in_specs=[pl.BlockSpec((tb, cd), lambda s, t: (s * n_blk + t, 0)),
                  pl.BlockSpec((width, cd), lambda s, t: (0, 0)), vec, vec, vec],
        out_specs=pl.BlockSpec((tb, cd), lambda s, t: (s * n_blk + t, 0)),
        out_shape=jax.ShapeDtypeStruct((n_seq * seq_len, cd), BF16),
        scratch_shapes=[pltpu.VMEM((HIST_ROWS + tb, cd), F32), pltpu.VMEM((tb, cd), F32)],
        compiler_params=_params(("parallel", "arbitrary"), 32),
        name="conv_prompt",
    )(ug, w, wb.reshape(1, cd), g.reshape(1, cd), b.reshape(1, cd))


def _conv_sample_kernel(u_ref, buf_ref, w_ref, wb_ref, g_ref, b_ref, c_ref, nbuf_ref, ext_ref, conv_ref,
                        *, width, seq_len):
    n_seq, hist, cd = buf_ref.shape
    pad_rows = ext_ref.shape[0] - hist - seq_len
    for s in range(n_seq):
        ext_ref[0:hist, :] = buf_ref[s]
        ext_ref[hist:hist + seq_len, :] = u_ref[s * seq_len:(s + 1) * seq_len, :]
        ext_ref[hist + seq_len:, :] = jnp.zeros((pad_rows, cd), F32)
        for c0 in range(0, cd, CONV_LANES):
            lanes = slice(c0, c0 + CONV_LANES)
            acc = _conv_strip(ext_ref, w_ref, 0, SUBLANES, lanes, width)
            conv_ref[s * seq_len:(s + 1) * seq_len, lanes] = acc[0:seq_len, :]
        nbuf_ref[s] = ext_ref[seq_len:seq_len + hist, :]
    c_ref[...] = _ln_silu(conv_ref[...] + wb_ref[...], g_ref[...], b_ref[...]).astype(c_ref.dtype)


def _conv_sample(ug, buf, w, wb, g, b, *, row0, seq_len):
    width, cd = w.shape
    n_seq, hist, _ = buf.shape
    assert hist == width - 1 and seq_len <= SUBLANES and cd % CONV_LANES == 0
    sb = _pick(n_seq, 8, 1)
    rows = sb * seq_len
    assert rows % 16 == 0 and row0 % rows == 0
    blk0 = row0 // rows
    ext_rows = -(-(hist + SUBLANES) // SUBLANES) * SUBLANES
    vec = pl.BlockSpec((1, cd), lambda i: (0, 0))
    return pl.pallas_call(
        functools.partial(_conv_sample_kernel, width=width, seq_len=seq_len),
        grid=(n_seq // sb,),
        in_specs=[pl.BlockSpec((rows, cd), lambda i: (blk0 + i, 0)),
                  pl.BlockSpec((sb, hist, cd), lambda i: (i, 0, 0)),
                  pl.BlockSpec((width, cd), lambda i: (0, 0)), vec, vec, vec],
        out_specs=[pl.BlockSpec((rows, cd), lambda i: (i, 0)),
                   pl.BlockSpec((sb, hist, cd), lambda i: (i, 0, 0))],
        out_shape=[jax.ShapeDtypeStruct((n_seq * seq_len, cd), BF16),
                   jax.ShapeDtypeStruct((n_seq, hist, cd), F32)],
        scratch_shapes=[pltpu.VMEM((ext_rows, cd), F32), pltpu.VMEM((rows, cd), F32)],
        compiler_params=_params(("parallel",), 32),
        name="conv_sample",
    )(ug, buf, w, wb.reshape(1, cd), g.reshape(1, cd), b.reshape(1, cd))


def _attn_kernel(q_ref, k_ref, v_ref, o_ref, *, heads, seq_rows, scale):
    rows, ca = q_ref.shape
    nseq = rows // seq_rows
    n_mem = k_ref.shape[0] // nseq
    hd = ca // heads
    q = q_ref[...]
    if nseq > 1:
        shift = seq_rows.bit_length() - 1
        row_seq = lax.broadcasted_iota(jnp.int32, (rows, 1), 0) >> shift
    for h in range(heads):
        hs = slice(h * hd, (h + 1) * hd)
        out = None
        for s in range(nseq):
            ms = slice(s * n_mem, (s + 1) * n_mem)
            sc = _dot_nt(q[:, hs], k_ref[ms, hs].astype(BF16)) * scale
            e = jnp.exp(sc - jnp.max(sc, axis=-1, keepdims=True))
            p = e / jnp.sum(e, axis=-1, keepdims=True)
            o = _dot(p.astype(BF16), v_ref[ms, hs].astype(BF16))
            out = o if out is None else jnp.where(row_seq == s, o, out)
        o_ref[:, hs] = out.astype(o_ref.dtype)


def _attn(q, k, v, *, row0, n_seq, seq_len, n_mem, heads, name):
    ca = q.shape[1]
    if seq_len % SUBLANES == 0:
        rows, nseq = _pick(seq_len, 256, 16), 1
    else:
        assert SUBLANES % seq_len == 0
        rows, nseq = 16, 16 // seq_len
    assert row0 % rows == 0 and (n_seq * seq_len) % rows == 0
    blk0 = row0 // rows
    per_seq = max(seq_len // rows, 1)
    kv_spec = pl.BlockSpec((nseq * n_mem, ca), lambda i: (i // per_seq, 0))
    return pl.pallas_call(
        functools.partial(_attn_kernel, heads=heads, seq_rows=seq_len if nseq > 1 else rows,
                          scale=(ca // heads) ** -0.5),
        grid=(n_seq * seq_len // rows,),
        in_specs=[pl.BlockSpec((rows, ca), lambda i: (blk0 + i, 0)), kv_spec, kv_spec],
        out_specs=pl.BlockSpec((rows, ca), lambda i: (i, 0)),
        out_shape=jax.ShapeDtypeStruct((n_seq * seq_len, ca), BF16),
        compiler_params=_params(("parallel",), 32),
        name=name,
    )(q, k, v)


ROUTE_E0, ROUTE_E1, ROUTE_G0, ROUTE_G1 = 0, 1, 2, 3


def _router_kernel(x_ref, g_ref, wg_ref, bg_ref, we_ref, be_ref, o_ref, *, n_groups, per_group):
    h = _rmsnorm_rows(x_ref[...], g_ref[...])
    h_hi, h_lo = _split_bf16(h, 2)

    def logits(w_ref, b_ref):
        w_hi, w_lo = _split_bf16(w_ref[...], 2)
        return _dot(h_hi, w_hi) + (_dot(h_hi, w_lo) + _dot(h_lo, w_hi)) + b_ref[...]

    lg = logits(wg_ref, bg_ref)
    le = logits(we_ref, be_ref)
    lane_i = lax.broadcasted_iota(jnp.int32, lg.shape, 1)
    lane = lane_i.astype(F32)
    neg = jnp.float32(-jnp.inf)

    def first_argmax(x, m):
        return jnp.min(jnp.where(x == m, lane, float(LANES)), axis=-1, keepdims=True)

    lg = jnp.where(lane_i < n_groups, lg, neg)
    mg = jnp.max(lg, axis=-1, keepdims=True)
    pg_top = 1.0 / jnp.sum(jnp.exp(lg - mg), axis=-1, keepdims=True)
    gsel = first_argmax(lg, mg)
    shift = per_group.bit_length() - 1
    le = jnp.where((lane_i >> shift).astype(F32) == gsel, le, neg)
    m0 = jnp.max(le, axis=-1, keepdims=True)
    z = jnp.sum(jnp.exp(le - m0), axis=-1, keepdims=True)
    e0 = first_argmax(le, m0)
    le1 = jnp.where(lane == e0, neg, le)
    m1 = jnp.max(le1, axis=-1, keepdims=True)
    e1 = first_argmax(le1, m1)
    p0 = 1.0 / z
    p1 = jnp.exp(m1 - m0) / z
    den = p0 + p1
    g0 = pg_top * (p0 / den)
    g1 = pg_top * (p1 / den)
    rec = jnp.where(lane_i == ROUTE_E0, e0, 0.0)
    rec = jnp.where(lane_i == ROUTE_E1, e1, rec)
    rec = jnp.where(lane_i == ROUTE_G0, g0, rec)
    rec = jnp.where(lane_i == ROUTE_G1, g1, rec)
    o_ref[...] = rec


def _router(x, g, w_group, b_group, w_expert, b_expert):
    t, d = x.shape
    n_groups, n_experts = w_group.shape[1], w_expert.shape[1]
    per_group = n_experts // n_groups
    assert n_experts <= LANES and per_group & (per_group - 1) == 0
    pad = lambda a: jnp.pad(a, ((0, 0), (0, LANES - a.shape[1])))
    tm = _pick(t, 128, SUBLANES)
    wspec = pl.BlockSpec((d, LANES), lambda i: (0, 0))
    bspec = pl.BlockSpec((1, LANES), lambda i: (0, 0))
    return pl.pallas_call(
        functools.partial(_router_kernel, n_groups=n_groups, per_group=per_group),
        grid=(t // tm,),
        in_specs=[pl.BlockSpec((tm, d), lambda i: (i, 0)), pl.BlockSpec((1, d), lambda i: (0, 0)),
                  wspec, bspec, wspec, bspec],
        out_specs=pl.BlockSpec((tm, LANES), lambda i: (i, 0)),
        out_shape=jax.ShapeDtypeStruct((t, LANES), F32),
        compiler_params=_params(("parallel",), 32),
        name="router",
    )(x, g.reshape(1, d), pad(w_group), pad(b_group.reshape(1, -1)), pad(w_expert), pad(b_expert.reshape(1, -1)))


def _dispatch_plan(e_ids, n_experts, tile):
    t, k = e_ids.shape
    n_pairs = t * k
    n_tiles = n_pairs // tile + n_experts
    flat = e_ids.reshape(n_pairs)
    onehot = (flat[:, None] == jnp.arange(n_experts, dtype=jnp.int32)[None, :]).astype(jnp.int32)
    counts = jnp.sum(onehot, axis=0)
    tiles_per = (counts + tile - 1) // tile
    tile_end = jnp.cumsum(tiles_per)
    tile_start = tile_end - tiles_per
    rank = jnp.take_along_axis(jnp.cumsum(onehot, axis=0), flat[:, None], axis=1)[:, 0] - 1
    pos = tile_start[flat] * tile + rank
    token = jnp.arange(n_pairs, dtype=jnp.int32) // k
    src = jnp.zeros((n_tiles * tile,), jnp.int32).at[pos].set(token)
    n_used = tile_end[-1]
    tile_ids = jnp.minimum(jnp.arange(n_tiles, dtype=jnp.int32), n_used - 1)
    tile_expert = jnp.searchsorted(tile_end, tile_ids, side="right").astype(jnp.int32)
    return pos.reshape(t, k), src.reshape(n_tiles, 1, tile), tile_expert, n_used.reshape(1).astype(jnp.int32)


def _gather_rows(src_hbm, idx_ref, dst_ref, sem, n_rows):
    def copy(r):
        return pltpu.make_async_copy(src_hbm.at[pl.ds(idx_ref[0, 0, r], 1), :], dst_ref.at[pl.ds(r, 1), :], sem)

    def start(r, carry):
        copy(r).start()
        return carry

    def wait(r, carry):
        copy(r).wait()
        return carry

    lax.fori_loop(0, n_rows, start, 0)
    lax.fori_loop(0, n_rows, wait, 0)


def _experts_kernel(texp_ref, nused_ref, idx_ref, x_hbm, g_ref, wg_ref, wu_ref, wd_ref, y_ref, xbuf, sem):
    i = pl.program_id(0)

    @pl.when(i < nused_ref[0])
    def _():
        tile = xbuf.shape[0]
        _gather_rows(x_hbm, idx_ref, xbuf, sem, tile)
        h = _rmsnorm_rows(xbuf[...], g_ref[...]).astype(BF16)
        a = _dot(h, wg_ref[0])
        u = _dot(h, wu_ref[0])
        hid = (_silu(a) * u).astype(BF16)
        y_ref[...] = _dot(hid, wd_ref[0])

    @pl.when(i >= nused_ref[0])
    def _():
        y_ref[...] = jnp.zeros(y_ref.shape, F32)


def _experts(x, g, src, tile_expert, n_used, wg, wu, wd):
    t, d = x.shape
    n_tiles, _, tile = src.shape
    de = wg.shape[2]
    last = lambda i, texp, nused: jnp.minimum(i, nused[0] - 1)
    grid_spec = pltpu.PrefetchScalarGridSpec(
        num_scalar_prefetch=2,
        grid=(n_tiles,),
        in_specs=[
            pl.BlockSpec((1, 1, tile), lambda i, texp, nused: (last(i, texp, nused), 0, 0),
                         memory_space=pltpu.SMEM),
            pl.BlockSpec(memory_space=pl.ANY),
            pl.BlockSpec((1, d), lambda i, texp, nused: (0, 0)),
            pl.BlockSpec((1, d, de), lambda i, texp, nused: (texp[i], 0, 0)),
            pl.BlockSpec((1, d, de), lambda i, texp, nused: (texp[i], 0, 0)),
            pl.BlockSpec((1, de, d), lambda i, texp, nused: (texp[i], 0, 0)),
        ],
        out_specs=pl.BlockSpec((tile, d), lambda i, texp, nused: (i, 0)),
        scratch_shapes=[pltpu.VMEM((tile, d), F32), pltpu.SemaphoreType.DMA(())],
    )
    return pl.pallas_call(
        _experts_kernel,
        grid_spec=grid_spec,
        out_shape=jax.ShapeDtypeStruct((n_tiles * tile, d), F32),
        compiler_params=_params(("arbitrary",), 48),
        name="experts",
    )(tile_expert, n_used, src, x, g.reshape(1, d), wg, wu, wd)


def _combine_kernel(idx0_ref, idx1_ref, y_hbm, x_ref, route_ref, gf_ref, *refs, n_first, final_norm):
    outs, (buf0, buf1, sem) = refs[:-3], refs[-3:]
    tm = x_ref.shape[0]
    _gather_rows(y_hbm, idx0_ref, buf0, sem, tm)
    _gather_rows(y_hbm, idx1_ref, buf1, sem, tm)
    route = route_ref[...]
    g0 = route[:, ROUTE_G0:ROUTE_G0 + 1]
    g1 = route[:, ROUTE_G1:ROUTE_G1 + 1]
    res = x_ref[...] + (g0 * buf0[...] + g1 * buf1[...])
    if final_norm:
        res = _rmsnorm_rows(res, gf_ref[...])
    if len(outs) == 1:
        outs[0][...] = res
    else:
        i = pl.program_id(0)

        @pl.when(i < n_first)
        def _():
            outs[0][...] = res

        @pl.when(i >= n_first)
        def _():
            outs[1][...] = res


def _combine(x, y_sorted, pos, route, g_final, *, split_rows, final_norm):
    t, d = x.shape
    tm = _pick(t if split_rows is None else math.gcd(split_rows, t - split_rows), 128, SUBLANES)
    n_blk = t // tm
    idx = [pos[:, k].reshape(n_blk, 1, tm) for k in range(TOP_K_IN_GROUP)]
    smem = pl.BlockSpec((1, 1, tm), lambda i: (i, 0, 0), memory_space=pltpu.SMEM)
    row = pl.BlockSpec((tm, d), lambda i: (i, 0))
    if split_rows is None:
        n_first = n_blk
        out_specs = [row]
        out_shape = [jax.ShapeDtypeStruct((t, d), F32)]
    else:
        n_first = split_rows // tm
        out_specs = [pl.BlockSpec((tm, d), lambda i: (jnp.minimum(i, n_first - 1), 0)),
                     pl.BlockSpec((tm, d), lambda i: (jnp.maximum(i - n_first, 0), 0))]
        out_shape = [jax.ShapeDtypeStruct((split_rows, d), F32), jax.ShapeDtypeStruct((t - split_rows, d), F32)]
    return pl.pallas_call(
        functools.partial(_combine_kernel, n_first=n_first, final_norm=final_norm),
        grid=(n_blk,),
        in_specs=[smem, smem, pl.BlockSpec(memory_space=pl.ANY), row,
                  pl.BlockSpec((tm, LANES), lambda i: (i, 0)), pl.BlockSpec((1, d), lambda i: (0, 0))],
        out_specs=out_specs,
        out_shape=out_shape,
        scratch_shapes=[pltpu.VMEM((tm, d), F32), pltpu.VMEM((tm, d), F32), pltpu.SemaphoreType.DMA(())],
        compiler_params=_params(("arbitrary",), 32),
        name="combine",
    )(idx[0], idx[1], y_sorted, x, route, g_final.reshape(1, d))


def kernel(x_prompt, x_sample, mem_prompt, state_gla, state_conv, cache_mem_k, cache_mem_v, norm_mix_g, w_in, w_alpha_up, b_alpha, gla_norm_g, w_branch_a, conv_dw_w, conv_dw_b, conv_ln_g, conv_ln_b, w_branch_b, w_out, norm_ca_g, norm_mem_g, w_ca_q, w_ca_k, w_ca_v, w_ca_o, norm_ffn_g, w_router_group, b_router_group, w_router_expert, b_router_expert, w_exp_gate, w_exp_up, w_exp_down, norm_final_g):
    depth = w_in.shape[0]
    bp, tp, d = x_prompt.shape
    bs, ts, _ = x_sample.shape
    heads, dk, dv = state_gla.shape[2:]
    key, val = heads * dk, heads * dv
    rank = w_alpha_up.shape[1]
    cd = state_conv.shape[3]
    n_mem, ca_heads, ca_hd = cache_mem_k.shape[2:]
    ca = ca_heads * ca_hd
    n_experts = w_router_expert.shape[2]
    rows_p, rows_s = bp * tp, bs * ts
    off_a = 2 * key + 2 * val
    off_u = off_a + rank
    off_g = off_u + 2 * cd
    assert rank <= LANES and w_in.shape[2] == off_g + 2 * d
    bf = lambda a: a.astype(BF16)

    x = jnp.concatenate([x_prompt.reshape(rows_p, d), x_sample.reshape(rows_s, d)], axis=0)
    outs = dict(gla_p=[], conv_p=[], mk_p=[], mv_p=[], gla_s=[], conv_s=[])
    for l in range(depth):
        wl = w_in[l]
        h = _rmsnorm(x, norm_mix_g[l], BF16)
        z = _mm([h], [bf(wl[:, :off_a])], [0], _ep_identity, F32, name="in_qkvr")
        a_low = _mm([h], [bf(jnp.pad(wl[:, off_a:off_u], ((0, 0), (0, LANES - rank))))], [0], _ep_identity, F32,
                    name="in_alow")
        ug = _mm([h, h], [bf(wl[:, off_u:off_u + cd]), bf(wl[:, off_u + cd:off_g])], [0, 0], _ep_glu, F32,
                 name="in_glu")
        w_up = bf(jnp.pad(w_alpha_up[l], ((0, LANES - rank), (0, 0))))
        gla_args = dict(heads=heads, dk=dk, dv=dv)
        og_p, st_p = _gla(z, a_low, w_up, b_alpha[l], gla_norm_g[l], None, row0=0, n_seq=bp, seq_len=tp,
                          name="gla_prompt", **gla_args)
        og_s, st_s = _gla(z, a_low, w_up, b_alpha[l], gla_norm_g[l], state_gla[l], row0=rows_p, n_seq=bs,
                          seq_len=ts, name="gla_sample", **gla_args)
        conv_w = (conv_dw_w[l], conv_dw_b[l], conv_ln_g[l], conv_ln_b[l])
        c_p = _conv_prompt(ug, *conv_w, n_seq=bp, seq_len=tp)
        c_s, buf_s = _conv_sample(ug, state_conv[l], *conv_w, row0=rows_p, seq_len=ts)
        og = jnp.concatenate([og_p, og_s], axis=0)
        c = jnp.concatenate([c_p, c_s], axis=0)
        merged = _mm([og, c, h], [bf(w_branch_a[l]), bf(w_branch_b[l]), bf(wl[:, off_g:off_g + d]),
                                  bf(wl[:, off_g + d:])], [0, 1, 2, 2], _ep_gated_merge, BF16,
                     tn_pref=256, name="merge")
        x = _mm([merged], [bf(w_out[l])], [0], _ep_residual, F32, extras=[x], name="out_proj")
        h = _rmsnorm(x, norm_ca_g[l], BF16)
        q = _mm([h], [bf(w_ca_q[l])], [0], _ep_identity, BF16, name="ca_q")
        m = _rmsnorm(mem_prompt.reshape(bp * n_mem, d), norm_mem_g[l], BF16)
        mk = _mm([m], [bf(w_ca_k[l])], [0], _ep_identity, F32, name="mem_k")
        mv = _mm([m], [bf(w_ca_v[l])], [0], _ep_identity, F32, name="mem_v")
        ao_p = _attn(q, mk, mv, row0=0, n_seq=bp, seq_len=tp, n_mem=n_mem, heads=ca_heads, name="attn_prompt")
        ao_s = _attn(q, cache_mem_k[l].reshape(bs * n_mem, ca), cache_mem_v[l].reshape(bs * n_mem, ca),
                     row0=rows_p, n_seq=bs, seq_len=ts, n_mem=n_mem, heads=ca_heads, name="attn_sample")
        ao = jnp.concatenate([ao_p, ao_s], axis=0)
        x = _mm([ao], [bf(w_ca_o[l])], [0], _ep_residual, F32, extras=[x], name="ca_out")
        route = _router(x, norm_ffn_g[l], w_router_group[l], b_router_group[l], w_router_expert[l],
                        b_router_expert[l])
        e_ids = route[:, ROUTE_E0:ROUTE_E1 + 1].astype(jnp.int32)
        tile = _pick(rows_p + rows_s, 256, SUBLANES)
        pos, src, tile_expert, n_used = _dispatch_plan(e_ids, n_experts, tile)
        y_sorted = _experts(x, norm_ffn_g[l], src, tile_expert, n_used, bf(w_exp_gate[l]), bf(w_exp_up[l]),
                            bf(w_exp_down[l]))
        last = l == depth - 1
        res = _combine(x, y_sorted, pos, route, norm_final_g, split_rows=rows_p if last else None,
                       final_norm=last)
        if not last:
            x = res[0]
        outs["gla_p"].append(st_p)
        outs["conv_p"].append(ug[:rows_p].reshape(bp, tp, cd)[:, tp - (conv_dw_w.shape[1] - 1):, :])
        outs["mk_p"].append(mk.reshape(bp, n_mem, ca_heads, ca_hd))
        outs["mv_p"].append(mv.reshape(bp, n_mem, ca_heads, ca_hd))
        outs["gla_s"].append(st_s)
        outs["conv_s"].append(buf_s)
    y_prompt = res[0].reshape(bp, tp, d)
    y_sample = res[1].reshape(bs, ts, d)
    return (y_prompt, y_sample, jnp.stack(outs["gla_p"]), jnp.stack(outs["conv_p"]), jnp.stack(outs["mk_p"]),
            jnp.stack(outs["mv_p"]), jnp.stack(outs["gla_s"]), jnp.stack(outs["conv_s"]))
```

```python
import functools
import math

import jax
import jax.numpy as jnp
from jax import lax
from jax.experimental import pallas as pl
from jax.experimental.pallas import tpu as pltpu

F32 = jnp.float32
BF16 = jnp.bfloat16

EPS = 1e-6
GLA_TAU = 16.0
GLA_CHUNK = 64
TOP_K_IN_GROUP = 2
LANES = 128
SUBLANES = 8
MIB = 1 << 20


def _pick(n, pref, mult):
    for d in range(min(pref, n), 0, -1):
        if n % d == 0 and d % mult == 0:
            return d
    return n


def _params(semantics, vmem_mib):
    return pltpu.CompilerParams(dimension_semantics=semantics, vmem_limit_bytes=vmem_mib * MIB)


def _sigmoid(x):
    return 1.0 / (1.0 + jnp.exp(-x))


def _silu(x):
    return x * _sigmoid(x)


def _dot(a, b):
    return jnp.dot(a, b, preferred_element_type=F32)


def _dot_nt(a, b):
    return lax.dot_general(a, b, (((1,), (1,)), ((), ())), preferred_element_type=F32)


def _dot_tn(a, b):
    return lax.dot_general(a, b, (((0,), (0,)), ((), ())), preferred_element_type=F32)


def _split_bf16(x, parts):
    out = []
    for _ in range(parts - 1):
        p = x.astype(BF16)
        out.append(p)
        x = x - p.astype(F32)
    out.append(x.astype(BF16))
    return out


def _rmsnorm_rows(x, g):
    return x * lax.rsqrt(jnp.mean(x * x, axis=-1, keepdims=True) + EPS) * g


def _rmsnorm_kernel(x_ref, g_ref, o_ref):
    o_ref[...] = _rmsnorm_rows(x_ref[...], g_ref[...]).astype(o_ref.dtype)


def _rmsnorm(x, g, out_dtype):
    t, d = x.shape
    tm = _pick(t, 128, 16)
    return pl.pallas_call(
        _rmsnorm_kernel,
        grid=(t // tm,),
        in_specs=[pl.BlockSpec((tm, d), lambda i: (i, 0)), pl.BlockSpec((1, d), lambda i: (0, 0))],
        out_specs=pl.BlockSpec((tm, d), lambda i: (i, 0)),
        out_shape=jax.ShapeDtypeStruct((t, d), out_dtype),
        compiler_params=_params(("parallel",), 32),
        name="rmsnorm",
    )(x, g.reshape(1, d))


def _ep_identity(accs, extras):
    return accs[0]


def _ep_glu(accs, extras):
    return accs[0] * _sigmoid(accs[1])


def _ep_residual(accs, extras):
    return extras[0] + accs[0]


def _ep_gated_merge(accs, extras):
    return _sigmoid(accs[2]) * accs[0] + _sigmoid(accs[3]) * accs[1]


def _mm_kernel(*refs, x_of_w, n_x, n_e, epilogue):
    n_w = len(x_of_w)
    xs = refs[:n_x]
    ws = refs[n_x:n_x + n_w]
    es = refs[n_x + n_w:n_x + n_w + n_e]
    o_ref = refs[-1]
    accs = [_dot(xs[x_of_w[i]][...], ws[i][...]) for i in range(n_w)]
    o_ref[...] = epilogue(accs, [e[...] for e in es]).astype(o_ref.dtype)


def _mm(xs, ws, x_of_w, epilogue, out_dtype, extras=(), tm_pref=512, tn_pref=512, name="mm"):
    t = xs[0].shape[0]
    n = ws[0].shape[1]
    tm = _pick(t, tm_pref, 16)
    tn = _pick(n, tn_pref, LANES)
    in_specs = [pl.BlockSpec((tm, x.shape[1]), lambda i, j: (i, 0)) for x in xs]
    in_specs += [pl.BlockSpec((w.shape[0], tn), lambda i, j: (0, j)) for w in ws]
    in_specs += [pl.BlockSpec((tm, tn), lambda i, j: (i, j)) for _ in extras]
    block_bytes = sum(tm * x.shape[1] * x.dtype.itemsize for x in xs)
    block_bytes += sum(w.shape[0] * tn * w.dtype.itemsize for w in ws)
    block_bytes += sum(tm * tn * e.dtype.itemsize for e in extras) + tm * tn * jnp.dtype(out_dtype).itemsize
    acc_bytes = len(ws) * tm * tn * 4
    vmem = (2 * block_bytes + 2 * acc_bytes) // MIB + 4
    return pl.pallas_call(
        functools.partial(_mm_kernel, x_of_w=tuple(x_of_w), n_x=len(xs), n_e=len(extras), epilogue=epilogue),
        grid=(t // tm, n // tn),
        in_specs=in_specs,
        out_specs=pl.BlockSpec((tm, tn), lambda i, j: (i, j)),
        out_shape=jax.ShapeDtypeStruct((t, n), out_dtype),
        compiler_params=_params(("parallel", "arbitrary"), vmem),
        name=name,
    )(*xs, *ws, *extras)


def _log_sigmoid(x):
    return jnp.minimum(x, 0.0) - jnp.log1p(jnp.exp(-jnp.abs(x)))


def _gla_kernel(*refs, heads, seq_rows, has_h0, scale):
    if has_h0:
        q_ref, k_ref, v_ref, r_ref, a_ref, wup_ref, bal_ref, gn_ref, h0_ref, og_ref, st_ref = refs
    else:
        q_ref, k_ref, v_ref, r_ref, a_ref, wup_ref, bal_ref, gn_ref, og_ref, st_ref = refs
    rows, key = q_ref.shape
    val = v_ref.shape[1]
    dk, dv = key // heads, val // heads
    nseq = rows // seq_rows

    @pl.when(pl.program_id(1) == 0)
    def _():
        if has_h0:
            st_ref[...] = h0_ref[...]
        else:
            st_ref[...] = jnp.zeros(st_ref.shape, F32)

    la = _dot(a_ref[...].astype(BF16), wup_ref[...]) + bal_ref[...]
    log_a = _log_sigmoid(la) * (1.0 / GLA_TAU)

    ri = lax.broadcasted_iota(jnp.int32, (rows, rows), 0)
    ci = lax.broadcasted_iota(jnp.int32, (rows, rows), 1)
    causal = ci <= ri
    if nseq > 1:
        shift = seq_rows.bit_length() - 1
        causal = jnp.logical_and(causal, (ri >> shift) == (ci >> shift))
        row_seq = lax.broadcasted_iota(jnp.int32, (rows, 1), 0) >> shift
    tri = jnp.where(causal, 1.0, 0.0).astype(BF16)
    b = sum(_dot(tri, part) for part in _split_bf16(log_a, 3))

    q = q_ref[...] * scale
    k = k_ref[...]
    v = v_ref[...]
    r = r_ref[...]
    gn = gn_ref[...]
    for h in range(heads):
        ks = slice(h * dk, (h + 1) * dk)
        vs = slice(h * dv, (h + 1) * dv)
        bh = b[:, ks]
        q_e = (q[:, ks] * jnp.exp(bh)).astype(BF16)
        k_e = (k[:, ks] * jnp.exp(-bh)).astype(BF16)
        vh = v[:, vs].astype(BF16)
        scores = jnp.where(causal, _dot_nt(q_e, k_e), 0.0)
        o = _dot(scores.astype(BF16), vh)
        for s in range(nseq):
            b_last = bh[(s + 1) * seq_rows - 1:(s + 1) * seq_rows, :]
            k_s = k[:, ks] * jnp.exp(b_last - bh)
            if nseq > 1:
                in_seq = row_seq == s
                k_s = jnp.where(in_seq, k_s, 0.0)
            state = st_ref[s, h]
            o_inter = _dot(q_e, state.astype(BF16))
            o = o + (jnp.where(in_seq, o_inter, 0.0) if nseq > 1 else o_inter)
            d_state = _dot_tn(k_s.astype(BF16), vh)
            decay_col = jnp.broadcast_to(jnp.exp(b_last), (LANES, dk)).T[:, 0:1]
            st_ref[s, h] = decay_col * state + d_state
        mu = jnp.mean(o, axis=-1, keepdims=True)
        oc = o - mu
        o_n = oc * lax.rsqrt(jnp.mean(oc * oc, axis=-1, keepdims=True) + EPS) * gn[:, vs]
        og_ref[:, vs] = (o_n * _silu(r[:, vs])).astype(og_ref.dtype)


def _gla(z, a_low, w_up, b_alpha, gn, h0, *, row0, n_seq, seq_len, heads, dk, dv, name):
    key, val = heads * dk, heads * dv
    assert val == 2 * key and z.shape[1] == 2 * key + 2 * val
    chunk = min(GLA_CHUNK, seq_len)
    assert seq_len % chunk == 0
    if chunk % SUBLANES == 0:
        rows, n_chunks = chunk, seq_len // chunk
    else:
        assert SUBLANES % seq_len == 0 and chunk == seq_len
        rows, n_chunks = SUBLANES, 1
    nseq = rows // chunk
    assert n_seq % nseq == 0 and row0 % rows == 0
    blk0 = row0 // rows
    row_map = lambda g, c: blk0 + g * n_chunks + c
    in_specs = [
        pl.BlockSpec((rows, key), lambda g, c: (row_map(g, c), 0)),
        pl.BlockSpec((rows, key), lambda g, c: (row_map(g, c), 1)),
        pl.BlockSpec((rows, val), lambda g, c: (row_map(g, c), 1)),
        pl.BlockSpec((rows, val), lambda g, c: (row_map(g, c), 2)),
        pl.BlockSpec((rows, LANES), lambda g, c: (row_map(g, c), 0)),
        pl.BlockSpec((LANES, key), lambda g, c: (0, 0)),
        pl.BlockSpec((1, key), lambda g, c: (0, 0)),
        pl.BlockSpec((1, val), lambda g, c: (0, 0)),
    ]
    args = [z, z, z, z, a_low, w_up, b_alpha.reshape(1, key), gn.reshape(1, val)]
    state_spec = pl.BlockSpec((nseq, heads, dk, dv), lambda g, c: (g, 0, 0, 0))
    if h0 is not None:
        in_specs.append(state_spec)
        args.append(h0)
    n_rows = n_seq * seq_len
    return pl.pallas_call(
        functools.partial(_gla_kernel, heads=heads, seq_rows=chunk, has_h0=h0 is not None, scale=dk ** -0.5),
        grid=(n_seq // nseq, n_chunks),
        in_specs=in_specs,
        out_specs=[pl.BlockSpec((rows, val), lambda g, c: (g * n_chunks + c, 0)), state_spec],
        out_shape=[jax.ShapeDtypeStruct((n_rows, val), BF16),
                   jax.ShapeDtypeStruct((n_seq, heads, dk, dv), F32)],
        compiler_params=_params(("parallel", "arbitrary"), 48),
        name=name,
    )(*args)


CONV_LANES = 512
CONV_ROWS = 32
HIST_ROWS = 32


def _ln_silu(x, g, b):
    mu = jnp.mean(x, axis=-1, keepdims=True)
    xc = x - mu
    var = jnp.mean(xc * xc, axis=-1, keepdims=True)
    return _silu(xc * lax.rsqrt(var + EPS) * g + b)


def _conv_strip(ext_ref, w_ref, row_start, n_rows, lanes, width):
    acc = jnp.zeros((n_rows, lanes.stop - lanes.start), F32)
    for j in range(width):
        acc = acc + ext_ref[row_start + j:row_start + j + n_rows, lanes] * w_ref[j:j + 1, lanes]
    return acc


def _conv_prompt_kernel(u_ref, w_ref, wb_ref, g_ref, b_ref, c_ref, ext_ref, conv_ref, *, width):
    tb, cd = u_ref.shape
    first = HIST_ROWS - (width - 1)

    @pl.when(pl.program_id(1) == 0)
    def _():
        ext_ref[0:HIST_ROWS, :] = jnp.zeros((HIST_ROWS, cd), F32)

    ext_ref[HIST_ROWS:HIST_ROWS + tb, :] = u_ref[...]
    for r0 in range(0, tb, CONV_ROWS):
        for c0 in range(0, cd, CONV_LANES):
            lanes = slice(c0, c0 + CONV_LANES)
            conv_ref[r0:r0 + CONV_ROWS, lanes] = _conv_strip(ext_ref, w_ref, first + r0, CONV_ROWS, lanes, width)
    ext_ref[0:HIST_ROWS, :] = ext_ref[tb:tb + HIST_ROWS, :]
    c_ref[...] = _ln_silu(conv_ref[...] + wb_ref[...], g_ref[...], b_ref[...]).astype(c_ref.dtype)


def _conv_prompt(ug, w, wb, g, b, *, n_seq, seq_len):
    width, cd = w.shape
    assert width - 1 <= HIST_ROWS and cd % CONV_LANES == 0
    tb = _pick(seq_len, 64, CONV_ROWS)
    assert tb % CONV_ROWS == 0 and tb >= HIST_ROWS
    n_blk = seq_len // tb
    vec = pl.BlockSpec((1, cd), lambda s, t: (0, 0))
    return pl.pallas_call(
        functools.partial(_conv_prompt_kernel, width=width),
        grid=(n_seq, n_blk),
        in_specs=[pl.BlockSpec((tb, cd), lambda s, t: (s * n_blk + t, 0)),
                  pl.BlockSpec((width, cd), lambda s, t: (0, 0)), vec, vec, vec],
        out_specs=pl.BlockSpec((tb, cd), lambda s, t: (s * n_blk + t, 0)),
        out_shape=jax.ShapeDtypeStruct((n_seq * seq_len, cd), BF16),
        scratch_shapes=[pltpu.VMEM((HIST_ROWS + tb, cd), F32), pltpu.VMEM((tb, cd), F32)],
        compiler_params=_params(("parallel", "arbitrary"), 32),
        name="conv_prompt",
    )(ug, w, wb.reshape(1, cd), g.reshape(1, cd), b.reshape(1, cd))


def _conv_sample_kernel(u_ref, buf_ref, w_ref, wb_ref, g_ref, b_ref, c_ref, nbuf_ref, ext_ref, conv_ref,
                        *, width, seq_len):
    n_seq, hist, cd = buf_ref.shape
    pad_rows = ext_ref.shape[0] - hist - seq_len
    for s in range(n_seq):
        ext_ref[0:hist, :] = buf_ref[s]
        ext_ref[hist:hist + seq_len, :] = u_ref[s * seq_len:(s + 1) * seq_len, :]
        ext_ref[hist + seq_len:, :] = jnp.zeros((pad_rows, cd), F32)
        for c0 in range(0, cd, CONV_LANES):
            lanes = slice(c0, c0 + CONV_LANES)
            acc = _conv_strip(ext_ref, w_ref, 0, SUBLANES, lanes, width)
            conv_ref[s * seq_len:(s + 1) * seq_len, lanes] = acc[0:seq_len, :]
        nbuf_ref[s] = ext_ref[seq_len:seq_len + hist, :]
    c_ref[...] = _ln_silu(conv_ref[...] + wb_ref[...], g_ref[...], b_ref[...]).astype(c_ref.dtype)


def _conv_sample(ug, buf, w, wb, g, b, *, row0, seq_len):
    width, cd = w.shape
    n_seq, hist, _ = buf.shape
    assert hist == width - 1 and seq_len <= SUBLANES and cd % CONV_LANES == 0
    sb = _pick(n_seq, 8, 1)
    rows = sb * seq_len
    assert rows % 16 == 0 and row0 % rows == 0
    blk0 = row0 // rows
    ext_rows = -(-(hist + SUBLANES) // SUBLANES) * SUBLANES
    vec = pl.BlockSpec((1, cd), lambda i: (0, 0))
    return pl.pallas_call(
        functools.partial(_conv_sample_kernel, width=width, seq_len=seq_len),
        grid=(n_seq // sb,),
        in_specs=[pl.BlockSpec((rows, cd), lambda i: (blk0 + i, 0)),
                  pl.BlockSpec((sb, hist, cd), lambda i: (i, 0, 0)),
                  pl.BlockSpec((width, cd), lambda i: (0, 0)), vec, vec, vec],
        out_specs=[pl.BlockSpec((rows, cd), lambda i: (i, 0)),
                   pl.BlockSpec((sb, hist, cd), lambda i: (i, 0, 0))],
        out_shape=[jax.ShapeDtypeStruct((n_seq * seq_len, cd), BF16),
                   jax.ShapeDtypeStruct((n_seq, hist, cd), F32)],
        scratch_shapes=[pltpu.VMEM((ext_rows, cd), F32), pltpu.VMEM((rows, cd), F32)],
        compiler_params=_params(("parallel",), 32),
        name="conv_sample",
    )(ug, buf, w, wb.reshape(1, cd), g.reshape(1, cd), b.reshape(1, cd))


def _attn_kernel(q_ref, k_ref, v_ref, o_ref, *, heads, seq_rows, scale):
    rows, ca = q_ref.shape
    nseq = rows // seq_rows
    n_mem = k_ref.shape[0] // nseq
    hd = ca // heads
    q = q_ref[...]
    if nseq > 1:
        shift = seq_rows.bit_length() - 1
        row_seq = lax.broadcasted_iota(jnp.int32, (rows, 1), 0) >> shift
    for h in range(heads):
        hs = slice(h * hd, (h + 1) * hd)
        out = None
        for s in range(nseq):
            ms = slice(s * n_mem, (s + 1) * n_mem)
            sc = _dot_nt(q[:, hs], k_ref[ms, hs].astype(BF16)) * scale
            e = jnp.exp(sc - jnp.max(sc, axis=-1, keepdims=True))
            p = e / jnp.sum(e, axis=-1, keepdims=True)
            o = _dot(p.astype(BF16), v_ref[ms, hs].astype(BF16))
            out = o if out is None else jnp.where(row_seq == s, o, out)
        o_ref[:, hs] = out.astype(o_ref.dtype)


def _attn(q, k, v, *, row0, n_seq, seq_len, n_mem, heads, name):
    ca = q.shape[1]
    if seq_len % SUBLANES == 0:
        rows, nseq = _pick(seq_len, 256, 16), 1
    else:
        assert SUBLANES % seq_len == 0
        rows, nseq = 16, 16 // seq_len
    assert row0 % rows == 0 and (n_seq * seq_len) % rows == 0
    blk0 = row0 // rows
    per_seq = max(seq_len // rows, 1)
    kv_spec = pl.BlockSpec((nseq * n_mem, ca), lambda i: (i // per_seq, 0))
    return pl.pallas_call(
        functools.partial(_attn_kernel, heads=heads, seq_rows=seq_len if nseq > 1 else rows,
                          scale=(ca // heads) ** -0.5),
        grid=(n_seq * seq_len // rows,),
        in_specs=[pl.BlockSpec((rows, ca), lambda i: (blk0 + i, 0)), kv_spec, kv_spec],
        out_specs=pl.BlockSpec((rows, ca), lambda i: (i, 0)),
        out_shape=jax.ShapeDtypeStruct((n_seq * seq_len, ca), BF16),
        compiler_params=_params(("parallel",), 32),
        name=name,
    )(q, k, v)


ROUTE_E0, ROUTE_E1, ROUTE_G0, ROUTE_G1 = 0, 1, 2, 3
NORM_ROWS = 32


def _router_kernel(x_ref, g_ref, wg_ref, bg_ref, we_ref, be_ref, o_ref, *, n_groups, per_group):
    h = _rmsnorm_rows(x_ref[...], g_ref[...])
    h_hi, h_lo = _split_bf16(h, 2)

    def logits(w_ref, b_ref):
        w_hi, w_lo = _split_bf16(w_ref[...], 2)
        return _dot(h_hi, w_hi) + (_dot(h_hi, w_lo) + _dot(h_lo, w_hi)) + b_ref[...]

    lg = logits(wg_ref, bg_ref)
    le = logits(we_ref, be_ref)
    lane_i = lax.broadcasted_iota(jnp.int32, lg.shape, 1)
    lane = lane_i.astype(F32)
    neg = jnp.float32(-jnp.inf)

    def first_argmax(x, m):
        return jnp.min(jnp.where(x == m, lane, float(LANES)), axis=-1, keepdims=True)

    lg = jnp.where(lane_i < n_groups, lg, neg)
    mg = jnp.max(lg, axis=-1, keepdims=True)
    pg_top = 1.0 / jnp.sum(jnp.exp(lg - mg), axis=-1, keepdims=True)
    gsel = first_argmax(lg, mg)
    shift = per_group.bit_length() - 1
    le = jnp.where((lane_i >> shift).astype(F32) == gsel, le, neg)
    m0 = jnp.max(le, axis=-1, keepdims=True)
    z = jnp.sum(jnp.exp(le - m0), axis=-1, keepdims=True)
    e0 = first_argmax(le, m0)
    le1 = jnp.where(lane == e0, neg, le)
    m1 = jnp.max(le1, axis=-1, keepdims=True)
    e1 = first_argmax(le1, m1)
    p0 = 1.0 / z
    p1 = jnp.exp(m1 - m0) / z
    den = p0 + p1
    g0 = pg_top * (p0 / den)
    g1 = pg_top * (p1 / den)
    rec = jnp.where(lane_i == ROUTE_E0, e0, 0.0)
    rec = jnp.where(lane_i == ROUTE_E1, e1, rec)
    rec = jnp.where(lane_i == ROUTE_G0, g0, rec)
    rec = jnp.where(lane_i == ROUTE_G1, g1, rec)
    o_ref[...] = rec


def _router(x, g, w_group, b_group, w_expert, b_expert):
    t, d = x.shape
    n_groups, n_experts = w_group.shape[1], w_expert.shape[1]
    per_group = n_experts // n_groups
    assert n_experts <= LANES and per_group & (per_group - 1) == 0
    pad = lambda a: jnp.pad(a, ((0, 0), (0, LANES - a.shape[1])))
    tm = _pick(t, 128, SUBLANES)
    wspec = pl.BlockSpec((d, LANES), lambda i: (0, 0))
    bspec = pl.BlockSpec((1, LANES), lambda i: (0, 0))
    return pl.pallas_call(
        functools.partial(_router_kernel, n_groups=n_groups, per_group=per_group),
        grid=(t // tm,),
        in_specs=[pl.BlockSpec((tm, d), lambda i: (i, 0)), pl.BlockSpec((1, d), lambda i: (0, 0)),
                  wspec, bspec, wspec, bspec],
        out_specs=pl.BlockSpec((tm, LANES), lambda i: (i, 0)),
        out_shape=jax.ShapeDtypeStruct((t, LANES), F32),
        compiler_params=_params(("parallel",), 32),
        name="router",
    )(x, g.reshape(1, d), pad(w_group), pad(b_group.reshape(1, -1)), pad(w_expert), pad(b_expert.reshape(1, -1)))


def _dispatch_plan(e_ids, n_experts, tile):
    t, k = e_ids.shape
    n_pairs = t * k
    n_tiles = n_pairs // tile + n_experts
    flat = e_ids.reshape(n_pairs)
    onehot = (flat[:, None] == jnp.arange(n_experts, dtype=jnp.int32)[None, :]).astype(jnp.int32)
    counts = jnp.sum(onehot, axis=0)
    tiles_per = (counts + tile - 1) // tile
    tile_end = jnp.cumsum(tiles_per)
    tile_start = tile_end - tiles_per
    rank = jnp.take_along_axis(jnp.cumsum(onehot, axis=0), flat[:, None], axis=1)[:, 0] - 1
    pos = tile_start[flat] * tile + rank
    token = jnp.arange(n_pairs, dtype=jnp.int32) // k
    src = jnp.zeros((n_tiles * tile,), jnp.int32).at[pos].set(token)
    n_used = tile_end[-1]
    tile_ids = jnp.minimum(jnp.arange(n_tiles, dtype=jnp.int32), n_used - 1)
    tile_expert = jnp.searchsorted(tile_end, tile_ids, side="right").astype(jnp.int32)
    after = tile_end[tile_expert]
    next_expert = jnp.where(after < n_used, tile_expert[jnp.minimum(after, n_tiles - 1)], -1).astype(jnp.int32)
    return (pos.reshape(t, k), src.reshape(n_tiles, 1, tile), tile_expert, next_expert,
            n_used.reshape(1).astype(jnp.int32))


def _row_gather(src_hbm, idx_ref, dst_ref, sem):
    n_rows = dst_ref.shape[0]

    def copy(r):
        return pltpu.make_async_copy(src_hbm.at[pl.ds(idx_ref[0, 0, r], 1), :], dst_ref.at[pl.ds(r, 1), :], sem)

    def start():
        lax.fori_loop(0, n_rows, lambda r, c: (copy(r).start(), c)[1], 0)

    def wait():
        lax.fori_loop(0, n_rows, lambda r, c: (copy(r).wait(), c)[1], 0)

    return start, wait


def _cast_rows(src_ref, dst_ref, rows_per):
    def body(c, carry):
        rows = pl.ds(pl.multiple_of(c * rows_per, rows_per), rows_per)
        dst_ref[rows, :] = src_ref[rows, :].astype(dst_ref.dtype)
        return carry

    lax.fori_loop(0, src_ref.shape[0] // rows_per, body, 0)


def _experts_kernel(texp_ref, nexp_ref, nused_ref, idx_ref, idx_next_ref, x_hbm, g_ref, wg_hbm, wu_hbm, wd_hbm,
                    y_ref, xbuf, hbuf, stage_g, stage_u, stage_d, wg_bf, wu_bf, wd_bf, gsem, wsem):
    i = pl.program_id(0)
    n_used = nused_ref[0]
    slot = lax.rem(i, 2)
    expert = texp_ref[i]
    start_rows, wait_rows = _row_gather(x_hbm, idx_ref, xbuf.at[slot], gsem.at[slot])
    start_next_rows, _ = _row_gather(x_hbm, idx_next_ref, xbuf.at[1 - slot], gsem.at[1 - slot])

    def weight_copies(e):
        return (pltpu.make_async_copy(wg_hbm.at[e], stage_g, wsem.at[0]),
                pltpu.make_async_copy(wu_hbm.at[e], stage_u, wsem.at[1]),
                pltpu.make_async_copy(wd_hbm.at[e], stage_d, wsem.at[2]))

    @pl.when(i < n_used)
    def _():
        @pl.when(i == 0)
        def _():
            start_rows()
            for cp in weight_copies(expert):
                cp.start()

        @pl.when(i + 1 < n_used)
        def _():
            start_next_rows()

        @pl.when(jnp.logical_or(i == 0, expert != texp_ref[jnp.maximum(i - 1, 0)]))
        def _():
            for cp in weight_copies(expert):
                cp.wait()
            _cast_rows(stage_g, wg_bf, 512)
            _cast_rows(stage_u, wu_bf, 512)
            _cast_rows(stage_d, wd_bf, 64)
            nxt = nexp_ref[i]

            @pl.when(nxt >= 0)
            def _():
                for cp in weight_copies(nxt):
                    cp.start()

        wait_rows()

        def norm_rows(c, carry):
            rows = pl.ds(pl.multiple_of(c * NORM_ROWS, NORM_ROWS), NORM_ROWS)
            hbuf[rows, :] = _rmsnorm_rows(xbuf[slot, rows, :], g_ref[...]).astype(BF16)
            return carry

        lax.fori_loop(0, hbuf.shape[0] // NORM_ROWS, norm_rows, 0)
        h = hbuf[...]
        a = _dot(h, wg_bf[...])
        u = _dot(h, wu_bf[...])
        hid = (_silu(a) * u).astype(BF16)
        y_ref[...] = _dot(hid, wd_bf[...])

    @pl.when(i >= n_used)
    def _():
        y_ref[...] = jnp.zeros(y_ref.shape, F32)


def _experts(x, g, src, tile_expert, next_expert, n_used, wg, wu, wd):
    t, d = x.shape
    n_tiles, _, tile = src.shape
    de = wg.shape[2]
    assert d % 512 == 0 and de % 64 == 0
    idx_spec = lambda ahead: pl.BlockSpec(
        (1, 1, tile), lambda i, texp, nexp, nused: (jnp.minimum(i + ahead, nused[0] - 1), 0, 0),
        memory_space=pltpu.SMEM)
    hbm = pl.BlockSpec(memory_space=pl.ANY)
    grid_spec = pltpu.PrefetchScalarGridSpec(
        num_scalar_prefetch=3,
        grid=(n_tiles,),
        in_specs=[idx_spec(0), idx_spec(1), hbm, pl.BlockSpec((1, d), lambda i, texp, nexp, nused: (0, 0)),
                  hbm, hbm, hbm],
        out_specs=pl.BlockSpec((tile, d), lambda i, texp, nexp, nused: (i, 0)),
        scratch_shapes=[pltpu.VMEM((2, tile, d), F32), pltpu.VMEM((tile, d), BF16),
                        pltpu.VMEM((d, de), F32), pltpu.VMEM((d, de), F32), pltpu.VMEM((de, d), F32),
                        pltpu.VMEM((d, de), BF16), pltpu.VMEM((d, de), BF16), pltpu.VMEM((de, d), BF16),
                        pltpu.SemaphoreType.DMA((2,)), pltpu.SemaphoreType.DMA((3,))],
    )
    weights_bytes = 3 * d * de * (4 + 2)
    tiles_bytes = (2 + 2) * tile * d * 4 + tile * d * 2 + 3 * tile * de * 4
    return pl.pallas_call(
        _experts_kernel,
        grid_spec=grid_spec,
        out_shape=jax.ShapeDtypeStruct((n_tiles * tile, d), F32),
        compiler_params=_params(("arbitrary",), (weights_bytes + tiles_bytes) // MIB + 3),
        name="experts",
    )(tile_expert, next_expert, n_used, src, src, x, g.reshape(1, d), wg, wu, wd)


def _combine_kernel(idx0_ref, idx1_ref, idx0_next_ref, idx1_next_ref, y_hbm, x_ref, route_ref, gf_ref, *refs,
                    n_first, final_norm):
    outs, (buf0, buf1, sem) = refs[:-3], refs[-3:]
    i = pl.program_id(0)
    slot = lax.rem(i, 2)
    gathers = [_row_gather(y_hbm, idx, buf.at[slot], sem.at[k, slot])
               for k, (idx, buf) in enumerate(((idx0_ref, buf0), (idx1_ref, buf1)))]
    gathers_next = [_row_gather(y_hbm, idx, buf.at[1 - slot], sem.at[k, 1 - slot])
                    for k, (idx, buf) in enumerate(((idx0_next_ref, buf0), (idx1_next_ref, buf1)))]

    @pl.when(i == 0)
    def _():
        for start, _ in gathers:
            start()

    @pl.when(i + 1 < pl.num_programs(0))
    def _():
        for start, _ in gathers_next:
            start()

    for _, wait in gathers:
        wait()
    route = route_ref[...]
    g0 = route[:, ROUTE_G0:ROUTE_G0 + 1]
    g1 = route[:, ROUTE_G1:ROUTE_G1 + 1]
    res = x_ref[...] + (g0 * buf0[slot] + g1 * buf1[slot])
    if final_norm:
        res = _rmsnorm_rows(res, gf_ref[...])
    if len(outs) == 1:
        outs[0][...] = res
    else:
        @pl.when(i < n_first)
        def _():
            outs[0][...] = res

        @pl.when(i >= n_first)
        def _():
            outs[1][...] = res


def _combine(x, y_sorted, pos, route, g_final, *, split_rows, final_norm):
    t, d = x.shape
    tm = _pick(t if split_rows is None else math.gcd(split_rows, t - split_rows), 128, SUBLANES)
    n_blk = t // tm
    idx = [pos[:, k].reshape(n_blk, 1, tm) for k in range(TOP_K_IN_GROUP)]
    smem = pl.BlockSpec((1, 1, tm), lambda i: (i, 0, 0), memory_space=pltpu.SMEM)
    smem_next = pl.BlockSpec((1, 1, tm), lambda i: (jnp.minimum(i + 1, n_blk - 1), 0, 0), memory_space=pltpu.SMEM)
    row = pl.BlockSpec((tm, d), lambda i: (i, 0))
    if split_rows is None:
        n_first = n_blk
        out_specs = [row]
        out_shape = [jax.ShapeDtypeStruct((t, d), F32)]
    else:
        n_first = split_rows // tm
        out_specs = [pl.BlockSpec((tm, d), lambda i: (jnp.minimum(i, n_first - 1), 0)),
                     pl.BlockSpec((tm, d), lambda i: (jnp.maximum(i - n_first, 0), 0))]
        out_shape = [jax.ShapeDtypeStruct((split_rows, d), F32), jax.ShapeDtypeStruct((t - split_rows, d), F32)]
    return pl.pallas_call(
        functools.partial(_combine_kernel, n_first=n_first, final_norm=final_norm),
        grid=(n_blk,),
        in_specs=[smem, smem, smem_next, smem_next, pl.BlockSpec(memory_space=pl.ANY), row,
                  pl.BlockSpec((tm, LANES), lambda i: (i, 0)), pl.BlockSpec((1, d), lambda i: (0, 0))],
        out_specs=out_specs,
        out_shape=out_shape,
        scratch_shapes=[pltpu.VMEM((2, tm, d), F32), pltpu.VMEM((2, tm, d), F32),
                        pltpu.SemaphoreType.DMA((TOP_K_IN_GROUP, 2))],
        compiler_params=_params(("arbitrary",), 32),
        name="combine",
    )(idx[0], idx[1], idx[0], idx[1], y_sorted, x, route, g_final.reshape(1, d))


def kernel(x_prompt, x_sample, mem_prompt, state_gla, state_conv, cache_mem_k, cache_mem_v, norm_mix_g, w_in, w_alpha_up, b_alpha, gla_norm_g, w_branch_a, conv_dw_w, conv_dw_b, conv_ln_g, conv_ln_b, w_branch_b, w_out, norm_ca_g, norm_mem_g, w_ca_q, w_ca_k, w_ca_v, w_ca_o, norm_ffn_g, w_router_group, b_router_group, w_router_expert, b_router_expert, w_exp_gate, w_exp_up, w_exp_down, norm_final_g):
    depth = w_in.shape[0]
    bp, tp, d = x_prompt.shape
    bs, ts, _ = x_sample.shape
    heads, dk, dv = state_gla.shape[2:]
    key, val = heads * dk, heads * dv
    rank = w_alpha_up.shape[1]
    cd = state_conv.shape[3]
    n_mem, ca_heads, ca_hd = cache_mem_k.shape[2:]
    ca = ca_heads * ca_hd
    n_experts = w_router_expert.shape[2]
    rows_p, rows_s = bp * tp, bs * ts
    off_a = 2 * key + 2 * val
    off_u = off_a + rank
    off_g = off_u + 2 * cd
    assert rank <= LANES and w_in.shape[2] == off_g + 2 * d
    bf = lambda a: a.astype(BF16)

    x = jnp.concatenate([x_prompt.reshape(rows_p, d), x_sample.reshape(rows_s, d)], axis=0)
    outs = dict(gla_p=[], conv_p=[], mk_p=[], mv_p=[], gla_s=[], conv_s=[])
    for l in range(depth):
        wl = w_in[l]
        h = _rmsnorm(x, norm_mix_g[l], BF16)
        z = _mm([h], [bf(wl[:, :off_a])], [0], _ep_identity, F32, name="in_qkvr")
        a_low = _mm([h], [bf(jnp.pad(wl[:, off_a:off_u], ((0, 0), (0, LANES - rank))))], [0], _ep_identity, F32,
                    name="in_alow")
        ug = _mm([h, h], [bf(wl[:, off_u:off_u + cd]), bf(wl[:, off_u + cd:off_g])], [0, 0], _ep_glu, F32,
                 name="in_glu")
        w_up = bf(jnp.pad(w_alpha_up[l], ((0, LANES - rank), (0, 0))))
        gla_args = dict(heads=heads, dk=dk, dv=dv)
        og_p, st_p = _gla(z, a_low, w_up, b_alpha[l], gla_norm_g[l], None, row0=0, n_seq=bp, seq_len=tp,
                          name="gla_prompt", **gla_args)
        og_s, st_s = _gla(z, a_low, w_up, b_alpha[l], gla_norm_g[l], state_gla[l], row0=rows_p, n_seq=bs,
                          seq_len=ts, name="gla_sample", **gla_args)
        conv_w = (conv_dw_w[l], conv_dw_b[l], conv_ln_g[l], conv_ln_b[l])
        c_p = _conv_prompt(ug, *conv_w, n_seq=bp, seq_len=tp)
        c_s, buf_s = _conv_sample(ug, state_conv[l], *conv_w, row0=rows_p, seq_len=ts)
        og = jnp.concatenate([og_p, og_s], axis=0)
        c = jnp.concatenate([c_p, c_s], axis=0)
        merged = _mm([og, c, h], [bf(w_branch_a[l]), bf(w_branch_b[l]), bf(wl[:, off_g:off_g + d]),
                                  bf(wl[:, off_g + d:])], [0, 1, 2, 2], _ep_gated_merge, BF16,
                     tn_pref=256, name="merge")
        x = _mm([merged], [bf(w_out[l])], [0], _ep_residual, F32, extras=[x], name="out_proj")
        h = _rmsnorm(x, norm_ca_g[l], BF16)
        q = _mm([h], [bf(w_ca_q[l])], [0], _ep_identity, BF16, name="ca_q")
        m = _rmsnorm(mem_prompt.reshape(bp * n_mem, d), norm_mem_g[l], BF16)
        mk = _mm([m], [bf(w_ca_k[l])], [0], _ep_identity, F32, name="mem_k")
        mv = _mm([m], [bf(w_ca_v[l])], [0], _ep_identity, F32, name="mem_v")
        ao_p = _attn(q, mk, mv, row0=0, n_seq=bp, seq_len=tp, n_mem=n_mem, heads=ca_heads, name="attn_prompt")
        ao_s = _attn(q, cache_mem_k[l].reshape(bs * n_mem, ca), cache_mem_v[l].reshape(bs * n_mem, ca),
                     row0=rows_p, n_seq=bs, seq_len=ts, n_mem=n_mem, heads=ca_heads, name="attn_sample")
        ao = jnp.concatenate([ao_p, ao_s], axis=0)
        x = _mm([ao], [bf(w_ca_o[l])], [0], _ep_residual, F32, extras=[x], name="ca_out")
        route = _router(x, norm_ffn_g[l], w_router_group[l], b_router_group[l], w_router_expert[l],
                        b_router_expert[l])
        e_ids = route[:, ROUTE_E0:ROUTE_E1 + 1].astype(jnp.int32)
        tile = _pick(rows_p + rows_s, 256, SUBLANES)
        pos, src, tile_expert, next_expert, n_used = _dispatch_plan(e_ids, n_experts, tile)
        y_sorted = _experts(x, norm_ffn_g[l], src, tile_expert, next_expert, n_used, w_exp_gate[l], w_exp_up[l],
                            w_exp_down[l])
        last = l == depth - 1
        res = _combine(x, y_sorted, pos, route, norm_final_g, split_rows=rows_p if last else None,
                       final_norm=last)
        if not last:
            x = res[0]
        outs["gla_p"].append(st_p)
        hist = conv_dw_w.shape[1] - 1
        outs["conv_p"].append(jnp.stack([ug[(b + 1) * tp - hist:(b + 1) * tp] for b in range(bp)]))
        outs["mk_p"].append(mk.reshape(bp, n_mem, ca_heads, ca_hd))
        outs["mv_p"].append(mv.reshape(bp, n_mem, ca_heads, ca_hd))
        outs["gla_s"].append(st_s)
        outs["conv_s"].append(buf_s)
    y_prompt = res[0].reshape(bp, tp, d)
    y_sample = res[1].reshape(bs, ts, d)
    return (y_prompt, y_sample, jnp.stack(outs["gla_p"]), jnp.stack(outs["conv_p"]), jnp.stack(outs["mk_p"]),
            jnp.stack(outs["mv_p"]), jnp.stack(outs["gla_s"]), jnp.stack(outs["conv_s"]))
```

```python
import functools
import math

import jax
import jax.numpy as jnp
from jax import lax
from jax.experimental import pallas as pl
from jax.experimental.pallas import tpu as pltpu

F32 = jnp.float32
BF16 = jnp.bfloat16

EPS = 1e-6
GLA_TAU = 16.0
GLA_CHUNK = 64
TOP_K_IN_GROUP = 2
LANES = 128
SUBLANES = 8
MIB = 1 << 20


def _pick(n, pref, mult):
    for d in range(min(pref, n), 0, -1):
        if n % d == 0 and d % mult == 0:
            return d
    return n


def _params(semantics, vmem_mib):
    return pltpu.CompilerParams(dimension_semantics=semantics, vmem_limit_bytes=vmem_mib * MIB)


def _sigmoid(x):
    return 1.0 / (1.0 + jnp.exp(-x))


def _silu(x):
    return x * _sigmoid(x)


def _dot(a, b):
    return jnp.dot(a, b, preferred_element_type=F32)


def _dot_nt(a, b):
    return lax.dot_general(a, b, (((1,), (1,)), ((), ())), preferred_element_type=F32)


def _dot_tn(a, b):
    return lax.dot_general(a, b, (((0,), (0,)), ((), ())), preferred_element_type=F32)


def _split_bf16(x, parts):
    out = []
    for _ in range(parts - 1):
        p = x.astype(BF16)
        out.append(p)
        x = x - p.astype(F32)
    out.append(x.astype(BF16))
    return out


def _rmsnorm_rows(x, g):
    return x * lax.rsqrt(jnp.mean(x * x, axis=-1, keepdims=True) + EPS) * g


def _parts(a):
    return a if isinstance(a, tuple) else (a,)


def _rows_of(a):
    return sum(p.shape[0] for p in _parts(a))


def _part_specs(a, tm, cols, col_of):
    parts = _parts(a)
    if len(parts) == 1:
        return [pl.BlockSpec((tm, cols), lambda i, *j: (i, col_of(*j)))]
    n_first = parts[0].shape[0] // tm
    assert parts[0].shape[0] % tm == 0 and parts[1].shape[0] % tm == 0
    return [pl.BlockSpec((tm, cols), lambda i, *j: (jnp.minimum(i, n_first - 1), col_of(*j))),
            pl.BlockSpec((tm, cols), lambda i, *j: (jnp.maximum(i - n_first, 0), col_of(*j)))]


def _part_tile(a, pref, mult):
    return _pick(math.gcd(*[p.shape[0] for p in _parts(a)]), pref, mult)


def _select_part(refs, n_first):
    if len(refs) == 1:
        return refs[0][...]
    return jnp.where(pl.program_id(0) < n_first, refs[0][...], refs[1][...])


def _rmsnorm_kernel(*refs, n_first):
    x_refs, (g_ref, o_ref) = refs[:-2], refs[-2:]
    o_ref[...] = _rmsnorm_rows(_select_part(x_refs, n_first), g_ref[...]).astype(o_ref.dtype)


def _rmsnorm(x, g, out_dtype):
    t, d = _rows_of(x), _parts(x)[0].shape[1]
    tm = _part_tile(x, 128, 16)
    return pl.pallas_call(
        functools.partial(_rmsnorm_kernel, n_first=_parts(x)[0].shape[0] // tm),
        grid=(t // tm,),
        in_specs=_part_specs(x, tm, d, lambda: 0) + [pl.BlockSpec((1, d), lambda i: (0, 0))],
        out_specs=pl.BlockSpec((tm, d), lambda i: (i, 0)),
        out_shape=jax.ShapeDtypeStruct((t, d), out_dtype),
        compiler_params=_params(("parallel",), 32),
        name="rmsnorm",
    )(*_parts(x), g.reshape(1, d))


def _cast_cols_kernel(*refs, shift):
    if shift == 0:
        a_ref, o_ref = refs
        o_ref[...] = a_ref[0].astype(BF16)
    else:
        a_ref, b_ref, o_ref = refs
        o_ref[...] = jnp.concatenate([a_ref[0][:, shift:], b_ref[0][:, :shift]], axis=1).astype(BF16)


def _cast_cols(w, layer, col0, n_cols):
    _, k, n_total = w.shape
    shift = col0 % LANES
    base = col0 - shift
    tn = _pick(math.gcd(base, n_cols), 1024, LANES)
    tk = _pick(k, 1024, SUBLANES)
    assert n_cols % tn == 0 and base + n_cols + (LANES if shift else 0) <= -(-n_total // LANES) * LANES
    in_specs = [pl.BlockSpec((1, tk, tn), lambda r, j: (layer, r, base // tn + j))]
    args = [w]
    if shift:
        in_specs.append(pl.BlockSpec((1, tk, LANES), lambda r, j: (layer, r, (base + (j + 1) * tn) // LANES)))
        args.append(w)
    return pl.pallas_call(
        functools.partial(_cast_cols_kernel, shift=shift),
        grid=(k // tk, n_cols // tn),
        in_specs=in_specs,
        out_specs=pl.BlockSpec((tk, tn), lambda r, j: (r, j)),
        out_shape=jax.ShapeDtypeStruct((k, n_cols), BF16),
        compiler_params=_params(("parallel", "parallel"), 32),
        name="cast_cols",
    )(*args)


def _ep_identity(accs, extras):
    return accs[0]


def _ep_glu(accs, extras):
    return accs[0] * _sigmoid(accs[1])


def _ep_residual(accs, extras):
    return extras[0] + accs[0]


def _ep_gated_merge(accs, extras):
    return _sigmoid(accs[2]) * accs[0] + _sigmoid(accs[3]) * accs[1]


def _mm_kernel(*refs, x_of_w, x_parts, e_parts, n_first, epilogue):
    refs = list(refs)
    take = lambda k: [refs.pop(0) for _ in range(k)]
    x_refs = [take(p) for p in x_parts]
    w_refs = take(len(x_of_w))
    e_refs = [take(p) for p in e_parts]
    o_ref = refs.pop(0)
    i = pl.program_id(0)
    xs = []
    for parts in x_refs:
        if len(parts) == 1:
            xs.append(parts[0])
            continue
        scratch = refs.pop(0)

        @pl.when(jnp.logical_and(pl.program_id(1) == 0, i < n_first))
        def _(scratch=scratch, src=parts[0]):
            scratch[...] = src[...]

        @pl.when(jnp.logical_and(pl.program_id(1) == 0, i >= n_first))
        def _(scratch=scratch, src=parts[1]):
            scratch[...] = src[...]

        xs.append(scratch)
    accs = [_dot(xs[x_of_w[k]][...], w_refs[k][...]) for k in range(len(x_of_w))]
    o_ref[...] = epilogue(accs, [_select_part(e, n_first) for e in e_refs]).astype(o_ref.dtype)


def _mm(xs, ws, x_of_w, epilogue, out_dtype, extras=(), tm_pref=512, tn_pref=512, name="mm", n=None, w_col0=None):
    t = _rows_of(xs[0])
    n = ws[0].shape[1] if n is None else n
    w_col0 = [0] * len(ws) if w_col0 is None else w_col0
    tm = min(_part_tile(a, tm_pref, 16) for a in list(xs) + list(extras))
    tn = _pick(math.gcd(n, *w_col0), tn_pref, LANES)
    split = [a for a in list(xs) + list(extras) if len(_parts(a)) == 2]
    assert all(a[0].shape[0] == split[0][0].shape[0] and a[0].shape[0] % tm == 0 for a in split)
    n_first = split[0][0].shape[0] // tm if split else 0
    k_of = lambda x: _parts(x)[0].shape[1]
    in_specs, args = [], []
    for x in xs:
        in_specs += _part_specs(x, tm, k_of(x), lambda j: 0)
        args += _parts(x)
    in_specs += [pl.BlockSpec((w.shape[0], tn), lambda i, j, b0=c0 // tn: (0, b0 + j)) for w, c0 in zip(ws, w_col0)]
    args += ws
    for e in extras:
        in_specs += _part_specs(e, tm, tn, lambda j: j)
        args += _parts(e)
    scratch = [pltpu.VMEM((tm, k_of(x)), _parts(x)[0].dtype) for x in xs if len(_parts(x)) == 2]
    block_bytes = sum(tm * p.shape[1] * p.dtype.itemsize for x in xs for p in _parts(x))
    block_bytes += sum(w.shape[0] * tn * w.dtype.itemsize for w in ws)
    block_bytes += sum(tm * tn * p.dtype.itemsize for e in extras for p in _parts(e))
    block_bytes += tm * tn * jnp.dtype(out_dtype).itemsize
    scratch_bytes = sum(tm * k_of(x) * _parts(x)[0].dtype.itemsize for x in xs if len(_parts(x)) == 2)
    acc_bytes = len(ws) * tm * tn * 4
    vmem = (2 * block_bytes + 2 * acc_bytes + scratch_bytes) // MIB + 4
    return pl.pallas_call(
        functools.partial(_mm_kernel, x_of_w=tuple(x_of_w), x_parts=tuple(len(_parts(x)) for x in xs),
                          e_parts=tuple(len(_parts(e)) for e in extras), n_first=n_first, epilogue=epilogue),
        grid=(t // tm, n // tn),
        in_specs=in_specs,
        out_specs=pl.BlockSpec((tm, tn), lambda i, j: (i, j)),
        out_shape=jax.ShapeDtypeStruct((t, n), out_dtype),
        scratch_shapes=scratch,
        compiler_params=_params(("parallel", "arbitrary"), vmem),
        name=name,
    )(*args)


def _log_sigmoid(x):
    return jnp.minimum(x, 0.0) - jnp.log1p(jnp.exp(-jnp.abs(x)))


def _gla_kernel(*refs, heads, seq_rows, has_h0, scale):
    if has_h0:
        q_ref, k_ref, v_ref, r_ref, a_ref, wup_ref, bal_ref, gn_ref, h0_ref, og_ref, st_ref = refs
    else:
        q_ref, k_ref, v_ref, r_ref, a_ref, wup_ref, bal_ref, gn_ref, og_ref, st_ref = refs
    rows, key = q_ref.shape
    val = v_ref.shape[1]
    dk, dv = key // heads, val // heads
    nseq = rows // seq_rows

    @pl.when(pl.program_id(1) == 0)
    def _():
        if has_h0:
            st_ref[...] = h0_ref[...]
        else:
            st_ref[...] = jnp.zeros(st_ref.shape, F32)

    la = _dot(a_ref[...].astype(BF16), wup_ref[...]) + bal_ref[...]
    log_a = _log_sigmoid(la) * (1.0 / GLA_TAU)

    ri = lax.broadcasted_iota(jnp.int32, (rows, rows), 0)
    ci = lax.broadcasted_iota(jnp.int32, (rows, rows), 1)
    causal = ci <= ri
    if nseq > 1:
        shift = seq_rows.bit_length() - 1
        causal = jnp.logical_and(causal, (ri >> shift) == (ci >> shift))
        row_seq = lax.broadcasted_iota(jnp.int32, (rows, 1), 0) >> shift
    tri = jnp.where(causal, 1.0, 0.0).astype(BF16)
    b = sum(_dot(tri, part) for part in _split_bf16(log_a, 3))

    q = q_ref[...] * scale
    k = k_ref[...]
    v = v_ref[...]
    r = r_ref[...]
    gn = gn_ref[...]
    for h in range(heads):
        ks = slice(h * dk, (h + 1) * dk)
        vs = slice(h * dv, (h + 1) * dv)
        bh = b[:, ks]
        q_e = (q[:, ks] * jnp.exp(bh)).astype(BF16)
        k_e = (k[:, ks] * jnp.exp(-bh)).astype(BF16)
        vh = v[:, vs].astype(BF16)
        scores = jnp.where(causal, _dot_nt(q_e, k_e), 0.0)
        o = _dot(scores.astype(BF16), vh)
        for s in range(nseq):
            b_last = bh[(s + 1) * seq_rows - 1:(s + 1) * seq_rows, :]
            k_s = k[:, ks] * jnp.exp(b_last - bh)
            if nseq > 1:
                in_seq = row_seq == s
                k_s = jnp.where(in_seq, k_s, 0.0)
            state = st_ref[s, h]
            o_inter = _dot(q_e, state.astype(BF16))
            o = o + (jnp.where(in_seq, o_inter, 0.0) if nseq > 1 else o_inter)
            d_state = _dot_tn(k_s.astype(BF16), vh)
            decay_col = jnp.broadcast_to(jnp.exp(b_last), (LANES, dk)).T[:, 0:1]
            st_ref[s, h] = decay_col * state + d_state
        mu = jnp.mean(o, axis=-1, keepdims=True)
        oc = o - mu
        o_n = oc * lax.rsqrt(jnp.mean(oc * oc, axis=-1, keepdims=True) + EPS) * gn[:, vs]
        og_ref[:, vs] = (o_n * _silu(r[:, vs])).astype(og_ref.dtype)


def _gla(z, a_low, w_up, b_alpha, gn, h0, *, row0, n_seq, seq_len, heads, dk, dv, name):
    key, val = heads * dk, heads * dv
    assert val == 2 * key and z.shape[1] == 2 * key + 2 * val
    chunk = min(GLA_CHUNK, seq_len)
    assert seq_len % chunk == 0
    if chunk % SUBLANES == 0:
        rows, n_chunks = chunk, seq_len // chunk
    else:
        assert SUBLANES % seq_len == 0 and chunk == seq_len
        rows, n_chunks = SUBLANES, 1
    nseq = rows // chunk
    assert n_seq % nseq == 0 and row0 % rows == 0
    blk0 = row0 // rows
    row_map = lambda g, c: blk0 + g * n_chunks + c
    in_specs = [
        pl.BlockSpec((rows, key), lambda g, c: (row_map(g, c), 0)),
        pl.BlockSpec((rows, key), lambda g, c: (row_map(g, c), 1)),
        pl.BlockSpec((rows, val), lambda g, c: (row_map(g, c), 1)),
        pl.BlockSpec((rows, val), lambda g, c: (row_map(g, c), 2)),
        pl.BlockSpec((rows, LANES), lambda g, c: (row_map(g, c), 0)),
        pl.BlockSpec((LANES, key), lambda g, c: (0, 0)),
        pl.BlockSpec((1, key), lambda g, c: (0, 0)),
        pl.BlockSpec((1, val), lambda g, c: (0, 0)),
    ]
    args = [z, z, z, z, a_low, w_up, b_alpha.reshape(1, key), gn.reshape(1, val)]
    state_spec = pl.BlockSpec((nseq, heads, dk, dv), lambda g, c: (g, 0, 0, 0))
    if h0 is not None:
        in_specs.append(state_spec)
        args.append(h0)
    n_rows = n_seq * seq_len
    return pl.pallas_call(
        functools.partial(_gla_kernel, heads=heads, seq_rows=chunk, has_h0=h0 is not None, scale=dk ** -0.5),
        grid=(n_seq // nseq, n_chunks),
        in_specs=in_specs,
        out_specs=[pl.BlockSpec((rows, val), lambda g, c: (g * n_chunks + c, 0)), state_spec],
        out_shape=[jax.ShapeDtypeStruct((n_rows, val), BF16),
                   jax.ShapeDtypeStruct((n_seq, heads, dk, dv), F32)],
        compiler_params=_params(("parallel", "arbitrary"), 48),
        name=name,
    )(*args)


CONV_LANES = 512
CONV_ROWS = 32
HIST_ROWS = 32


def _ln_silu(x, g, b):
    mu = jnp.mean(x, axis=-1, keepdims=True)
    xc = x - mu
    var = jnp.mean(xc * xc, axis=-1, keepdims=True)
    return _silu(xc * lax.rsqrt(var + EPS) * g + b)


def _conv_strip(ext_ref, w_ref, row_start, n_rows, lanes, width):
    acc = jnp.zeros((n_rows, lanes.stop - lanes.start), F32)
    for j in range(width):
        acc = acc + ext_ref[row_start + j:row_start + j + n_rows, lanes] * w_ref[j:j + 1, lanes]
    return acc


def _conv_prompt_kernel(u_ref, w_ref, wb_ref, g_ref, b_ref, c_ref, ext_ref, conv_ref, *, width):
    tb, cd = u_ref.shape
    first = HIST_ROWS - (width - 1)

    @pl.when(pl.program_id(1) == 0)
    def _():
        ext_ref[0:HIST_ROWS, :] = jnp.zeros((HIST_ROWS, cd), F32)

    ext_ref[HIST_ROWS:HIST_ROWS + tb, :] = u_ref[...]
    for r0 in range(0, tb, CONV_ROWS):
        for c0 in range(0, cd, CONV_LANES):
            lanes = slice(c0, c0 + CONV_LANES)
            conv_ref[r0:r0 + CONV_ROWS, lanes] = _conv_strip(ext_ref, w_ref, first + r0, CONV_ROWS, lanes, width)
    ext_ref[0:HIST_ROWS, :] = ext_ref[tb:tb + HIST_ROWS, :]
    c_ref[...] = _ln_silu(conv_ref[...] + wb_ref[...], g_ref[...], b_ref[...]).astype(c_ref.dtype)


def _conv_prompt(ug, w, wb, g, b, *, n_seq, seq_len):
    width, cd = w.shape
    assert width - 1 <= HIST_ROWS and cd % CONV_LANES == 0
    tb = _pick(seq_len, 64, CONV_ROWS)
    assert tb % CONV_ROWS == 0 and tb >= HIST_ROWS
    n_blk = seq_len // tb
    vec = pl.BlockSpec((1, cd), lambda s, t: (0, 0))
    return pl.pallas_call(
        functools.partial(_conv_prompt_kernel, width=width),
        grid=(n_seq, n_blk),
        in_specs=[pl.BlockSpec((tb, cd), lambda s, t: (s * n_blk + t, 0)),
                  pl.BlockSpec((width, cd), lambda s, t: (0, 0)), vec, vec, vec],
        out_specs=pl.BlockSpec((tb, cd), lambda s, t: (s * n_blk + t, 0)),
        out_shape=jax.ShapeDtypeStruct((n_seq * seq_len, cd), BF16),
        scratch_shapes=[pltpu.VMEM((HIST_ROWS + tb, cd), F32), pltpu.VMEM((tb, cd), F32)],
        compiler_params=_params(("parallel", "arbitrary"), 32),
        name="conv_prompt",
    )(ug, w, wb.reshape(1, cd), g.reshape(1, cd), b.reshape(1, cd))


def _conv_sample_kernel(u_ref, buf_ref, w_ref, wb_ref, g_ref, b_ref, c_ref, nbuf_ref, ext_ref, conv_ref,
                        *, width, seq_len):
    n_seq, hist, cd = buf_ref.shape
    pad_rows = ext_ref.shape[0] - hist - seq_len
    for s in range(n_seq):
        ext_ref[0:hist, :] = buf_ref[s]
        ext_ref[hist:hist + seq_len, :] = u_ref[s * seq_len:(s + 1) * seq_len, :]
        ext_ref[hist + seq_len:, :] = jnp.zeros((pad_rows, cd), F32)
        for c0 in range(0, cd, CONV_LANES):
            lanes = slice(c0, c0 + CONV_LANES)
            acc = _conv_strip(ext_ref, w_ref, 0, SUBLANES, lanes, width)
            conv_ref[s * seq_len:(s + 1) * seq_len, lanes] = acc[0:seq_len, :]
        nbuf_ref[s] = ext_ref[seq_len:seq_len + hist, :]
    c_ref[...] = _ln_silu(conv_ref[...] + wb_ref[...], g_ref[...], b_ref[...]).astype(c_ref.dtype)


def _conv_sample(ug, buf, w, wb, g, b, *, row0, seq_len):
    width, cd = w.shape
    n_seq, hist, _ = buf.shape
    assert hist == width - 1 and seq_len <= SUBLANES and cd % CONV_LANES == 0
    sb = _pick(n_seq, 8, 1)
    rows = sb * seq_len
    assert rows % 16 == 0 and row0 % rows == 0
    blk0 = row0 // rows
    ext_rows = -(-(hist + SUBLANES) // SUBLANES) * SUBLANES
    vec = pl.BlockSpec((1, cd), lambda i: (0, 0))
    return pl.pallas_call(
        functools.partial(_conv_sample_kernel, width=width, seq_len=seq_len),
        grid=(n_seq // sb,),
        in_specs=[pl.BlockSpec((rows, cd), lambda i: (blk0 + i, 0)),
                  pl.BlockSpec((sb, hist, cd), lambda i: (i, 0, 0)),
                  pl.BlockSpec((width, cd), lambda i: (0, 0)), vec, vec, vec],
        out_specs=[pl.BlockSpec((rows, cd), lambda i: (i, 0)),
                   pl.BlockSpec((sb, hist, cd), lambda i: (i, 0, 0))],
        out_shape=[jax.ShapeDtypeStruct((n_seq * seq_len, cd), BF16),
                   jax.ShapeDtypeStruct((n_seq, hist, cd), F32)],
        scratch_shapes=[pltpu.VMEM((ext_rows, cd), F32), pltpu.VMEM((rows, cd), F32)],
        compiler_params=_params(("parallel",), 32),
        name="conv_sample",
    )(ug, buf, w, wb.reshape(1, cd), g.reshape(1, cd), b.reshape(1, cd))


def _attn_kernel(q_ref, k_ref, v_ref, o_ref, *, heads, seq_rows, scale):
    rows, ca = q_ref.shape
    nseq = rows // seq_rows
    n_mem = k_ref.shape[0] // nseq
    hd = ca // heads
    q = q_ref[...]
    if nseq > 1:
        shift = seq_rows.bit_length() - 1
        row_seq = lax.broadcasted_iota(jnp.int32, (rows, 1), 0) >> shift
    for h in range(heads):
        hs = slice(h * hd, (h + 1) * hd)
        out = None
        for s in range(nseq):
            ms = slice(s * n_mem, (s + 1) * n_mem)
            sc = _dot_nt(q[:, hs], k_ref[ms, hs].astype(BF16)) * scale
            e = jnp.exp(sc - jnp.max(sc, axis=-1, keepdims=True))
            p = e / jnp.sum(e, axis=-1, keepdims=True)
            o = _dot(p.astype(BF16), v_ref[ms, hs].astype(BF16))
            out = o if out is None else jnp.where(row_seq == s, o, out)
        o_ref[:, hs] = out.astype(o_ref.dtype)


def _attn(q, k, v, *, row0, n_seq, seq_len, n_mem, heads, name):
    ca = q.shape[1]
    if seq_len % SUBLANES == 0:
        rows, nseq = _pick(seq_len, 256, 16), 1
    else:
        assert SUBLANES % seq_len == 0
        rows, nseq = 16, 16 // seq_len
    assert row0 % rows == 0 and (n_seq * seq_len) % rows == 0
    blk0 = row0 // rows
    per_seq = max(seq_len // rows, 1)
    kv_spec = pl.BlockSpec((nseq * n_mem, ca), lambda i: (i // per_seq, 0))
    return pl.pallas_call(
        functools.partial(_attn_kernel, heads=heads, seq_rows=seq_len if nseq > 1 else rows,
                          scale=(ca // heads) ** -0.5),
        grid=(n_seq * seq_len // rows,),
        in_specs=[pl.BlockSpec((rows, ca), lambda i: (blk0 + i, 0)), kv_spec, kv_spec],
        out_specs=pl.BlockSpec((rows, ca), lambda i: (i, 0)),
        out_shape=jax.ShapeDtypeStruct((n_seq * seq_len, ca), BF16),
        compiler_params=_params(("parallel",), 32),
        name=name,
    )(q, k, v)


ROUTE_E0, ROUTE_E1, ROUTE_G0, ROUTE_G1 = 0, 1, 2, 3
NORM_ROWS = 32
TOKEN_STRIDE = 40
GATHER_UNROLL = 8


def _chunk(c, t0, n):
    return pl.ds(t0 * TOKEN_STRIDE + c, n, stride=TOKEN_STRIDE), slice(None)


def _token_copies(hbm_ref, hbm_token, buf_ref, t0, n_tokens, chunks, sem, to_hbm):
    def pair(hbm_rows, buf_rows):
        return (buf_rows, hbm_rows) if to_hbm else (hbm_rows, buf_rows)

    def start():
        def body(j, carry):
            for k in range(GATHER_UNROLL):
                r = j * GATHER_UNROLL + k
                src, dst = pair(hbm_ref.at[pl.ds(hbm_token(r) * chunks, chunks), :],
                                buf_ref.at[pl.ds((t0 + r) * TOKEN_STRIDE, chunks), :])
                pltpu.make_async_copy(src, dst, sem).start()
            return carry

        lax.fori_loop(0, n_tokens // GATHER_UNROLL, body, 0)

    def wait():
        src, dst = pair(hbm_ref.at[pl.ds(0, n_tokens * chunks), :],
                        buf_ref.at[pl.ds(t0 * TOKEN_STRIDE, n_tokens * chunks), :])
        pltpu.make_async_copy(src, dst, sem).wait()

    return start, wait


def _router_kernel(x_ref, g_ref, wg_ref, bg_ref, we_ref, be_ref, o_ref, xtok_hbm, tokbuf, sem,
                   *, n_groups, per_group):
    tm, d = x_ref.shape
    chunks = d // LANES
    for c in range(chunks):
        tokbuf[_chunk(c, 0, tm)] = x_ref[:, c * LANES:(c + 1) * LANES]
    row0 = pl.program_id(0) * tm
    start_tok_copy, wait_tok_copy = _token_copies(xtok_hbm, lambda r: row0 + r, tokbuf, 0, tm, chunks, sem,
                                                  to_hbm=True)
    start_tok_copy()
    h = _rmsnorm_rows(x_ref[...], g_ref[...])
    h_hi, h_lo = _split_bf16(h, 2)

    def logits(w_ref, b_ref):
        w_hi, w_lo = _split_bf16(w_ref[...], 2)
        return _dot(h_hi, w_hi) + (_dot(h_hi, w_lo) + _dot(h_lo, w_hi)) + b_ref[...]

    lg = logits(wg_ref, bg_ref)
    le = logits(we_ref, be_ref)
    lane_i = lax.broadcasted_iota(jnp.int32, lg.shape, 1)
    lane = lane_i.astype(F32)
    neg = jnp.float32(-jnp.inf)

    def first_argmax(x, m):
        return jnp.min(jnp.where(x == m, lane, float(LANES)), axis=-1, keepdims=True)

    lg = jnp.where(lane_i < n_groups, lg, neg)
    mg = jnp.max(lg, axis=-1, keepdims=True)
    pg_top = 1.0 / jnp.sum(jnp.exp(lg - mg), axis=-1, keepdims=True)
    gsel = first_argmax(lg, mg)
    shift = per_group.bit_length() - 1
    le = jnp.where((lane_i >> shift).astype(F32) == gsel, le, neg)
    m0 = jnp.max(le, axis=-1, keepdims=True)
    z = jnp.sum(jnp.exp(le - m0), axis=-1, keepdims=True)
    e0 = first_argmax(le, m0)
    le1 = jnp.where(lane == e0, neg, le)
    m1 = jnp.max(le1, axis=-1, keepdims=True)
    e1 = first_argmax(le1, m1)
    p0 = 1.0 / z
    p1 = jnp.exp(m1 - m0) / z
    den = p0 + p1
    g0 = pg_top * (p0 / den)
    g1 = pg_top * (p1 / den)
    rec = jnp.where(lane_i == ROUTE_E0, e0, 0.0)
    rec = jnp.where(lane_i == ROUTE_E1, e1, rec)
    rec = jnp.where(lane_i == ROUTE_G0, g0, rec)
    rec = jnp.where(lane_i == ROUTE_G1, g1, rec)
    o_ref[...] = rec
    wait_tok_copy()


def _router(x, g, w_group, b_group, w_expert, b_expert):
    t, d = x.shape
    n_groups, n_experts = w_group.shape[1], w_expert.shape[1]
    per_group = n_experts // n_groups
    assert n_experts <= LANES and per_group & (per_group - 1) == 0
    pad = lambda a: jnp.pad(a, ((0, 0), (0, LANES - a.shape[1])))
    assert d // LANES <= TOKEN_STRIDE
    tm = _pick(t, 256, GATHER_UNROLL)
    wspec = pl.BlockSpec((d, LANES), lambda i: (0, 0))
    bspec = pl.BlockSpec((1, LANES), lambda i: (0, 0))
    return pl.pallas_call(
        functools.partial(_router_kernel, n_groups=n_groups, per_group=per_group),
        grid=(t // tm,),
        in_specs=[pl.BlockSpec((tm, d), lambda i: (i, 0)), pl.BlockSpec((1, d), lambda i: (0, 0)),
                  wspec, bspec, wspec, bspec],
        out_specs=[pl.BlockSpec((tm, LANES), lambda i: (i, 0)), pl.BlockSpec(memory_space=pl.ANY)],
        out_shape=[jax.ShapeDtypeStruct((t, LANES), F32), jax.ShapeDtypeStruct((t * (d // LANES), LANES), F32)],
        scratch_shapes=[pltpu.VMEM((tm * TOKEN_STRIDE, LANES), F32), pltpu.SemaphoreType.DMA(())],
        compiler_params=_params(("parallel",), 32),
        name="router",
    )(x, g.reshape(1, d), pad(w_group), pad(b_group.reshape(1, -1)), pad(w_expert), pad(b_expert.reshape(1, -1)))


def _dispatch_plan(e_ids, n_experts, tile):
    t, k = e_ids.shape
    n_pairs = t * k
    n_tiles = n_pairs // tile + n_experts
    flat = e_ids.reshape(n_pairs)
    onehot = (flat[:, None] == jnp.arange(n_experts, dtype=jnp.int32)[None, :]).astype(jnp.int32)
    counts = jnp.sum(onehot, axis=0)
    tiles_per = (counts + tile - 1) // tile
    tile_end = jnp.cumsum(tiles_per)
    tile_start = tile_end - tiles_per
    rank = jnp.take_along_axis(jnp.cumsum(onehot, axis=0), flat[:, None], axis=1)[:, 0] - 1
    pos = tile_start[flat] * tile + rank
    token = jnp.arange(n_pairs, dtype=jnp.int32) // k
    src = jnp.zeros((n_tiles * tile,), jnp.int32).at[pos].set(token)
    n_used = tile_end[-1]
    tile_ids = jnp.minimum(jnp.arange(n_tiles, dtype=jnp.int32), n_used - 1)
    tile_expert = jnp.searchsorted(tile_end, tile_ids, side="right").astype(jnp.int32)
    after = tile_end[tile_expert]
    next_expert = jnp.where(after < n_used, tile_expert[jnp.minimum(after, n_tiles - 1)], -1).astype(jnp.int32)
    return (pos.reshape(t, k), src.reshape(n_tiles, 1, tile), tile_expert, next_expert,
            n_used.reshape(1).astype(jnp.int32))


def _cast_rows(src_ref, dst_ref, rows_per):
    def body(c, carry):
        rows = pl.ds(pl.multiple_of(c * rows_per, rows_per), rows_per)
        dst_ref[rows, :] = src_ref[rows, :].astype(dst_ref.dtype)
        return carry

    lax.fori_loop(0, src_ref.shape[0] // rows_per, body, 0)


DOWN_COLS = 1024


def _experts_kernel(texp_ref, nexp_ref, nused_ref, idx_ref, idx_next_ref, x_hbm, g_ref, wg_hbm, wu_hbm, wd_hbm,
                    y_hbm, xbuf, ybuf, hbuf, stage_g, stage_u, stage_d, wg_bf, wu_bf, wd_bf, gsem, ysem, wsem):
    i = pl.program_id(0)
    n_used = nused_ref[0]
    tile, d = hbuf.shape
    chunks = d // LANES
    slot = lax.rem(i, 2)
    expert = texp_ref[i]
    start_tokens, wait_tokens = _token_copies(x_hbm, lambda r: idx_ref[0, 0, r], xbuf, slot * tile, tile, chunks,
                                              gsem.at[slot], to_hbm=False)
    start_next_tokens, _ = _token_copies(x_hbm, lambda r: idx_next_ref[0, 0, r], xbuf, (1 - slot) * tile, tile,
                                         chunks, gsem.at[1 - slot], to_hbm=False)

    def weight_copies(e):
        return (pltpu.make_async_copy(wg_hbm.at[e], stage_g, wsem.at[0]),
                pltpu.make_async_copy(wu_hbm.at[e], stage_u, wsem.at[1]),
                pltpu.make_async_copy(wd_hbm.at[e], stage_d, wsem.at[2]))

    def y_copy(step):
        return _token_copies(y_hbm, lambda r: step * tile + r, ybuf, 0, tile, chunks, ysem, to_hbm=True)

    @pl.when(i < n_used)
    def _():
        @pl.when(i == 0)
        def _():
            start_tokens()
            for cp in weight_copies(expert):
                cp.start()

        @pl.when(i + 1 < n_used)
        def _():
            start_next_tokens()

        @pl.when(jnp.logical_or(i == 0, expert != texp_ref[jnp.maximum(i - 1, 0)]))
        def _():
            for cp in weight_copies(expert):
                cp.wait()
            _cast_rows(stage_g, wg_bf, 512)
            _cast_rows(stage_u, wu_bf, 512)
            _cast_rows(stage_d, wd_bf, 64)
            nxt = nexp_ref[i]

            @pl.when(nxt >= 0)
            def _():
                for cp in weight_copies(nxt):
                    cp.start()

        wait_tokens()

        def norm_rows(rc, carry):
            t0 = slot * tile + rc * NORM_ROWS
            ssq = jnp.zeros((NORM_ROWS, LANES), F32)
            for c in range(chunks):
                xc = xbuf[_chunk(c, t0, NORM_ROWS)]
                ssq = ssq + xc * xc
            inv = lax.rsqrt(jnp.sum(ssq, axis=-1, keepdims=True) * (1.0 / d) + EPS)
            rows = pl.ds(pl.multiple_of(rc * NORM_ROWS, NORM_ROWS), NORM_ROWS)
            for c in range(chunks):
                lanes = slice(c * LANES, (c + 1) * LANES)
                hbuf[rows, lanes] = (xbuf[_chunk(c, t0, NORM_ROWS)] * inv * g_ref[:, lanes]).astype(BF16)
            return carry

        lax.fori_loop(0, tile // NORM_ROWS, norm_rows, 0)
        h = hbuf[...]
        a = _dot(h, wg_bf[...])
        u = _dot(h, wu_bf[...])
        hid = (_silu(a) * u).astype(BF16)

        @pl.when(i > 0)
        def _():
            y_copy(i - 1)[1]()

        for c0 in range(0, d, DOWN_COLS):
            y = _dot(hid, wd_bf[:, c0:c0 + DOWN_COLS])
            for c in range(DOWN_COLS // LANES):
                ybuf[_chunk(c0 // LANES + c, 0, tile)] = y[:, c * LANES:(c + 1) * LANES]

    @pl.when(i >= n_used)
    def _():
        y_copy(i - 1)[1]()

        @pl.when(i == n_used)
        def _():
            ybuf[...] = jnp.zeros(ybuf.shape, F32)

    y_copy(i)[0]()

    @pl.when(i == pl.num_programs(0) - 1)
    def _():
        y_copy(i)[1]()


def _experts(x_tok, g, src, tile_expert, next_expert, n_used, wg, wu, wd):
    d = g.shape[0]
    chunks = d // LANES
    n_tiles, _, tile = src.shape
    de = wg.shape[2]
    assert d % 512 == 0 and de % 64 == 0 and d % DOWN_COLS == 0 and chunks <= TOKEN_STRIDE
    assert tile % NORM_ROWS == 0 and tile % GATHER_UNROLL == 0
    idx_spec = lambda ahead: pl.BlockSpec(
        (1, 1, tile), lambda i, texp, nexp, nused: (jnp.minimum(i + ahead, nused[0] - 1), 0, 0),
        memory_space=pltpu.SMEM)
    hbm = pl.BlockSpec(memory_space=pl.ANY)
    grid_spec = pltpu.PrefetchScalarGridSpec(
        num_scalar_prefetch=3,
        grid=(n_tiles,),
        in_specs=[idx_spec(0), idx_spec(1), hbm, pl.BlockSpec((1, d), lambda i, texp, nexp, nused: (0, 0)),
                  hbm, hbm, hbm],
        out_specs=hbm,
        scratch_shapes=[pltpu.VMEM((2 * tile * TOKEN_STRIDE, LANES), F32),
                        pltpu.VMEM((tile * TOKEN_STRIDE, LANES), F32), pltpu.VMEM((tile, d), BF16),
                        pltpu.VMEM((d, de), F32), pltpu.VMEM((d, de), F32), pltpu.VMEM((de, d), F32),
                        pltpu.VMEM((d, de), BF16), pltpu.VMEM((d, de), BF16), pltpu.VMEM((de, d), BF16),
                        pltpu.SemaphoreType.DMA((2,)), pltpu.SemaphoreType.DMA(()), pltpu.SemaphoreType.DMA((3,))],
    )
    weights_bytes = 3 * d * de * (4 + 2)
    tiles_bytes = 3 * tile * TOKEN_STRIDE * LANES * 4 + tile * d * 2 + tile * DOWN_COLS * 4 + 3 * tile * de * 4
    return pl.pallas_call(
        _experts_kernel,
        grid_spec=grid_spec,
        out_shape=jax.ShapeDtypeStruct((n_tiles * tile * chunks, LANES), F32),
        compiler_params=_params(("arbitrary",), (weights_bytes + tiles_bytes) // MIB + 3),
        name="experts",
    )(tile_expert, next_expert, n_used, src, src, x_tok, g.reshape(1, d), wg, wu, wd)


def _combine_kernel(idx0_ref, idx1_ref, idx0_next_ref, idx1_next_ref, y_hbm, x_ref, route_ref, gf_ref, *refs,
                    n_first, final_norm):
    outs, (buf0, buf1, res_ref, sem) = refs[:-4], refs[-4:]
    i = pl.program_id(0)
    tm, d = x_ref.shape
    chunks = d // LANES
    slot = lax.rem(i, 2)
    def gather(idx_ref, buf, k, into_slot):
        return _token_copies(y_hbm, lambda r: idx_ref[0, 0, r], buf, into_slot * tm, tm, chunks,
                             sem.at[k, into_slot], to_hbm=False)

    gathers = [gather(idx0_ref, buf0, 0, slot), gather(idx1_ref, buf1, 1, slot)]
    gathers_next = [gather(idx0_next_ref, buf0, 0, 1 - slot), gather(idx1_next_ref, buf1, 1, 1 - slot)]

    @pl.when(i == 0)
    def _():
        for start, _ in gathers:
            start()

    @pl.when(i + 1 < pl.num_programs(0))
    def _():
        for start, _ in gathers_next:
            start()

    for _, wait in gathers:
        wait()
    g0 = route_ref[:, ROUTE_G0:ROUTE_G0 + 1]
    g1 = route_ref[:, ROUTE_G1:ROUTE_G1 + 1]
    ssq = jnp.zeros((tm, LANES), F32)
    for c in range(chunks):
        lanes = slice(c * LANES, (c + 1) * LANES)
        y0 = buf0[_chunk(c, slot * tm, tm)]
        y1 = buf1[_chunk(c, slot * tm, tm)]
        r = x_ref[:, lanes] + (g0 * y0 + g1 * y1)
        res_ref[:, lanes] = r
        ssq = ssq + r * r
    if final_norm:
        scale = lax.rsqrt(jnp.sum(ssq, axis=-1, keepdims=True) * (1.0 / d) + EPS)
        result = lambda: res_ref[...] * scale * gf_ref[...]
    else:
        result = lambda: res_ref[...]
    if len(outs) == 1:
        outs[0][...] = result()
    else:
        @pl.when(i < n_first)
        def _():
            outs[0][...] = result()

        @pl.when(i >= n_first)
        def _():
            outs[1][...] = result()


def _combine(x, y_sorted, pos, route, g_final, *, split_rows, final_norm):
    t, d = x.shape
    assert y_sorted.shape[1] == LANES
    tm = _pick(t if split_rows is None else math.gcd(split_rows, t - split_rows), 128, GATHER_UNROLL)
    n_blk = t // tm
    idx = [pos[:, k].reshape(n_blk, 1, tm) for k in range(TOP_K_IN_GROUP)]
    smem = pl.BlockSpec((1, 1, tm), lambda i: (i, 0, 0), memory_space=pltpu.SMEM)
    smem_next = pl.BlockSpec((1, 1, tm), lambda i: (jnp.minimum(i + 1, n_blk - 1), 0, 0), memory_space=pltpu.SMEM)
    row = pl.BlockSpec((tm, d), lambda i: (i, 0))
    if split_rows is None:
        n_first = n_blk
        out_specs = [row]
        out_shape = [jax.ShapeDtypeStruct((t, d), F32)]
    else:
        n_first = split_rows // tm
        out_specs = [pl.BlockSpec((tm, d), lambda i: (jnp.minimum(i, n_first - 1), 0)),
                     pl.BlockSpec((tm, d), lambda i: (jnp.maximum(i - n_first, 0), 0))]
        out_shape = [jax.ShapeDtypeStruct((split_rows, d), F32), jax.ShapeDtypeStruct((t - split_rows, d), F32)]
    return pl.pallas_call(
        functools.partial(_combine_kernel, n_first=n_first, final_norm=final_norm),
        grid=(n_blk,),
        in_specs=[smem, smem, smem_next, smem_next, pl.BlockSpec(memory_space=pl.ANY), row,
                  pl.BlockSpec((tm, LANES), lambda i: (i, 0)), pl.BlockSpec((1, d), lambda i: (0, 0))],
        out_specs=out_specs,
        out_shape=out_shape,
        scratch_shapes=[pltpu.VMEM((2 * tm * TOKEN_STRIDE, LANES), F32),
                        pltpu.VMEM((2 * tm * TOKEN_STRIDE, LANES), F32), pltpu.VMEM((tm, d), F32),
                        pltpu.SemaphoreType.DMA((TOP_K_IN_GROUP, 2))],
        compiler_params=_params(("arbitrary",), 32),
        name="combine",
    )(idx[0], idx[1], idx[0], idx[1], y_sorted, x, route, g_final.reshape(1, d))


def kernel(x_prompt, x_sample, mem_prompt, state_gla, state_conv, cache_mem_k, cache_mem_v, norm_mix_g, w_in, w_alpha_up, b_alpha, gla_norm_g, w_branch_a, conv_dw_w, conv_dw_b, conv_ln_g, conv_ln_b, w_branch_b, w_out, norm_ca_g, norm_mem_g, w_ca_q, w_ca_k, w_ca_v, w_ca_o, norm_ffn_g, w_router_group, b_router_group, w_router_expert, b_router_expert, w_exp_gate, w_exp_up, w_exp_down, norm_final_g):
    depth = w_in.shape[0]
    bp, tp, d = x_prompt.shape
    bs, ts, _ = x_sample.shape
    heads, dk, dv = state_gla.shape[2:]
    key, val = heads * dk, heads * dv
    rank = w_alpha_up.shape[1]
    cd = state_conv.shape[3]
    n_mem, ca_heads, ca_hd = cache_mem_k.shape[2:]
    ca = ca_heads * ca_hd
    n_experts = w_router_expert.shape[2]
    rows_p, rows_s = bp * tp, bs * ts
    off_a = 2 * key + 2 * val
    off_u = off_a + rank
    off_g = off_u + 2 * cd
    assert rank <= LANES and w_in.shape[2] == off_g + 2 * d
    bf = lambda a: a.astype(BF16)

    x = (x_prompt.reshape(rows_p, d), x_sample.reshape(rows_s, d))
    outs = dict(gla_p=[], conv_p=[], mk_p=[], mv_p=[], gla_s=[], conv_s=[])
    for l in range(depth):
        w_qkvr = _cast_cols(w_in, l, 0, off_a)
        w_ug = _cast_cols(w_in, l, off_u, 2 * cd + 2 * d)
        w_a = bf(jnp.pad(w_in[l, :, off_a:off_u], ((0, 0), (0, LANES - rank))))
        h = _rmsnorm(x, norm_mix_g[l], BF16)
        z = _mm([h], [w_qkvr], [0], _ep_identity, F32, name="in_qkvr")
        a_low = _mm([h], [w_a], [0], _ep_identity, F32, name="in_alow")
        ug = _mm([h, h], [w_ug, w_ug], [0, 0], _ep_glu, F32, n=cd, w_col0=[0, cd], name="in_glu")
        w_up = bf(jnp.pad(w_alpha_up[l], ((0, LANES - rank), (0, 0))))
        gla_args = dict(heads=heads, dk=dk, dv=dv)
        og_p, st_p = _gla(z, a_low, w_up, b_alpha[l], gla_norm_g[l], None, row0=0, n_seq=bp, seq_len=tp,
                          name="gla_prompt", **gla_args)
        og_s, st_s = _gla(z, a_low, w_up, b_alpha[l], gla_norm_g[l], state_gla[l], row0=rows_p, n_seq=bs,
                          seq_len=ts, name="gla_sample", **gla_args)
        conv_w = (conv_dw_w[l], conv_dw_b[l], conv_ln_g[l], conv_ln_b[l])
        c_p = _conv_prompt(ug, *conv_w, n_seq=bp, seq_len=tp)
        c_s, buf_s = _conv_sample(ug, state_conv[l], *conv_w, row0=rows_p, seq_len=ts)
        og = (og_p, og_s)
        c = (c_p, c_s)
        merged = _mm([og, c, h], [bf(w_branch_a[l]), bf(w_branch_b[l]), w_ug, w_ug], [0, 1, 2, 2],
                     _ep_gated_merge, BF16, n=d, w_col0=[0, 0, 2 * cd, 2 * cd + d], tn_pref=256, name="merge")
        x = _mm([merged], [bf(w_out[l])], [0], _ep_residual, F32, extras=[x], name="out_proj")
        h = _rmsnorm(x, norm_ca_g[l], BF16)
        q = _mm([h], [bf(w_ca_q[l])], [0], _ep_identity, BF16, name="ca_q")
        m = _rmsnorm(mem_prompt.reshape(bp * n_mem, d), norm_mem_g[l], BF16)
        mk = _mm([m], [bf(w_ca_k[l])], [0], _ep_identity, F32, name="mem_k")
        mv = _mm([m], [bf(w_ca_v[l])], [0], _ep_identity, F32, name="mem_v")
        ao_p = _attn(q, mk, mv, row0=0, n_seq=bp, seq_len=tp, n_mem=n_mem, heads=ca_heads, name="attn_prompt")
        ao_s = _attn(q, cache_mem_k[l].reshape(bs * n_mem, ca), cache_mem_v[l].reshape(bs * n_mem, ca),
                     row0=rows_p, n_seq=bs, seq_len=ts, n_mem=n_mem, heads=ca_heads, name="attn_sample")
        ao = (ao_p, ao_s)
        x = _mm([ao], [bf(w_ca_o[l])], [0], _ep_residual, F32, extras=[x], name="ca_out")
        route, x_tok = _router(x, norm_ffn_g[l], w_router_group[l], b_router_group[l], w_router_expert[l],
                               b_router_expert[l])
        e_ids = route[:, ROUTE_E0:ROUTE_E1 + 1].astype(jnp.int32)
        tile = _pick(rows_p + rows_s, 256, SUBLANES)
        pos, src, tile_expert, next_expert, n_used = _dispatch_plan(e_ids, n_experts, tile)
        y_sorted = _experts(x_tok, norm_ffn_g[l], src, tile_expert, next_expert, n_used, w_exp_gate[l],
                            w_exp_up[l], w_exp_down[l])
        last = l == depth - 1
        res = _combine(x, y_sorted, pos, route, norm_final_g, split_rows=rows_p if last else None,
                       final_norm=last)
        if not last:
            x = res[0]
        outs["gla_p"].append(st_p)
        hist = conv_dw_w.shape[1] - 1
        outs["conv_p"].append(jnp.stack([ug[(b + 1) * tp - hist:(b + 1) * tp] for b in range(bp)]))
        outs["mk_p"].append(mk.reshape(bp, n_mem, ca_heads, ca_hd))
        outs["mv_p"].append(mv.reshape(bp, n_mem, ca_heads, ca_hd))
        outs["gla_s"].append(st_s)
        outs["conv_s"].append(buf_s)
    y_prompt = res[0].reshape(bp, tp, d)
    y_sample = res[1].reshape(bs, ts, d)
    return (y_prompt, y_sample, jnp.stack(outs["gla_p"]), jnp.stack(outs["conv_p"]), jnp.stack(outs["mk_p"]),
            jnp.stack(outs["mv_p"]), jnp.stack(outs["gla_s"]), jnp.stack(outs["conv_s"]))
```

```python
import functools
import math

import jax
import jax.numpy as jnp
from jax import lax
from jax.experimental import pallas as pl
from jax.experimental.pallas import tpu as pltpu

F32 = jnp.float32
BF16 = jnp.bfloat16

EPS = 1e-6
GLA_TAU = 16.0
GLA_CHUNK = 64
TOP_K_IN_GROUP = 2
LANES = 128
SUBLANES = 8
MIB = 1 << 20


def _pick(n, pref, mult):
    for d in range(min(pref, n), 0, -1):
        if n % d == 0 and d % mult == 0:
            return d
    return n


def _params(semantics, vmem_mib):
    return pltpu.CompilerParams(dimension_semantics=semantics, vmem_limit_bytes=vmem_mib * MIB)


def _sigmoid(x):
    return 1.0 / (1.0 + jnp.exp(-x))


def _silu(x):
    return x * _sigmoid(x)


def _dot(a, b):
    return jnp.dot(a, b, preferred_element_type=F32)


def _dot_nt(a, b):
    return lax.dot_general(a, b, (((1,), (1,)), ((), ())), preferred_element_type=F32)


def _dot_tn(a, b):
    return lax.dot_general(a, b, (((0,), (0,)), ((), ())), preferred_element_type=F32)


def _split_bf16(x, parts):
    out = []
    for _ in range(parts - 1):
        p = x.astype(BF16)
        out.append(p)
        x = x - p.astype(F32)
    out.append(x.astype(BF16))
    return out


def _rmsnorm_rows(x, g):
    return x * lax.rsqrt(jnp.mean(x * x, axis=-1, keepdims=True) + EPS) * g


def _parts(a):
    return a if isinstance(a, tuple) else (a,)


def _rows_of(a):
    return sum(p.shape[0] for p in _parts(a))


def _part_specs(a, tm, cols, col_of):
    parts = _parts(a)
    if len(parts) == 1:
        return [pl.BlockSpec((tm, cols), lambda i, *j: (i, col_of(*j)))]
    n_first = parts[0].shape[0] // tm
    assert parts[0].shape[0] % tm == 0 and parts[1].shape[0] % tm == 0
    return [pl.BlockSpec((tm, cols), lambda i, *j: (jnp.minimum(i, n_first - 1), col_of(*j))),
            pl.BlockSpec((tm, cols), lambda i, *j: (jnp.maximum(i - n_first, 0), col_of(*j)))]


def _part_tile(a, pref, mult):
    return _pick(math.gcd(*[p.shape[0] for p in _parts(a)]), pref, mult)


def _select_part(refs, n_first):
    if len(refs) == 1:
        return refs[0][...]
    return jnp.where(pl.program_id(0) < n_first, refs[0][...], refs[1][...])


def _rmsnorm_kernel(*refs, n_first):
    x_refs, (g_ref, o_ref) = refs[:-2], refs[-2:]
    o_ref[...] = _rmsnorm_rows(_select_part(x_refs, n_first), g_ref[...]).astype(o_ref.dtype)


def _rmsnorm(x, g, out_dtype):
    t, d = _rows_of(x), _parts(x)[0].shape[1]
    tm = _part_tile(x, 128, 16)
    return pl.pallas_call(
        functools.partial(_rmsnorm_kernel, n_first=_parts(x)[0].shape[0] // tm),
        grid=(t // tm,),
        in_specs=_part_specs(x, tm, d, lambda: 0) + [pl.BlockSpec((1, d), lambda i: (0, 0))],
        out_specs=pl.BlockSpec((tm, d), lambda i: (i, 0)),
        out_shape=jax.ShapeDtypeStruct((t, d), out_dtype),
        compiler_params=_params(("parallel",), 32),
        name="rmsnorm",
    )(*_parts(x), g.reshape(1, d))


def _cast_cols_kernel(a_ref, o_ref):
    o_ref[...] = a_ref[...].T.astype(BF16)


def _cast_cols(w_t, col0, n_cols):
    _, k = w_t.shape
    tn = _pick(n_cols, 1024, LANES)
    tk = _pick(k, 1024, LANES)
    assert col0 % SUBLANES == 0
    return pl.pallas_call(
        _cast_cols_kernel,
        grid=(n_cols // tn, k // tk),
        in_specs=[pl.BlockSpec((pl.Element(tn), pl.Element(tk)),
                               lambda j, r: (pl.multiple_of(col0 + j * tn, SUBLANES), r * tk))],
        out_specs=pl.BlockSpec((tk, tn), lambda j, r: (r, j)),
        out_shape=jax.ShapeDtypeStruct((k, n_cols), BF16),
        compiler_params=_params(("parallel", "parallel"), 32),
        name="cast_cols",
    )(w_t)


def _ep_identity(accs, extras):
    return accs[0]


def _ep_glu(accs, extras):
    return accs[0] * _sigmoid(accs[1])


def _ep_residual(accs, extras):
    return extras[0] + accs[0]


def _ep_gated_merge(accs, extras):
    return _sigmoid(accs[2]) * accs[0] + _sigmoid(accs[3]) * accs[1]


WIDE_ROWS = 1088


def _mm_kernel(*refs, x_of_w, x_parts, e_parts, n_first, epilogue):
    refs = list(refs)
    take = lambda k: [refs.pop(0) for _ in range(k)]
    x_refs = [take(p) for p in x_parts]
    w_refs = take(len(x_of_w))
    e_refs = [take(p) for p in e_parts]
    o_ref = refs.pop(0)
    i = pl.program_id(0)
    xs = []
    for parts in x_refs:
        if len(parts) == 1:
            xs.append(parts[0])
            continue
        scratch = refs.pop(0)

        @pl.when(jnp.logical_and(pl.program_id(1) == 0, i < n_first))
        def _(scratch=scratch, src=parts[0]):
            scratch[...] = src[...]

        @pl.when(jnp.logical_and(pl.program_id(1) == 0, i >= n_first))
        def _(scratch=scratch, src=parts[1]):
            scratch[...] = src[...]

        xs.append(scratch)
    accs = [_dot(xs[x_of_w[k]][...], w_refs[k][...]) for k in range(len(x_of_w))]
    o_ref[...] = epilogue(accs, [_select_part(e, n_first) for e in e_refs]).astype(o_ref.dtype)


def _mm(xs, ws, x_of_w, epilogue, out_dtype, extras=(), tm_pref=512, tn_pref=512, name="mm", n=None, w_col0=None):
    t = _rows_of(xs[0])
    n = ws[0].shape[1] if n is None else n
    w_col0 = [0] * len(ws) if w_col0 is None else w_col0
    tm = min(_part_tile(a, tm_pref, 16) for a in list(xs) + list(extras))
    tn = _pick(math.gcd(n, *w_col0), tn_pref, LANES)
    split = [a for a in list(xs) + list(extras) if len(_parts(a)) == 2]
    assert all(a[0].shape[0] == split[0][0].shape[0] and a[0].shape[0] % tm == 0 for a in split)
    n_first = split[0][0].shape[0] // tm if split else 0
    k_of = lambda x: _parts(x)[0].shape[1]
    in_specs, args = [], []
    for x in xs:
        in_specs += _part_specs(x, tm, k_of(x), lambda j: 0)
        args += _parts(x)
    in_specs += [pl.BlockSpec((w.shape[0], tn), lambda i, j, b0=c0 // tn: (0, b0 + j)) for w, c0 in zip(ws, w_col0)]
    args += ws
    for e in extras:
        in_specs += _part_specs(e, tm, tn, lambda j: j)
        args += _parts(e)
    scratch = [pltpu.VMEM((tm, k_of(x)), _parts(x)[0].dtype) for x in xs if len(_parts(x)) == 2]
    block_bytes = sum(tm * p.shape[1] * p.dtype.itemsize for x in xs for p in _parts(x))
    block_bytes += sum(w.shape[0] * tn * w.dtype.itemsize for w in ws)
    block_bytes += sum(tm * tn * p.dtype.itemsize for e in extras for p in _parts(e))
    block_bytes += tm * tn * jnp.dtype(out_dtype).itemsize
    scratch_bytes = sum(tm * k_of(x) * _parts(x)[0].dtype.itemsize for x in xs if len(_parts(x)) == 2)
    acc_bytes = len(ws) * tm * tn * 4
    vmem = (2 * block_bytes + 2 * acc_bytes + scratch_bytes) // MIB + 4
    return pl.pallas_call(
        functools.partial(_mm_kernel, x_of_w=tuple(x_of_w), x_parts=tuple(len(_parts(x)) for x in xs),
                          e_parts=tuple(len(_parts(e)) for e in extras), n_first=n_first, epilogue=epilogue),
        grid=(t // tm, n // tn),
        in_specs=in_specs,
        out_specs=pl.BlockSpec((tm, tn), lambda i, j: (i, j)),
        out_shape=jax.ShapeDtypeStruct((t, n), out_dtype),
        scratch_shapes=scratch,
        compiler_params=_params(("parallel", "arbitrary"), vmem),
        name=name,
    )(*args)


def _log_sigmoid(x):
    return jnp.minimum(x, 0.0) - jnp.log1p(jnp.exp(-jnp.abs(x)))


def _gla_kernel(*refs, heads, seq_rows, has_h0, scale):
    if has_h0:
        q_ref, k_ref, v_ref, r_ref, a_ref, wup_ref, bal_ref, gn_ref, h0_ref, og_ref, st_ref = refs
    else:
        q_ref, k_ref, v_ref, r_ref, a_ref, wup_ref, bal_ref, gn_ref, og_ref, st_ref = refs
    rows, key = q_ref.shape
    val = v_ref.shape[1]
    dk, dv = key // heads, val // heads
    nseq = rows // seq_rows

    @pl.when(pl.program_id(1) == 0)
    def _():
        if has_h0:
            st_ref[...] = h0_ref[...]
        else:
            st_ref[...] = jnp.zeros(st_ref.shape, F32)

    la = _dot(a_ref[...].astype(BF16), wup_ref[...]) + bal_ref[...]
    log_a = _log_sigmoid(la) * (1.0 / GLA_TAU)

    ri = lax.broadcasted_iota(jnp.int32, (rows, rows), 0)
    ci = lax.broadcasted_iota(jnp.int32, (rows, rows), 1)
    causal = ci <= ri
    if nseq > 1:
        shift = seq_rows.bit_length() - 1
        causal = jnp.logical_and(causal, (ri >> shift) == (ci >> shift))
        row_seq = lax.broadcasted_iota(jnp.int32, (rows, 1), 0) >> shift
    tri = jnp.where(causal, 1.0, 0.0).astype(BF16)
    b = sum(_dot(tri, part) for part in _split_bf16(log_a, 3))

    q = q_ref[...] * scale
    k = k_ref[...]
    v = v_ref[...]
    r = r_ref[...]
    gn = gn_ref[...]
    for h in range(heads):
        ks = slice(h * dk, (h + 1) * dk)
        vs = slice(h * dv, (h + 1) * dv)
        bh = b[:, ks]
        q_e = (q[:, ks] * jnp.exp(bh)).astype(BF16)
        k_e = (k[:, ks] * jnp.exp(-bh)).astype(BF16)
        vh = v[:, vs].astype(BF16)
        scores = jnp.where(causal, _dot_nt(q_e, k_e), 0.0)
        o = _dot(scores.astype(BF16), vh)
        for s in range(nseq):
            b_last = bh[(s + 1) * seq_rows - 1:(s + 1) * seq_rows, :]
            k_s = k[:, ks] * jnp.exp(b_last - bh)
            if nseq > 1:
                in_seq = row_seq == s
                k_s = jnp.where(in_seq, k_s, 0.0)
            state = st_ref[s, h]
            o_inter = _dot(q_e, state.astype(BF16))
            o = o + (jnp.where(in_seq, o_inter, 0.0) if nseq > 1 else o_inter)
            d_state = _dot_tn(k_s.astype(BF16), vh)
            decay_col = jnp.broadcast_to(jnp.exp(b_last), (LANES, dk)).T[:, 0:1]
            st_ref[s, h] = decay_col * state + d_state
        mu = jnp.mean(o, axis=-1, keepdims=True)
        oc = o - mu
        o_n = oc * lax.rsqrt(jnp.mean(oc * oc, axis=-1, keepdims=True) + EPS) * gn[:, vs]
        og_ref[:, vs] = (o_n * _silu(r[:, vs])).astype(og_ref.dtype)


def _gla(z, a_low, w_up, b_alpha, gn, h0, *, row0, n_seq, seq_len, heads, dk, dv, name):
    key, val = heads * dk, heads * dv
    assert val == 2 * key and z.shape[1] == 2 * key + 2 * val
    chunk = min(GLA_CHUNK, seq_len)
    assert seq_len % chunk == 0
    if chunk % SUBLANES == 0:
        rows, n_chunks = chunk, seq_len // chunk
    else:
        assert SUBLANES % seq_len == 0 and chunk == seq_len
        rows, n_chunks = SUBLANES, 1
    nseq = rows // chunk
    assert n_seq % nseq == 0 and row0 % rows == 0
    blk0 = row0 // rows
    row_map = lambda g, c: blk0 + g * n_chunks + c
    in_specs = [
        pl.BlockSpec((rows, key), lambda g, c: (row_map(g, c), 0)),
        pl.BlockSpec((rows, key), lambda g, c: (row_map(g, c), 1)),
        pl.BlockSpec((rows, val), lambda g, c: (row_map(g, c), 1)),
        pl.BlockSpec((rows, val), lambda g, c: (row_map(g, c), 2)),
        pl.BlockSpec((rows, LANES), lambda g, c: (row_map(g, c), 0)),
        pl.BlockSpec((LANES, key), lambda g, c: (0, 0)),
        pl.BlockSpec((1, key), lambda g, c: (0, 0)),
        pl.BlockSpec((1, val), lambda g, c: (0, 0)),
    ]
    args = [z, z, z, z, a_low, w_up, b_alpha.reshape(1, key), gn.reshape(1, val)]
    state_spec = pl.BlockSpec((nseq, heads, dk, dv), lambda g, c: (g, 0, 0, 0))
    if h0 is not None:
        in_specs.append(state_spec)
        args.append(h0)
    n_rows = n_seq * seq_len
    return pl.pallas_call(
        functools.partial(_gla_kernel, heads=heads, seq_rows=chunk, has_h0=h0 is not None, scale=dk ** -0.5),
        grid=(n_seq // nseq, n_chunks),
        in_specs=in_specs,
        out_specs=[pl.BlockSpec((rows, val), lambda g, c: (g * n_chunks + c, 0)), state_spec],
        out_shape=[jax.ShapeDtypeStruct((n_rows, val), BF16),
                   jax.ShapeDtypeStruct((n_seq, heads, dk, dv), F32)],
        compiler_params=_params(("parallel", "arbitrary"), 48),
        name=name,
    )(*args)


CONV_LANES = 512
CONV_ROWS = 32
HIST_ROWS = 32


def _ln_silu(x, g, b):
    mu = jnp.mean(x, axis=-1, keepdims=True)
    xc = x - mu
    var = jnp.mean(xc * xc, axis=-1, keepdims=True)
    return _silu(xc * lax.rsqrt(var + EPS) * g + b)


def _conv_strip(ext_ref, w_ref, row_start, n_rows, lanes, width):
    n_lanes = lanes.stop - lanes.start
    total = None
    for r in range(SUBLANES):
        taps = [j for j in range(width) if (row_start + j) % SUBLANES == r]
        if not taps:
            continue
        rows = n_rows if r == 0 else n_rows + SUBLANES
        acc = jnp.zeros((rows, n_lanes), F32)
        for j in taps:
            start = row_start + j - r
            acc = acc + ext_ref[start:start + rows, lanes] * w_ref[j:j + 1, lanes]
        part = acc[r:r + n_rows, :]
        total = part if total is None else total + part
    return total


def _conv_prompt_kernel(u_ref, w_ref, wb_ref, g_ref, b_ref, c_ref, ext_ref, conv_ref, *, width):
    tb, cd = u_ref.shape
    first = HIST_ROWS - (width - 1)

    @pl.when(pl.program_id(1) == 0)
    def _():
        ext_ref[0:HIST_ROWS, :] = jnp.zeros((HIST_ROWS, cd), F32)

    ext_ref[HIST_ROWS:HIST_ROWS + tb, :] = u_ref[...]
    for r0 in range(0, tb, CONV_ROWS):
        for c0 in range(0, cd, CONV_LANES):
            lanes = slice(c0, c0 + CONV_LANES)
            conv_ref[r0:r0 + CONV_ROWS, lanes] = _conv_strip(ext_ref, w_ref, first + r0, CONV_ROWS, lanes, width)
    ext_ref[0:HIST_ROWS, :] = ext_ref[tb:tb + HIST_ROWS, :]
    c_ref[...] = _ln_silu(conv_ref[...] + wb_ref[...], g_ref[...], b_ref[...]).astype(c_ref.dtype)


def _conv_prompt(ug, w, wb, g, b, *, n_seq, seq_len):
    width, cd = w.shape
    assert width - 1 <= HIST_ROWS and cd % CONV_LANES == 0
    tb = _pick(seq_len, 64, CONV_ROWS)
    assert tb % CONV_ROWS == 0 and tb >= HIST_ROWS
    n_blk = seq_len // tb
    vec = pl.BlockSpec((1, cd), lambda s, t: (0, 0))
    return pl.pallas_call(
        functools.partial(_conv_prompt_kernel, width=width),
        grid=(n_seq, n_blk),
        in_specs=[pl.BlockSpec((tb, cd), lambda s, t: (s * n_blk + t, 0)),
                  pl.BlockSpec((width, cd), lambda s, t: (0, 0)), vec, vec, vec],
        out_specs=pl.BlockSpec((tb, cd), lambda s, t: (s * n_blk + t, 0)),
        out_shape=jax.ShapeDtypeStruct((n_seq * seq_len, cd), BF16),
        scratch_shapes=[pltpu.VMEM((HIST_ROWS + tb, cd), F32), pltpu.VMEM((tb, cd), F32)],
        compiler_params=_params(("parallel", "arbitrary"), 32),
        name="conv_prompt",
    )(ug, w, wb.reshape(1, cd), g.reshape(1, cd), b.reshape(1, cd))


def _conv_sample_kernel(u_ref, buf_ref, w_ref, wb_ref, g_ref, b_ref, c_ref, nbuf_ref, ext_ref, conv_ref,
                        *, width, seq_len):
    n_seq, hist, cd = buf_ref.shape
    pad_rows = ext_ref.shape[0] - hist - seq_len
    for s in range(n_seq):
        ext_ref[0:hist, :] = buf_ref[s]
        ext_ref[hist:hist + seq_len, :] = u_ref[s * seq_len:(s + 1) * seq_len, :]
        ext_ref[hist + seq_len:, :] = jnp.zeros((pad_rows, cd), F32)
        for c0 in range(0, cd, CONV_LANES):
            lanes = slice(c0, c0 + CONV_LANES)
            acc = _conv_strip(ext_ref, w_ref, 0, SUBLANES, lanes, width)
            conv_ref[s * seq_len:(s + 1) * seq_len, lanes] = acc[0:seq_len, :]
        nbuf_ref[s] = ext_ref[seq_len:seq_len + hist, :]
    c_ref[...] = _ln_silu(conv_ref[...] + wb_ref[...], g_ref[...], b_ref[...]).astype(c_ref.dtype)


def _conv_sample(ug, buf, w, wb, g, b, *, row0, seq_len):
    width, cd = w.shape
    n_seq, hist, _ = buf.shape
    assert hist == width - 1 and seq_len <= SUBLANES and cd % CONV_LANES == 0
    sb = _pick(n_seq, 8, 1)
    rows = sb * seq_len
    assert rows % 16 == 0 and row0 % rows == 0
    blk0 = row0 // rows
    ext_rows = -(-(hist + SUBLANES) // SUBLANES) * SUBLANES
    vec = pl.BlockSpec((1, cd), lambda i: (0, 0))
    return pl.pallas_call(
        functools.partial(_conv_sample_kernel, width=width, seq_len=seq_len),
        grid=(n_seq // sb,),
        in_specs=[pl.BlockSpec((rows, cd), lambda i: (blk0 + i, 0)),
                  pl.BlockSpec((sb, hist, cd), lambda i: (i, 0, 0)),
                  pl.BlockSpec((width, cd), lambda i: (0, 0)), vec, vec, vec],
        out_specs=[pl.BlockSpec((rows, cd), lambda i: (i, 0)),
                   pl.BlockSpec((sb, hist, cd), lambda i: (i, 0, 0))],
        out_shape=[jax.ShapeDtypeStruct((n_seq * seq_len, cd), BF16),
                   jax.ShapeDtypeStruct((n_seq, hist, cd), F32)],
        scratch_shapes=[pltpu.VMEM((ext_rows, cd), F32), pltpu.VMEM((rows, cd), F32)],
        compiler_params=_params(("parallel",), 32),
        name="conv_sample",
    )(ug, buf, w, wb.reshape(1, cd), g.reshape(1, cd), b.reshape(1, cd))


def _attn_kernel(q_ref, k_ref, v_ref, o_ref, *, heads, seq_rows, scale):
    rows, ca = q_ref.shape
    nseq = rows // seq_rows
    n_mem = k_ref.shape[0] // nseq
    hd = ca // heads
    q = q_ref[...]
    if nseq > 1:
        shift = seq_rows.bit_length() - 1
        row_seq = lax.broadcasted_iota(jnp.int32, (rows, 1), 0) >> shift
    for h in range(heads):
        hs = slice(h * hd, (h + 1) * hd)
        out = None
        for s in range(nseq):
            ms = slice(s * n_mem, (s + 1) * n_mem)
            sc = _dot_nt(q[:, hs], k_ref[ms, hs].astype(BF16)) * scale
            e = jnp.exp(sc - jnp.max(sc, axis=-1, keepdims=True))
            p = e / jnp.sum(e, axis=-1, keepdims=True)
            o = _dot(p.astype(BF16), v_ref[ms, hs].astype(BF16))
            out = o if out is None else jnp.where(row_seq == s, o, out)
        o_ref[:, hs] = out.astype(o_ref.dtype)


def _attn(q, k, v, *, row0, n_seq, seq_len, n_mem, heads, name):
    ca = q.shape[1]
    if seq_len % SUBLANES == 0:
        rows, nseq = _pick(seq_len, 256, 16), 1
    else:
        assert SUBLANES % seq_len == 0
        rows, nseq = 16, 16 // seq_len
    assert row0 % rows == 0 and (n_seq * seq_len) % rows == 0
    blk0 = row0 // rows
    per_seq = max(seq_len // rows, 1)
    kv_spec = pl.BlockSpec((nseq * n_mem, ca), lambda i: (i // per_seq, 0))
    return pl.pallas_call(
        functools.partial(_attn_kernel, heads=heads, seq_rows=seq_len if nseq > 1 else rows,
                          scale=(ca // heads) ** -0.5),
        grid=(n_seq * seq_len // rows,),
        in_specs=[pl.BlockSpec((rows, ca), lambda i: (blk0 + i, 0)), kv_spec, kv_spec],
        out_specs=pl.BlockSpec((rows, ca), lambda i: (i, 0)),
        out_shape=jax.ShapeDtypeStruct((n_seq * seq_len, ca), BF16),
        compiler_params=_params(("parallel",), 32),
        name=name,
    )(q, k, v)


ROUTE_E0, ROUTE_E1, ROUTE_G0, ROUTE_G1 = 0, 1, 2, 3
GATHER_UNROLL = 8
HIGH_HALF = 0xFFFF0000


def _token_stride(chunks):
    return chunks + SUBLANES


def _chunk(c, t0, n, chunks):
    stride = _token_stride(chunks)
    return pl.ds(t0 * stride + c, n, stride=stride), slice(None)


def _token_copies(hbm_ref, hbm_token, buf_ref, t0, n_tokens, chunks, sem, to_hbm):
    stride = _token_stride(chunks)

    def pair(hbm_rows, buf_rows):
        return (buf_rows, hbm_rows) if to_hbm else (hbm_rows, buf_rows)

    def start():
        def body(j, carry):
            for k in range(GATHER_UNROLL):
                r = j * GATHER_UNROLL + k
                src, dst = pair(hbm_ref.at[pl.ds(hbm_token(r) * chunks, chunks), :],
                                buf_ref.at[pl.ds((t0 + r) * stride, chunks), :])
                pltpu.make_async_copy(src, dst, sem).start()
            return carry

        lax.fori_loop(0, n_tokens // GATHER_UNROLL, body, 0)

    def wait():
        src, dst = pair(hbm_ref.at[pl.ds(0, n_tokens * chunks), :],
                        buf_ref.at[pl.ds(t0 * stride, n_tokens * chunks), :])
        pltpu.make_async_copy(src, dst, sem).wait()

    return start, wait


def _router_kernel(x_ref, g_ref, w_ref, b_ref, o_ref, htok_hbm, tokbuf, sem, *, n_groups, per_group):
    tm, d = x_ref.shape
    words = d // (2 * LANES)
    h = _rmsnorm_rows(x_ref[...], g_ref[...])
    bits = pltpu.bitcast(h.astype(BF16).astype(F32), jnp.uint32)
    for c in range(words):
        low = bits[:, 2 * c * LANES:(2 * c + 1) * LANES]
        high = bits[:, (2 * c + 1) * LANES:(2 * c + 2) * LANES]
        tokbuf[_chunk(c, 0, tm, words)] = (low >> 16) | (high & jnp.uint32(HIGH_HALF))
    row0 = pl.program_id(0) * tm
    start_tok_copy, wait_tok_copy = _token_copies(htok_hbm, lambda r: row0 + r, tokbuf, 0, tm, words, sem,
                                                  to_hbm=True)
    start_tok_copy()
    h_hi, h_lo = _split_bf16(h, 2)

    w_hi, w_lo = _split_bf16(w_ref[...], 2)
    logits = _dot(h_hi, w_hi) + (_dot(h_hi, w_lo) + _dot(h_lo, w_hi)) + b_ref[...]
    n_experts = n_groups * per_group
    lane_i = lax.broadcasted_iota(jnp.int32, logits.shape, 1)
    lane = lane_i.astype(F32)
    neg = jnp.float32(-jnp.inf)

    def first_argmax(x, m):
        return jnp.min(jnp.where(x == m, lane, float(LANES)), axis=-1, keepdims=True)

    lg = jnp.where(jnp.logical_and(lane_i >= n_experts, lane_i < n_experts + n_groups), logits, neg)
    mg = jnp.max(lg, axis=-1, keepdims=True)
    pg_top = 1.0 / jnp.sum(jnp.exp(lg - mg), axis=-1, keepdims=True)
    gsel = first_argmax(lg, mg) - float(n_experts)
    shift = per_group.bit_length() - 1
    in_group = jnp.logical_and(lane_i < n_experts, (lane_i >> shift).astype(F32) == gsel)
    le = jnp.where(in_group, logits, neg)
    m0 = jnp.max(le, axis=-1, keepdims=True)
    z = jnp.sum(jnp.exp(le - m0), axis=-1, keepdims=True)
    e0 = first_argmax(le, m0)
    le1 = jnp.where(lane == e0, neg, le)
    m1 = jnp.max(le1, axis=-1, keepdims=True)
    e1 = first_argmax(le1, m1)
    p0 = 1.0 / z
    p1 = jnp.exp(m1 - m0) / z
    den = p0 + p1
    g0 = pg_top * (p0 / den)
    g1 = pg_top * (p1 / den)
    rec = jnp.where(lane_i == ROUTE_E0, e0, 0.0)
    rec = jnp.where(lane_i == ROUTE_E1, e1, rec)
    rec = jnp.where(lane_i == ROUTE_G0, g0, rec)
    rec = jnp.where(lane_i == ROUTE_G1, g1, rec)
    o_ref[...] = rec
    wait_tok_copy()


def _router(x, g, w_group, b_group, w_expert, b_expert):
    t, d = x.shape
    n_groups, n_experts = w_group.shape[1], w_expert.shape[1]
    per_group = n_experts // n_groups
    assert n_experts + n_groups <= LANES and per_group & (per_group - 1) == 0
    pad = lambda a: jnp.pad(a, ((0, 0), (0, LANES - a.shape[1])))
    w_both = pad(jnp.concatenate([w_expert, w_group], axis=1))
    b_both = pad(jnp.concatenate([b_expert, b_group]).reshape(1, -1))
    words = d // (2 * LANES)
    tm = _pick(t, 256, GATHER_UNROLL)
    return pl.pallas_call(
        functools.partial(_router_kernel, n_groups=n_groups, per_group=per_group),
        grid=(t // tm,),
        in_specs=[pl.BlockSpec((tm, d), lambda i: (i, 0)), pl.BlockSpec((1, d), lambda i: (0, 0)),
                  pl.BlockSpec((d, LANES), lambda i: (0, 0)), pl.BlockSpec((1, LANES), lambda i: (0, 0))],
        out_specs=[pl.BlockSpec((tm, LANES), lambda i: (i, 0)), pl.BlockSpec(memory_space=pl.ANY)],
        out_shape=[jax.ShapeDtypeStruct((t, LANES), F32), jax.ShapeDtypeStruct((t * words, LANES), jnp.uint32)],
        scratch_shapes=[pltpu.VMEM((tm * _token_stride(words), LANES), jnp.uint32), pltpu.SemaphoreType.DMA(())],
        compiler_params=_params(("parallel",), 32),
        name="router",
    )(x, g.reshape(1, d), w_both, b_both)


def _dispatch_plan(e_ids, n_experts, tile):
    t, k = e_ids.shape
    n_pairs = t * k
    n_tiles = n_pairs // tile + n_experts
    flat = e_ids.reshape(n_pairs)
    onehot = (flat[:, None] == jnp.arange(n_experts, dtype=jnp.int32)[None, :]).astype(jnp.int32)
    counts = jnp.sum(onehot, axis=0)
    tiles_per = (counts + tile - 1) // tile
    tile_end = jnp.cumsum(tiles_per)
    tile_start = tile_end - tiles_per
    rank = jnp.take_along_axis(jnp.cumsum(onehot, axis=0), flat[:, None], axis=1)[:, 0] - 1
    pos = tile_start[flat] * tile + rank
    token = jnp.arange(n_pairs, dtype=jnp.int32) // k
    src = jnp.zeros((n_tiles * tile,), jnp.int32).at[pos].set(token)
    n_used = tile_end[-1]
    tile_ids = jnp.minimum(jnp.arange(n_tiles, dtype=jnp.int32), n_used - 1)
    tile_expert = jnp.searchsorted(tile_end, tile_ids, side="right").astype(jnp.int32)
    after = tile_end[tile_expert]
    next_expert = jnp.where(after < n_used, tile_expert[jnp.minimum(after, n_tiles - 1)], -1).astype(jnp.int32)
    return (pos.reshape(t, k), src.reshape(n_tiles, 1, tile), tile_expert, next_expert,
            n_used.reshape(1).astype(jnp.int32))


def _cast_rows(src_ref, dst_ref, rows_per):
    def body(c, carry):
        rows = pl.ds(pl.multiple_of(c * rows_per, rows_per), rows_per)
        dst_ref[rows, :] = src_ref[rows, :].astype(dst_ref.dtype)
        return carry

    lax.fori_loop(0, src_ref.shape[0] // rows_per, body, 0)


DOWN_COLS = 1024


def _experts_kernel(texp_ref, nexp_ref, nused_ref, idx_ref, idx_next_ref, h_hbm, wg_hbm, wu_hbm, wd_hbm,
                    y_hbm, xbuf, ybuf, hbuf, stage_g, stage_u, stage_d, wg_bf, wu_bf, wd_bf, gsem, ysem, wsem):
    i = pl.program_id(0)
    n_used = nused_ref[0]
    tile, d = hbuf.shape
    chunks = d // LANES
    words = chunks // 2
    slot = lax.rem(i, 2)
    expert = texp_ref[i]
    start_tokens, wait_tokens = _token_copies(h_hbm, lambda r: idx_ref[0, 0, r], xbuf, slot * tile, tile, words,
                                              gsem.at[slot], to_hbm=False)
    start_next_tokens, _ = _token_copies(h_hbm, lambda r: idx_next_ref[0, 0, r], xbuf, (1 - slot) * tile, tile,
                                         words, gsem.at[1 - slot], to_hbm=False)

    def weight_copies(e):
        return (pltpu.make_async_copy(wg_hbm.at[e], stage_g, wsem.at[0]),
                pltpu.make_async_copy(wu_hbm.at[e], stage_u, wsem.at[1]),
                pltpu.make_async_copy(wd_hbm.at[e], stage_d, wsem.at[2]))

    def y_copy(step):
        return _token_copies(y_hbm, lambda r: step * tile + r, ybuf, 0, tile, chunks, ysem, to_hbm=True)

    @pl.when(i < n_used)
    def _():
        @pl.when(i == 0)
        def _():
            start_tokens()
            for cp in weight_copies(expert):
                cp.start()

        @pl.when(i + 1 < n_used)
        def _():
            start_next_tokens()

        @pl.when(jnp.logical_or(i == 0, expert != texp_ref[jnp.maximum(i - 1, 0)]))
        def _():
            for cp in weight_copies(expert):
                cp.wait()
            _cast_rows(stage_g, wg_bf, 512)
            _cast_rows(stage_u, wu_bf, 512)
            _cast_rows(stage_d, wd_bf, 64)
            nxt = nexp_ref[i]

            @pl.when(nxt >= 0)
            def _():
                for cp in weight_copies(nxt):
                    cp.start()

        wait_tokens()
        for c in range(words):
            word = xbuf[_chunk(c, slot * tile, tile, words)]
            low = pltpu.bitcast(word << 16, F32)
            high = pltpu.bitcast(word & jnp.uint32(HIGH_HALF), F32)
            hbuf[:, 2 * c * LANES:(2 * c + 1) * LANES] = low.astype(BF16)
            hbuf[:, (2 * c + 1) * LANES:(2 * c + 2) * LANES] = high.astype(BF16)
        h = hbuf[...]
        a = _dot(h, wg_bf[...])
        u = _dot(h, wu_bf[...])
        hid = (_silu(a) * u).astype(BF16)

        @pl.when(i > 0)
        def _():
            y_copy(i - 1)[1]()

        for c0 in range(0, d, DOWN_COLS):
            y = _dot(hid, wd_bf[:, c0:c0 + DOWN_COLS])
            for c in range(DOWN_COLS // LANES):
                ybuf[_chunk(c0 // LANES + c, 0, tile, chunks)] = y[:, c * LANES:(c + 1) * LANES]

    @pl.when(i >= n_used)
    def _():
        y_copy(i - 1)[1]()

        @pl.when(i == n_used)
        def _():
            ybuf[...] = jnp.zeros(ybuf.shape, F32)

    y_copy(i)[0]()

    @pl.when(i == pl.num_programs(0) - 1)
    def _():
        y_copy(i)[1]()


def _experts(h_tok, src, tile_expert, next_expert, n_used, wg, wu, wd):
    d = wg.shape[1]
    chunks = d // LANES
    words = chunks // 2
    n_tiles, _, tile = src.shape
    de = wg.shape[2]
    assert d % 512 == 0 and de % 64 == 0 and d % DOWN_COLS == 0 and tile % GATHER_UNROLL == 0
    idx_spec = lambda ahead: pl.BlockSpec(
        (1, 1, tile), lambda i, texp, nexp, nused: (jnp.minimum(i + ahead, nused[0] - 1), 0, 0),
        memory_space=pltpu.SMEM)
    hbm = pl.BlockSpec(memory_space=pl.ANY)
    grid_spec = pltpu.PrefetchScalarGridSpec(
        num_scalar_prefetch=3,
        grid=(n_tiles,),
        in_specs=[idx_spec(0), idx_spec(1), hbm, hbm, hbm, hbm],
        out_specs=hbm,
        scratch_shapes=[pltpu.VMEM((2 * tile * _token_stride(words), LANES), jnp.uint32),
                        pltpu.VMEM((tile * _token_stride(chunks), LANES), F32), pltpu.VMEM((tile, d), BF16),
                        pltpu.VMEM((d, de), F32), pltpu.VMEM((d, de), F32), pltpu.VMEM((de, d), F32),
                        pltpu.VMEM((d, de), BF16), pltpu.VMEM((d, de), BF16), pltpu.VMEM((de, d), BF16),
                        pltpu.SemaphoreType.DMA((2,)), pltpu.SemaphoreType.DMA(()), pltpu.SemaphoreType.DMA((3,))],
    )
    weights_bytes = 3 * d * de * (4 + 2)
    tiles_bytes = tile * (2 * _token_stride(words) + _token_stride(chunks)) * LANES * 4
    tiles_bytes += tile * d * 2 + tile * DOWN_COLS * 4 + 3 * tile * de * 4
    return pl.pallas_call(
        _experts_kernel,
        grid_spec=grid_spec,
        out_shape=jax.ShapeDtypeStruct((n_tiles * tile * chunks, LANES), F32),
        compiler_params=_params(("arbitrary",), (weights_bytes + tiles_bytes) // MIB + 3),
        name="experts",
    )(tile_expert, next_expert, n_used, src, src, h_tok, wg, wu, wd)


def _combine_kernel(idx0_ref, idx1_ref, idx0_next_ref, idx1_next_ref, y_hbm, x_ref, route_ref, gf_ref, *refs,
                    n_first, final_norm):
    outs, (buf0, buf1, res_ref, sem) = refs[:-4], refs[-4:]
    i = pl.program_id(0)
    tm, d = x_ref.shape
    chunks = d // LANES
    slot = lax.rem(i, 2)
    def gather(idx_ref, buf, k, into_slot):
        return _token_copies(y_hbm, lambda r: idx_ref[0, 0, r], buf, into_slot * tm, tm, chunks,
                             sem.at[k, into_slot], to_hbm=False)

    gathers = [gather(idx0_ref, buf0, 0, slot), gather(idx1_ref, buf1, 1, slot)]
    gathers_next = [gather(idx0_next_ref, buf0, 0, 1 - slot), gather(idx1_next_ref, buf1, 1, 1 - slot)]

    @pl.when(i == 0)
    def _():
        for start, _ in gathers:
            start()

    @pl.when(i + 1 < pl.num_programs(0))
    def _():
        for start, _ in gathers_next:
            start()

    for _, wait in gathers:
        wait()
    g0 = route_ref[:, ROUTE_G0:ROUTE_G0 + 1]
    g1 = route_ref[:, ROUTE_G1:ROUTE_G1 + 1]
    ssq = jnp.zeros((tm, LANES), F32)
    for c in range(chunks):
        lanes = slice(c * LANES, (c + 1) * LANES)
        y0 = buf0[_chunk(c, slot * tm, tm, chunks)]
        y1 = buf1[_chunk(c, slot * tm, tm, chunks)]
        r = x_ref[:, lanes] + (g0 * y0 + g1 * y1)
        res_ref[:, lanes] = r
        ssq = ssq + r * r
    if final_norm:
        scale = lax.rsqrt(jnp.sum(ssq, axis=-1, keepdims=True) * (1.0 / d) + EPS)
        result = lambda: res_ref[...] * scale * gf_ref[...]
    else:
        result = lambda: res_ref[...]
    if len(outs) == 1:
        outs[0][...] = result()
    else:
        @pl.when(i < n_first)
        def _():
            outs[0][...] = result()

        @pl.when(i >= n_first)
        def _():
            outs[1][...] = result()


def _combine(x, y_sorted, pos, route, g_final, *, split_rows, final_norm):
    t, d = x.shape
    assert y_sorted.shape[1] == LANES
    tm = _pick(t if split_rows is None else math.gcd(split_rows, t - split_rows), 128, GATHER_UNROLL)
    n_blk = t // tm
    idx = [pos[:, k].reshape(n_blk, 1, tm) for k in range(TOP_K_IN_GROUP)]
    smem = pl.BlockSpec((1, 1, tm), lambda i: (i, 0, 0), memory_space=pltpu.SMEM)
    smem_next = pl.BlockSpec((1, 1, tm), lambda i: (jnp.minimum(i + 1, n_blk - 1), 0, 0), memory_space=pltpu.SMEM)
    row = pl.BlockSpec((tm, d), lambda i: (i, 0))
    if split_rows is None:
        n_first = n_blk
        out_specs = [row]
        out_shape = [jax.ShapeDtypeStruct((t, d), F32)]
    else:
        n_first = split_rows // tm
        out_specs = [pl.BlockSpec((tm, d), lambda i: (jnp.minimum(i, n_first - 1), 0)),
                     pl.BlockSpec((tm, d), lambda i: (jnp.maximum(i - n_first, 0), 0))]
        out_shape = [jax.ShapeDtypeStruct((split_rows, d), F32), jax.ShapeDtypeStruct((t - split_rows, d), F32)]
    return pl.pallas_call(
        functools.partial(_combine_kernel, n_first=n_first, final_norm=final_norm),
        grid=(n_blk,),
        in_specs=[smem, smem, smem_next, smem_next, pl.BlockSpec(memory_space=pl.ANY), row,
                  pl.BlockSpec((tm, LANES), lambda i: (i, 0)), pl.BlockSpec((1, d), lambda i: (0, 0))],
        out_specs=out_specs,
        out_shape=out_shape,
        scratch_shapes=[pltpu.VMEM((2 * tm * _token_stride(d // LANES), LANES), F32),
                        pltpu.VMEM((2 * tm * _token_stride(d // LANES), LANES), F32), pltpu.VMEM((tm, d), F32),
                        pltpu.SemaphoreType.DMA((TOP_K_IN_GROUP, 2))],
        compiler_params=_params(("arbitrary",), 32),
        name="combine",
    )(idx[0], idx[1], idx[0], idx[1], y_sorted, x, route, g_final.reshape(1, d))


def kernel(x_prompt, x_sample, mem_prompt, state_gla, state_conv, cache_mem_k, cache_mem_v, norm_mix_g, w_in, w_alpha_up, b_alpha, gla_norm_g, w_branch_a, conv_dw_w, conv_dw_b, conv_ln_g, conv_ln_b, w_branch_b, w_out, norm_ca_g, norm_mem_g, w_ca_q, w_ca_k, w_ca_v, w_ca_o, norm_ffn_g, w_router_group, b_router_group, w_router_expert, b_router_expert, w_exp_gate, w_exp_up, w_exp_down, norm_final_g):
    depth = w_in.shape[0]
    bp, tp, d = x_prompt.shape
    bs, ts, _ = x_sample.shape
    heads, dk, dv = state_gla.shape[2:]
    key, val = heads * dk, heads * dv
    rank = w_alpha_up.shape[1]
    cd = state_conv.shape[3]
    n_mem, ca_heads, ca_hd = cache_mem_k.shape[2:]
    ca = ca_heads * ca_hd
    n_experts = w_router_expert.shape[2]
    rows_p, rows_s = bp * tp, bs * ts
    off_a = 2 * key + 2 * val
    off_u = off_a + rank
    off_g = off_u + 2 * cd
    assert rank <= LANES and w_in.shape[2] == off_g + 2 * d
    bf = lambda a: a.astype(BF16)

    x = (x_prompt.reshape(rows_p, d), x_sample.reshape(rows_s, d))
    outs = dict(gla_p=[], conv_p=[], mk_p=[], mv_p=[], gla_s=[], conv_s=[])
    for l in range(depth):
        w_in_t = jnp.swapaxes(w_in, 1, 2)[l]
        w_qkvr = _cast_cols(w_in_t, 0, off_a)
        w_ug = _cast_cols(w_in_t, off_u, 2 * cd + 2 * d)
        w_a = bf(jnp.pad(w_in[l, :, off_a:off_u], ((0, 0), (0, LANES - rank))))
        h = _rmsnorm(x, norm_mix_g[l], BF16)
        z = _mm([h], [w_qkvr], [0], _ep_identity, F32, tm_pref=WIDE_ROWS, name="in_qkvr")
        a_low = _mm([h], [w_a], [0], _ep_identity, F32, name="in_alow")
        ug = _mm([h], [w_ug, w_ug], [0, 0], _ep_glu, F32, n=cd, w_col0=[0, cd], tm_pref=WIDE_ROWS,
                 name="in_glu")
        w_up = bf(jnp.pad(w_alpha_up[l], ((0, LANES - rank), (0, 0))))
        gla_args = dict(heads=heads, dk=dk, dv=dv)
        og_p, st_p = _gla(z, a_low, w_up, b_alpha[l], gla_norm_g[l], None, row0=0, n_seq=bp, seq_len=tp,
                          name="gla_prompt", **gla_args)
        og_s, st_s = _gla(z, a_low, w_up, b_alpha[l], gla_norm_g[l], state_gla[l], row0=rows_p, n_seq=bs,
                          seq_len=ts, name="gla_sample", **gla_args)
        conv_w = (conv_dw_w[l], conv_dw_b[l], conv_ln_g[l], conv_ln_b[l])
        c_p = _conv_prompt(ug, *conv_w, n_seq=bp, seq_len=tp)
        c_s, buf_s = _conv_sample(ug, state_conv[l], *conv_w, row0=rows_p, seq_len=ts)
        og = (og_p, og_s)
        c = (c_p, c_s)
        merged = _mm([og, c, h], [bf(w_branch_a[l]), bf(w_branch_b[l]), w_ug, w_ug], [0, 1, 2, 2],
                     _ep_gated_merge, BF16, n=d, w_col0=[0, 0, 2 * cd, 2 * cd + d], tn_pref=256, name="merge")
        x = _mm([merged], [bf(w_out[l])], [0], _ep_residual, F32, extras=[x], name="out_proj")
        h = _rmsnorm(x, norm_ca_g[l], BF16)
        q = _mm([h], [bf(w_ca_q[l])], [0], _ep_identity, BF16, tm_pref=WIDE_ROWS, name="ca_q")
        m = _rmsnorm(mem_prompt.reshape(bp * n_mem, d), norm_mem_g[l], BF16)
        mk = _mm([m], [bf(w_ca_k[l])], [0], _ep_identity, F32, name="mem_k")
        mv = _mm([m], [bf(w_ca_v[l])], [0], _ep_identity, F32, name="mem_v")
        ao_p = _attn(q, mk, mv, row0=0, n_seq=bp, seq_len=tp, n_mem=n_mem, heads=ca_heads, name="attn_prompt")
        ao_s = _attn(q, cache_mem_k[l].reshape(bs * n_mem, ca), cache_mem_v[l].reshape(bs * n_mem, ca),
                     row0=rows_p, n_seq=bs, seq_len=ts, n_mem=n_mem, heads=ca_heads, name="attn_sample")
        ao = (ao_p, ao_s)
        x = _mm([ao], [bf(w_ca_o[l])], [0], _ep_residual, F32, extras=[x], name="ca_out")
        route, h_tok = _router(x, norm_ffn_g[l], w_router_group[l], b_router_group[l], w_router_expert[l],
                               b_router_expert[l])
        e_ids = route[:, ROUTE_E0:ROUTE_E1 + 1].astype(jnp.int32)
        tile = _pick(rows_p + rows_s, 256, SUBLANES)
        pos, src, tile_expert, next_expert, n_used = _dispatch_plan(e_ids, n_experts, tile)
        y_sorted = _experts(h_tok, src, tile_expert, next_expert, n_used, w_exp_gate[l], w_exp_up[l],
                            w_exp_down[l])
        last = l == depth - 1
        res = _combine(x, y_sorted, pos, route, norm_final_g, split_rows=rows_p if last else None,
                       final_norm=last)
        if not last:
            x = res[0]
        outs["gla_p"].append(st_p)
        hist = conv_dw_w.shape[1] - 1
        outs["conv_p"].append(jnp.stack([ug[(b + 1) * tp - hist:(b + 1) * tp] for b in range(bp)]))
        outs["mk_p"].append(mk.reshape(bp, n_mem, ca_heads, ca_hd))
        outs["mv_p"].append(mv.reshape(bp, n_mem, ca_heads, ca_hd))
        outs["gla_s"].append(st_s)
        outs["conv_s"].append(buf_s)
    y_prompt = res[0].reshape(bp, tp, d)
    y_sample = res[1].reshape(bs, ts, d)
    return (y_prompt, y_sample, jnp.stack(outs["gla_p"]), jnp.stack(outs["conv_p"]), jnp.stack(outs["mk_p"]),
            jnp.stack(outs["mv_p"]), jnp.stack(outs["gla_s"]), jnp.stack(outs["conv_s"]))
```

```python
import functools
import math

import jax
import jax.numpy as jnp
from jax import lax
from jax.experimental import pallas as pl
from jax.experimental.pallas import tpu as pltpu

F32 = jnp.float32
BF16 = jnp.bfloat16

EPS = 1e-6
GLA_TAU = 16.0
GLA_CHUNK = 64
TOP_K_IN_GROUP = 2
LANES = 128
SUBLANES = 8
MIB = 1 << 20


def _pick(n, pref, mult):
    for d in range(min(pref, n), 0, -1):
        if n % d == 0 and d % mult == 0:
            return d
    return n


def _params(semantics, vmem_mib):
    return pltpu.CompilerParams(dimension_semantics=semantics, vmem_limit_bytes=vmem_mib * MIB)


def _sigmoid(x):
    return 1.0 / (1.0 + jnp.exp(-x))


def _silu(x):
    return x * _sigmoid(x)


def _dot(a, b):
    return jnp.dot(a, b, preferred_element_type=F32)


def _dot_nt(a, b):
    return lax.dot_general(a, b, (((1,), (1,)), ((), ())), preferred_element_type=F32)


def _dot_tn(a, b):
    return lax.dot_general(a, b, (((0,), (0,)), ((), ())), preferred_element_type=F32)


def _split_bf16(x, parts):
    out = []
    for _ in range(parts - 1):
        p = x.astype(BF16)
        out.append(p)
        x = x - p.astype(F32)
    out.append(x.astype(BF16))
    return out


def _rmsnorm_rows(x, g):
    return x * lax.rsqrt(jnp.mean(x * x, axis=-1, keepdims=True) + EPS) * g


def _parts(a):
    return a if isinstance(a, tuple) else (a,)


def _rows_of(a):
    return sum(p.shape[0] for p in _parts(a))


def _part_specs(a, tm, cols, col_of):
    parts = _parts(a)
    if len(parts) == 1:
        return [pl.BlockSpec((tm, cols), lambda i, *j: (i, col_of(*j)))]
    n_first = parts[0].shape[0] // tm
    assert parts[0].shape[0] % tm == 0 and parts[1].shape[0] % tm == 0
    return [pl.BlockSpec((tm, cols), lambda i, *j: (jnp.minimum(i, n_first - 1), col_of(*j))),
            pl.BlockSpec((tm, cols), lambda i, *j: (jnp.maximum(i - n_first, 0), col_of(*j)))]


def _part_tile(a, pref, mult):
    return _pick(math.gcd(*[p.shape[0] for p in _parts(a)]), pref, mult)


def _select_part(refs, n_first):
    if len(refs) == 1:
        return refs[0][...]
    return jnp.where(pl.program_id(0) < n_first, refs[0][...], refs[1][...])


def _rmsnorm_kernel(*refs, n_first):
    x_refs, (g_ref, o_ref) = refs[:-2], refs[-2:]
    o_ref[...] = _rmsnorm_rows(_select_part(x_refs, n_first), g_ref[...]).astype(o_ref.dtype)


def _rmsnorm(x, g, out_dtype):
    t, d = _rows_of(x), _parts(x)[0].shape[1]
    tm = _part_tile(x, 128, 16)
    return pl.pallas_call(
        functools.partial(_rmsnorm_kernel, n_first=_parts(x)[0].shape[0] // tm),
        grid=(t // tm,),
        in_specs=_part_specs(x, tm, d, lambda: 0) + [pl.BlockSpec((1, d), lambda i: (0, 0))],
        out_specs=pl.BlockSpec((tm, d), lambda i: (i, 0)),
        out_shape=jax.ShapeDtypeStruct((t, d), out_dtype),
        compiler_params=_params(("parallel",), 32),
        name="rmsnorm",
    )(*_parts(x), g.reshape(1, d))


def _cast_weight_kernel(w_hbm, o_ref, buf, sem, *, row0):
    j = pl.program_id(0)
    tn = o_ref.shape[0]
    slot = lax.rem(j, 2)

    def copy(block, into):
        rows = pl.ds(pl.multiple_of(row0 + block * tn, SUBLANES), tn)
        return pltpu.make_async_copy(w_hbm.at[rows, :], buf.at[into], sem.at[into])

    @pl.when(j == 0)
    def _():
        copy(j, slot).start()

    @pl.when(j + 1 < pl.num_programs(0))
    def _():
        copy(j + 1, 1 - slot).start()

    copy(j, slot).wait()
    o_ref[...] = buf[slot].astype(BF16)


def _cast_weight(w, row0, n_rows):
    _, k = w.shape
    tn = _pick(n_rows, 256, 16)
    assert row0 % SUBLANES == 0 and n_rows % tn == 0
    return pl.pallas_call(
        functools.partial(_cast_weight_kernel, row0=row0),
        grid=(n_rows // tn,),
        in_specs=[pl.BlockSpec(memory_space=pl.ANY)],
        out_specs=pl.BlockSpec((tn, k), lambda j: (j, 0)),
        out_shape=jax.ShapeDtypeStruct((n_rows, k), BF16),
        scratch_shapes=[pltpu.VMEM((2, tn, k), F32), pltpu.SemaphoreType.DMA((2,))],
        compiler_params=_params(("arbitrary",), 32),
        name="cast_weight",
    )(w)


def _ep_identity(accs, extras):
    return accs[0]


def _ep_glu(accs, extras):
    return accs[0] * _sigmoid(accs[1])


def _ep_residual(accs, extras):
    return extras[0] + accs[0]


def _ep_gated_merge(accs, extras):
    return _sigmoid(accs[0]) * accs[2] + _sigmoid(accs[1]) * accs[3]


WIDE_ROWS = 1088


def _mm_kernel(*refs, x_of_w, w_rows, x_parts, e_parts, n_first, has_norm, epilogue):
    refs = list(refs)
    take = lambda k: [refs.pop(0) for _ in range(k)]
    x_refs = [take(p) for p in x_parts]
    g_ref = refs.pop(0) if has_norm else None
    w_refs = take(len(x_of_w))
    e_refs = [take(p) for p in e_parts]
    o_ref = refs.pop(0)
    i = pl.program_id(0)
    first_col = pl.program_id(1) == 0
    xs = []
    for k, parts in enumerate(x_refs):
        norm = has_norm and k == 0
        if len(parts) == 1 and not norm:
            xs.append(parts[0])
            continue
        scratch = refs.pop(0)

        def fill(src, scratch=scratch, norm=norm):
            val = src[...]
            if norm:
                val = _rmsnorm_rows(val, g_ref[...])
            scratch[...] = val.astype(scratch.dtype)

        if len(parts) == 1:
            pl.when(first_col)(functools.partial(fill, parts[0]))
        else:
            pl.when(jnp.logical_and(first_col, i < n_first))(functools.partial(fill, parts[0]))
            pl.when(jnp.logical_and(first_col, i >= n_first))(functools.partial(fill, parts[1]))
        xs.append(scratch)
    accs = [(_dot_nt if w_rows[k] else _dot)(xs[x_of_w[k]][...], w_refs[k][...]) for k in range(len(x_of_w))]
    o_ref[...] = epilogue(accs, [_select_part(e, n_first) for e in e_refs]).astype(o_ref.dtype)


def _mm(xs, ws, x_of_w, epilogue, out_dtype, extras=(), tm_pref=512, tn_pref=512, name="mm", n=None, w_col0=None,
        w_rows=None, norm_gain=None):
    t = _rows_of(xs[0])
    w_rows = [False] * len(ws) if w_rows is None else w_rows
    w_col0 = [0] * len(ws) if w_col0 is None else w_col0
    n = ws[0].shape[0 if w_rows[0] else 1] if n is None else n
    tm = min(_part_tile(a, tm_pref, 16) for a in list(xs) + list(extras))
    tn = _pick(math.gcd(n, *w_col0), tn_pref, LANES)
    split = [a for a in list(xs) + list(extras) if len(_parts(a)) == 2]
    assert all(a[0].shape[0] == split[0][0].shape[0] and a[0].shape[0] % tm == 0 for a in split)
    n_first = split[0][0].shape[0] // tm if split else 0
    k_of = lambda x: _parts(x)[0].shape[1]
    in_specs, args = [], []
    for x in xs:
        in_specs += _part_specs(x, tm, k_of(x), lambda j: 0)
        args += _parts(x)
    if norm_gain is not None:
        in_specs.append(pl.BlockSpec((1, k_of(xs[0])), lambda i, j: (0, 0)))
        args.append(norm_gain.reshape(1, -1))
    for w, c0, rows in zip(ws, w_col0, w_rows):
        if rows:
            in_specs.append(pl.BlockSpec((tn, w.shape[1]), lambda i, j, b0=c0 // tn: (b0 + j, 0)))
        else:
            in_specs.append(pl.BlockSpec((w.shape[0], tn), lambda i, j, b0=c0 // tn: (0, b0 + j)))
    args += ws
    for e in extras:
        in_specs += _part_specs(e, tm, tn, lambda j: j)
        args += _parts(e)
    staged = [x for k, x in enumerate(xs) if len(_parts(x)) == 2 or (norm_gain is not None and k == 0)]
    scratch = [pltpu.VMEM((tm, k_of(x)), BF16) for x in staged]
    block_bytes = sum(tm * p.shape[1] * p.dtype.itemsize for x in xs for p in _parts(x))
    block_bytes += sum(w.size // (w.shape[0 if rows else 1] // tn) * w.dtype.itemsize for w, rows in zip(ws, w_rows))
    block_bytes += sum(tm * tn * p.dtype.itemsize for e in extras for p in _parts(e))
    block_bytes += tm * tn * jnp.dtype(out_dtype).itemsize
    scratch_bytes = sum(tm * k_of(x) * 2 for x in staged)
    norm_bytes = 2 * tm * k_of(xs[0]) * 4 if norm_gain is not None else 0
    acc_bytes = len(ws) * tm * tn * 4
    vmem = (2 * block_bytes + 2 * acc_bytes + scratch_bytes + norm_bytes) // MIB + 4
    return pl.pallas_call(
        functools.partial(_mm_kernel, x_of_w=tuple(x_of_w), w_rows=tuple(w_rows),
                          x_parts=tuple(len(_parts(x)) for x in xs), e_parts=tuple(len(_parts(e)) for e in extras),
                          n_first=n_first, has_norm=norm_gain is not None, epilogue=epilogue),
        grid=(t // tm, n // tn),
        in_specs=in_specs,
        out_specs=pl.BlockSpec((tm, tn), lambda i, j: (i, j)),
        out_shape=jax.ShapeDtypeStruct((t, n), out_dtype),
        scratch_shapes=scratch,
        compiler_params=_params(("parallel", "arbitrary"), vmem),
        name=name,
    )(*args)


def _log_sigmoid(x):
    return jnp.minimum(x, 0.0) - jnp.log1p(jnp.exp(-jnp.abs(x)))


def _gla_kernel(*refs, heads, seq_rows, has_h0, scale):
    if has_h0:
        q_ref, k_ref, v_ref, r_ref, a_ref, wup_ref, bal_ref, gn_ref, h0_ref, og_ref, st_ref = refs
    else:
        q_ref, k_ref, v_ref, r_ref, a_ref, wup_ref, bal_ref, gn_ref, og_ref, st_ref = refs
    rows, key = q_ref.shape
    val = v_ref.shape[1]
    dk, dv = key // heads, val // heads
    nseq = rows // seq_rows

    @pl.when(pl.program_id(1) == 0)
    def _():
        if has_h0:
            st_ref[...] = h0_ref[...]
        else:
            st_ref[...] = jnp.zeros(st_ref.shape, F32)

    la = _dot(a_ref[...].astype(BF16), wup_ref[...]) + bal_ref[...]
    log_a = _log_sigmoid(la) * (1.0 / GLA_TAU)

    ri = lax.broadcasted_iota(jnp.int32, (rows, rows), 0)
    ci = lax.broadcasted_iota(jnp.int32, (rows, rows), 1)
    causal = ci <= ri
    if nseq > 1:
        shift = seq_rows.bit_length() - 1
        causal = jnp.logical_and(causal, (ri >> shift) == (ci >> shift))
        row_seq = lax.broadcasted_iota(jnp.int32, (rows, 1), 0) >> shift
    tri = jnp.where(causal, 1.0, 0.0).astype(BF16)
    b = sum(_dot(tri, part) for part in _split_bf16(log_a, 3))

    q = q_ref[...] * scale
    k = k_ref[...]
    v = v_ref[...]
    r = r_ref[...]
    gn = gn_ref[...]
    for h in range(heads):
        ks = slice(h * dk, (h + 1) * dk)
        vs = slice(h * dv, (h + 1) * dv)
        bh = b[:, ks]
        q_e = (q[:, ks] * jnp.exp(bh)).astype(BF16)
        k_e = (k[:, ks] * jnp.exp(-bh)).astype(BF16)
        vh = v[:, vs].astype(BF16)
        scores = jnp.where(causal, _dot_nt(q_e, k_e), 0.0)
        o = _dot(scores.astype(BF16), vh)
        for s in range(nseq):
            b_last = bh[(s + 1) * seq_rows - 1:(s + 1) * seq_rows, :]
            k_s = k[:, ks] * jnp.exp(b_last - bh)
            if nseq > 1:
                in_seq = row_seq == s
                k_s = jnp.where(in_seq, k_s, 0.0)
            state = st_ref[s, h]
            o_inter = _dot(q_e, state.astype(BF16))
            o = o + (jnp.where(in_seq, o_inter, 0.0) if nseq > 1 else o_inter)
            d_state = _dot_tn(k_s.astype(BF16), vh)
            decay_col = jnp.broadcast_to(jnp.exp(b_last), (LANES, dk)).T[:, 0:1]
            st_ref[s, h] = decay_col * state + d_state
        mu = jnp.mean(o, axis=-1, keepdims=True)
        oc = o - mu
        o_n = oc * lax.rsqrt(jnp.mean(oc * oc, axis=-1, keepdims=True) + EPS) * gn[:, vs]
        og_ref[:, vs] = (o_n * _silu(r[:, vs])).astype(og_ref.dtype)


def _gla(z, a_low, w_up, b_alpha, gn, h0, *, row0, n_seq, seq_len, heads, dk, dv, name):
    key, val = heads * dk, heads * dv
    assert val == 2 * key and z.shape[1] == 2 * key + 2 * val
    chunk = min(GLA_CHUNK, seq_len)
    assert seq_len % chunk == 0
    if chunk % SUBLANES == 0:
        rows, n_chunks = chunk, seq_len // chunk
    else:
        assert SUBLANES % seq_len == 0 and chunk == seq_len
        rows, n_chunks = SUBLANES, 1
    nseq = rows // chunk
    assert n_seq % nseq == 0 and row0 % rows == 0
    blk0 = row0 // rows
    row_map = lambda g, c: blk0 + g * n_chunks + c
    in_specs = [
        pl.BlockSpec((rows, key), lambda g, c: (row_map(g, c), 0)),
        pl.BlockSpec((rows, key), lambda g, c: (row_map(g, c), 1)),
        pl.BlockSpec((rows, val), lambda g, c: (row_map(g, c), 1)),
        pl.BlockSpec((rows, val), lambda g, c: (row_map(g, c), 2)),
        pl.BlockSpec((rows, LANES), lambda g, c: (row_map(g, c), 0)),
        pl.BlockSpec((LANES, key), lambda g, c: (0, 0)),
        pl.BlockSpec((1, key), lambda g, c: (0, 0)),
        pl.BlockSpec((1, val), lambda g, c: (0, 0)),
    ]
    args = [z, z, z, z, a_low, w_up, b_alpha.reshape(1, key), gn.reshape(1, val)]
    state_spec = pl.BlockSpec((nseq, heads, dk, dv), lambda g, c: (g, 0, 0, 0))
    if h0 is not None:
        in_specs.append(state_spec)
        args.append(h0)
    n_rows = n_seq * seq_len
    return pl.pallas_call(
        functools.partial(_gla_kernel, heads=heads, seq_rows=chunk, has_h0=h0 is not None, scale=dk ** -0.5),
        grid=(n_seq // nseq, n_chunks),
        in_specs=in_specs,
        out_specs=[pl.BlockSpec((rows, val), lambda g, c: (g * n_chunks + c, 0)), state_spec],
        out_shape=[jax.ShapeDtypeStruct((n_rows, val), BF16),
                   jax.ShapeDtypeStruct((n_seq, heads, dk, dv), F32)],
        compiler_params=_params(("parallel", "arbitrary"), 48),
        name=name,
    )(*args)


CONV_LANES = 512
CONV_ROWS = 32
HIST_ROWS = 32


def _ln_silu(x, g, b):
    mu = jnp.mean(x, axis=-1, keepdims=True)
    xc = x - mu
    var = jnp.mean(xc * xc, axis=-1, keepdims=True)
    return _silu(xc * lax.rsqrt(var + EPS) * g + b)


def _conv_strip(ext_ref, w_ref, row_start, n_rows, lanes, width):
    n_lanes = lanes.stop - lanes.start
    total = None
    for r in range(SUBLANES):
        taps = [j for j in range(width) if (row_start + j) % SUBLANES == r]
        if not taps:
            continue
        rows = n_rows if r == 0 else n_rows + SUBLANES
        acc = jnp.zeros((rows, n_lanes), F32)
        for j in taps:
            start = row_start + j - r
            acc = acc + ext_ref[start:start + rows, lanes] * w_ref[j:j + 1, lanes]
        part = acc[r:r + n_rows, :]
        total = part if total is None else total + part
    return total


def _conv_prompt_kernel(u_ref, w_ref, wb_ref, g_ref, b_ref, c_ref, ext_ref, conv_ref, *, width):
    tb, cd = u_ref.shape
    first = HIST_ROWS - (width - 1)

    @pl.when(pl.program_id(1) == 0)
    def _():
        ext_ref[0:HIST_ROWS, :] = jnp.zeros((HIST_ROWS, cd), F32)

    ext_ref[HIST_ROWS:HIST_ROWS + tb, :] = u_ref[...]
    for r0 in range(0, tb, CONV_ROWS):
        for c0 in range(0, cd, CONV_LANES):
            lanes = slice(c0, c0 + CONV_LANES)
            conv_ref[r0:r0 + CONV_ROWS, lanes] = _conv_strip(ext_ref, w_ref, first + r0, CONV_ROWS, lanes, width)
    ext_ref[0:HIST_ROWS, :] = ext_ref[tb:tb + HIST_ROWS, :]
    c_ref[...] = _ln_silu(conv_ref[...] + wb_ref[...], g_ref[...], b_ref[...]).astype(c_ref.dtype)


def _conv_prompt(ug, w, wb, g, b, *, n_seq, seq_len):
    width, cd = w.shape
    assert width - 1 <= HIST_ROWS and cd % CONV_LANES == 0
    tb = _pick(seq_len, 64, CONV_ROWS)
    assert tb % CONV_ROWS == 0 and tb >= HIST_ROWS
    n_blk = seq_len // tb
    vec = pl.BlockSpec((1, cd), lambda s, t: (0, 0))
    return pl.pallas_call(
        functools.partial(_conv_prompt_kernel, width=width),
        grid=(n_seq, n_blk),
        in_specs=[pl.BlockSpec((tb, cd), lambda s, t: (s * n_blk + t, 0)),
                  pl.BlockSpec((width, cd), lambda s, t: (0, 0)), vec, vec, vec],
        out_specs=pl.BlockSpec((tb, cd), lambda s, t: (s * n_blk + t, 0)),
        out_shape=jax.ShapeDtypeStruct((n_seq * seq_len, cd), BF16),
        scratch_shapes=[pltpu.VMEM((HIST_ROWS + tb, cd), F32), pltpu.VMEM((tb, cd), F32)],
        compiler_params=_params(("parallel", "arbitrary"), 32),
        name="conv_prompt",
    )(ug, w, wb.reshape(1, cd), g.reshape(1, cd), b.reshape(1, cd))


def _conv_sample_kernel(u_ref, buf_ref, w_ref, wb_ref, g_ref, b_ref, c_ref, nbuf_ref, ext_ref, conv_ref,
                        *, width, seq_len):
    n_seq, hist, cd = buf_ref.shape
    pad_rows = ext_ref.shape[0] - hist - seq_len
    for s in range(n_seq):
        ext_ref[0:hist, :] = buf_ref[s]
        ext_ref[hist:hist + seq_len, :] = u_ref[s * seq_len:(s + 1) * seq_len, :]
        ext_ref[hist + seq_len:, :] = jnp.zeros((pad_rows, cd), F32)
        for c0 in range(0, cd, CONV_LANES):
            lanes = slice(c0, c0 + CONV_LANES)
            acc = _conv_strip(ext_ref, w_ref, 0, SUBLANES, lanes, width)
            conv_ref[s * seq_len:(s + 1) * seq_len, lanes] = acc[0:seq_len, :]
        nbuf_ref[s] = ext_ref[seq_len:seq_len + hist, :]
    c_ref[...] = _ln_silu(conv_ref[...] + wb_ref[...], g_ref[...], b_ref[...]).astype(c_ref.dtype)


def _conv_sample(ug, buf, w, wb, g, b, *, row0, seq_len):
    width, cd = w.shape
    n_seq, hist, _ = buf.shape
    assert hist == width - 1 and seq_len <= SUBLANES and cd % CONV_LANES == 0
    sb = _pick(n_seq, 8, 1)
    rows = sb * seq_len
    assert rows % 16 == 0 and row0 % rows == 0
    blk0 = row0 // rows
    ext_rows = -(-(hist + SUBLANES) // SUBLANES) * SUBLANES
    vec = pl.BlockSpec((1, cd), lambda i: (0, 0))
    return pl.pallas_call(
        functools.partial(_conv_sample_kernel, width=width, seq_len=seq_len),
        grid=(n_seq // sb,),
        in_specs=[pl.BlockSpec((rows, cd), lambda i: (blk0 + i, 0)),
                  pl.BlockSpec((sb, hist, cd), lambda i: (i, 0, 0)),
                  pl.BlockSpec((width, cd), lambda i: (0, 0)), vec, vec, vec],
        out_specs=[pl.BlockSpec((rows, cd), lambda i: (i, 0)),
                   pl.BlockSpec((sb, hist, cd), lambda i: (i, 0, 0))],
        out_shape=[jax.ShapeDtypeStruct((n_seq * seq_len, cd), BF16),
                   jax.ShapeDtypeStruct((n_seq, hist, cd), F32)],
        scratch_shapes=[pltpu.VMEM((ext_rows, cd), F32), pltpu.VMEM((rows, cd), F32)],
        compiler_params=_params(("parallel",), 32),
        name="conv_sample",
    )(ug, buf, w, wb.reshape(1, cd), g.reshape(1, cd), b.reshape(1, cd))


def _attn_kernel(q_ref, k_ref, v_ref, o_ref, *, heads, seq_rows, scale):
    rows, ca = q_ref.shape
    nseq = rows // seq_rows
    n_mem = k_ref.shape[0] // nseq
    hd = ca // heads
    q = q_ref[...]
    if nseq > 1:
        shift = seq_rows.bit_length() - 1
        row_seq = lax.broadcasted_iota(jnp.int32, (rows, 1), 0) >> shift
    for h in range(heads):
        hs = slice(h * hd, (h + 1) * hd)
        out = None
        for s in range(nseq):
            ms = slice(s * n_mem, (s + 1) * n_mem)
            sc = _dot_nt(q[:, hs], k_ref[ms, hs].astype(BF16)) * scale
            e = jnp.exp(sc - jnp.max(sc, axis=-1, keepdims=True))
            p = e / jnp.sum(e, axis=-1, keepdims=True)
            o = _dot(p.astype(BF16), v_ref[ms, hs].astype(BF16))
            out = o if out is None else jnp.where(row_seq == s, o, out)
        o_ref[:, hs] = out.astype(o_ref.dtype)


def _attn(q, k, v, *, row0, n_seq, seq_len, n_mem, heads, name):
    ca = q.shape[1]
    if seq_len % SUBLANES == 0:
        rows, nseq = _pick(seq_len, 256, 16), 1
    else:
        assert SUBLANES % seq_len == 0
        rows, nseq = 16, 16 // seq_len
    assert row0 % rows == 0 and (n_seq * seq_len) % rows == 0
    blk0 = row0 // rows
    per_seq = max(seq_len // rows, 1)
    kv_spec = pl.BlockSpec((nseq * n_mem, ca), lambda i: (i // per_seq, 0))
    return pl.pallas_call(
        functools.partial(_attn_kernel, heads=heads, seq_rows=seq_len if nseq > 1 else rows,
                          scale=(ca // heads) ** -0.5),
        grid=(n_seq * seq_len // rows,),
        in_specs=[pl.BlockSpec((rows, ca), lambda i: (blk0 + i, 0)), kv_spec, kv_spec],
        out_specs=pl.BlockSpec((rows, ca), lambda i: (i, 0)),
        out_shape=jax.ShapeDtypeStruct((n_seq * seq_len, ca), BF16),
        compiler_params=_params(("parallel",), 32),
        name=name,
    )(q, k, v)


ROUTE_E0, ROUTE_E1, ROUTE_G0, ROUTE_G1 = 0, 1, 2, 3
GATHER_UNROLL = 8
HIGH_HALF = 0xFFFF0000


def _token_stride(chunks):
    return chunks + SUBLANES


def _chunk(c, t0, n, chunks):
    stride = _token_stride(chunks)
    return pl.ds(t0 * stride + c, n, stride=stride), slice(None)


def _token_copies(hbm_ref, hbm_token, buf_ref, t0, n_tokens, chunks, sem, to_hbm):
    stride = _token_stride(chunks)

    def pair(hbm_rows, buf_rows):
        return (buf_rows, hbm_rows) if to_hbm else (hbm_rows, buf_rows)

    def start():
        def body(j, carry):
            for k in range(GATHER_UNROLL):
                r = j * GATHER_UNROLL + k
                src, dst = pair(hbm_ref.at[pl.ds(hbm_token(r) * chunks, chunks), :],
                                buf_ref.at[pl.ds((t0 + r) * stride, chunks), :])
                pltpu.make_async_copy(src, dst, sem).start()
            return carry

        lax.fori_loop(0, n_tokens // GATHER_UNROLL, body, 0)

    def wait():
        src, dst = pair(hbm_ref.at[pl.ds(0, n_tokens * chunks), :],
                        buf_ref.at[pl.ds(t0 * stride, n_tokens * chunks), :])
        pltpu.make_async_copy(src, dst, sem).wait()

    return start, wait


def _router_kernel(x_ref, g_ref, w_ref, b_ref, o_ref, htok_hbm, tokbuf, sem, *, n_groups, per_group):
    tm, d = x_ref.shape
    words = d // (2 * LANES)
    h = _rmsnorm_rows(x_ref[...], g_ref[...])
    bits = pltpu.bitcast(h.astype(BF16).astype(F32), jnp.uint32)
    for c in range(words):
        low = bits[:, 2 * c * LANES:(2 * c + 1) * LANES]
        high = bits[:, (2 * c + 1) * LANES:(2 * c + 2) * LANES]
        tokbuf[_chunk(c, 0, tm, words)] = (low >> 16) | (high & jnp.uint32(HIGH_HALF))
    row0 = pl.program_id(0) * tm
    start_tok_copy, wait_tok_copy = _token_copies(htok_hbm, lambda r: row0 + r, tokbuf, 0, tm, words, sem,
                                                  to_hbm=True)
    start_tok_copy()
    h_hi, h_lo = _split_bf16(h, 2)

    w_hi, w_lo = _split_bf16(w_ref[...], 2)
    logits = _dot(h_hi, w_hi) + (_dot(h_hi, w_lo) + _dot(h_lo, w_hi)) + b_ref[...]
    n_experts = n_groups * per_group
    lane_i = lax.broadcasted_iota(jnp.int32, logits.shape, 1)
    lane = lane_i.astype(F32)
    neg = jnp.float32(-jnp.inf)

    def first_argmax(x, m):
        return jnp.min(jnp.where(x == m, lane, float(LANES)), axis=-1, keepdims=True)

    lg = jnp.where(jnp.logical_and(lane_i >= n_experts, lane_i < n_experts + n_groups), logits, neg)
    mg = jnp.max(lg, axis=-1, keepdims=True)
    pg_top = 1.0 / jnp.sum(jnp.exp(lg - mg), axis=-1, keepdims=True)
    gsel = first_argmax(lg, mg) - float(n_experts)
    shift = per_group.bit_length() - 1
    in_group = jnp.logical_and(lane_i < n_experts, (lane_i >> shift).astype(F32) == gsel)
    le = jnp.where(in_group, logits, neg)
    m0 = jnp.max(le, axis=-1, keepdims=True)
    z = jnp.sum(jnp.exp(le - m0), axis=-1, keepdims=True)
    e0 = first_argmax(le, m0)
    le1 = jnp.where(lane == e0, neg, le)
    m1 = jnp.max(le1, axis=-1, keepdims=True)
    e1 = first_argmax(le1, m1)
    p0 = 1.0 / z
    p1 = jnp.exp(m1 - m0) / z
    den = p0 + p1
    g0 = pg_top * (p0 / den)
    g1 = pg_top * (p1 / den)
    rec = jnp.where(lane_i == ROUTE_E0, e0, 0.0)
    rec = jnp.where(lane_i == ROUTE_E1, e1, rec)
    rec = jnp.where(lane_i == ROUTE_G0, g0, rec)
    rec = jnp.where(lane_i == ROUTE_G1, g1, rec)
    o_ref[...] = rec
    wait_tok_copy()


def _router(x, g, w_group, b_group, w_expert, b_expert):
    t, d = x.shape
    n_groups, n_experts = w_group.shape[1], w_expert.shape[1]
    per_group = n_experts // n_groups
    assert n_experts + n_groups <= LANES and per_group & (per_group - 1) == 0
    pad = lambda a: jnp.pad(a, ((0, 0), (0, LANES - a.shape[1])))
    w_both = pad(jnp.concatenate([w_expert, w_group], axis=1))
    b_both = pad(jnp.concatenate([b_expert, b_group]).reshape(1, -1))
    words = d // (2 * LANES)
    tm = _pick(t, 256, GATHER_UNROLL)
    return pl.pallas_call(
        functools.partial(_router_kernel, n_groups=n_groups, per_group=per_group),
        grid=(t // tm,),
        in_specs=[pl.BlockSpec((tm, d), lambda i: (i, 0)), pl.BlockSpec((1, d), lambda i: (0, 0)),
                  pl.BlockSpec((d, LANES), lambda i: (0, 0)), pl.BlockSpec((1, LANES), lambda i: (0, 0))],
        out_specs=[pl.BlockSpec((tm, LANES), lambda i: (i, 0)), pl.BlockSpec(memory_space=pl.ANY)],
        out_shape=[jax.ShapeDtypeStruct((t, LANES), F32), jax.ShapeDtypeStruct((t * words, LANES), jnp.uint32)],
        scratch_shapes=[pltpu.VMEM((tm * _token_stride(words), LANES), jnp.uint32), pltpu.SemaphoreType.DMA(())],
        compiler_params=_params(("parallel",), 32),
        name="router",
    )(x, g.reshape(1, d), w_both, b_both)


def _dispatch_plan(e_ids, n_experts, tile):
    t, k = e_ids.shape
    n_pairs = t * k
    n_tiles = n_pairs // tile + n_experts
    flat = e_ids.reshape(n_pairs)
    onehot = (flat[:, None] == jnp.arange(n_experts, dtype=jnp.int32)[None, :]).astype(jnp.int32)
    counts = jnp.sum(onehot, axis=0)
    tiles_per = (counts + tile - 1) // tile
    tile_end = jnp.cumsum(tiles_per)
    tile_start = tile_end - tiles_per
    rank = jnp.take_along_axis(jnp.cumsum(onehot, axis=0), flat[:, None], axis=1)[:, 0] - 1
    pos = tile_start[flat] * tile + rank
    token = jnp.arange(n_pairs, dtype=jnp.int32) // k
    src = jnp.zeros((n_tiles * tile,), jnp.int32).at[pos].set(token)
    n_used = tile_end[-1]
    tile_ids = jnp.minimum(jnp.arange(n_tiles, dtype=jnp.int32), n_used - 1)
    tile_expert = jnp.searchsorted(tile_end, tile_ids, side="right").astype(jnp.int32)
    after = tile_end[tile_expert]
    next_expert = jnp.where(after < n_used, tile_expert[jnp.minimum(after, n_tiles - 1)], -1).astype(jnp.int32)
    return (pos.reshape(t, k), src.reshape(n_tiles, 1, tile), tile_expert, next_expert,
            n_used.reshape(1).astype(jnp.int32))


def _cast_rows(src_ref, dst_ref, rows_per):
    def body(c, carry):
        rows = pl.ds(pl.multiple_of(c * rows_per, rows_per), rows_per)
        dst_ref[rows, :] = src_ref[rows, :].astype(dst_ref.dtype)
        return carry

    lax.fori_loop(0, src_ref.shape[0] // rows_per, body, 0)


DOWN_COLS = 1024


def _experts_kernel(texp_ref, nexp_ref, nused_ref, idx_ref, idx_next_ref, h_hbm, wg_hbm, wu_hbm, wd_hbm,
                    y_hbm, xbuf, ybuf, hbuf, stage_g, stage_u, stage_d, wg_bf, wu_bf, wd_bf, gsem, ysem, wsem):
    i = pl.program_id(0)
    n_used = nused_ref[0]
    tile, d = hbuf.shape
    chunks = d // LANES
    words = chunks // 2
    slot = lax.rem(i, 2)
    expert = texp_ref[i]
    start_tokens, wait_tokens = _token_copies(h_hbm, lambda r: idx_ref[0, 0, r], xbuf, slot * tile, tile, words,
                                              gsem.at[slot], to_hbm=False)
    start_next_tokens, _ = _token_copies(h_hbm, lambda r: idx_next_ref[0, 0, r], xbuf, (1 - slot) * tile, tile,
                                         words, gsem.at[1 - slot], to_hbm=False)

    weights = ((wg_hbm, stage_g, wg_bf, 512), (wu_hbm, stage_u, wu_bf, 512), (wd_hbm, stage_d, wd_bf, 64))

    def weight_copy(e, k):
        return pltpu.make_async_copy(weights[k][0].at[e], weights[k][1], wsem.at[k])

    def y_copy(step):
        return _token_copies(y_hbm, lambda r: step * tile + r, ybuf, 0, tile, chunks, ysem, to_hbm=True)

    @pl.when(i < n_used)
    def _():
        @pl.when(i == 0)
        def _():
            start_tokens()
            for k in range(len(weights)):
                weight_copy(expert, k).start()

        @pl.when(i + 1 < n_used)
        def _():
            start_next_tokens()

        @pl.when(jnp.logical_or(i == 0, expert != texp_ref[jnp.maximum(i - 1, 0)]))
        def _():
            nxt = nexp_ref[i]
            for k, (_, stage, dst, rows_per) in enumerate(weights):
                weight_copy(expert, k).wait()
                _cast_rows(stage, dst, rows_per)
                pl.when(nxt >= 0)(lambda k=k: weight_copy(nxt, k).start())

        wait_tokens()
        for c in range(words):
            word = xbuf[_chunk(c, slot * tile, tile, words)]
            low = pltpu.bitcast(word << 16, F32)
            high = pltpu.bitcast(word & jnp.uint32(HIGH_HALF), F32)
            hbuf[:, 2 * c * LANES:(2 * c + 1) * LANES] = low.astype(BF16)
            hbuf[:, (2 * c + 1) * LANES:(2 * c + 2) * LANES] = high.astype(BF16)
        h = hbuf[...]
        a = _dot(h, wg_bf[...])
        u = _dot(h, wu_bf[...])
        hid = (_silu(a) * u).astype(BF16)

        @pl.when(i > 0)
        def _():
            y_copy(i - 1)[1]()

        for c0 in range(0, d, DOWN_COLS):
            y = _dot(hid, wd_bf[:, c0:c0 + DOWN_COLS])
            for c in range(DOWN_COLS // LANES):
                ybuf[_chunk(c0 // LANES + c, 0, tile, chunks)] = y[:, c * LANES:(c + 1) * LANES]

    @pl.when(i >= n_used)
    def _():
        y_copy(i - 1)[1]()

        @pl.when(i == n_used)
        def _():
            ybuf[...] = jnp.zeros(ybuf.shape, F32)

    y_copy(i)[0]()

    @pl.when(i == pl.num_programs(0) - 1)
    def _():
        y_copy(i)[1]()


def _experts(h_tok, src, tile_expert, next_expert, n_used, wg, wu, wd):
    d = wg.shape[1]
    chunks = d // LANES
    words = chunks // 2
    n_tiles, _, tile = src.shape
    de = wg.shape[2]
    assert d % 512 == 0 and de % 64 == 0 and d % DOWN_COLS == 0 and tile % GATHER_UNROLL == 0
    idx_spec = lambda ahead: pl.BlockSpec(
        (1, 1, tile), lambda i, texp, nexp, nused: (jnp.minimum(i + ahead, nused[0] - 1), 0, 0),
        memory_space=pltpu.SMEM)
    hbm = pl.BlockSpec(memory_space=pl.ANY)
    grid_spec = pltpu.PrefetchScalarGridSpec(
        num_scalar_prefetch=3,
        grid=(n_tiles,),
        in_specs=[idx_spec(0), idx_spec(1), hbm, hbm, hbm, hbm],
        out_specs=hbm,
        scratch_shapes=[pltpu.VMEM((2 * tile * _token_stride(words), LANES), jnp.uint32),
                        pltpu.VMEM((tile * _token_stride(chunks), LANES), F32), pltpu.VMEM((tile, d), BF16),
                        pltpu.VMEM((d, de), F32), pltpu.VMEM((d, de), F32), pltpu.VMEM((de, d), F32),
                        pltpu.VMEM((d, de), BF16), pltpu.VMEM((d, de), BF16), pltpu.VMEM((de, d), BF16),
                        pltpu.SemaphoreType.DMA((2,)), pltpu.SemaphoreType.DMA(()), pltpu.SemaphoreType.DMA((3,))],
    )
    weights_bytes = 3 * d * de * (4 + 2)
    tiles_bytes = tile * (2 * _token_stride(words) + _token_stride(chunks)) * LANES * 4
    tiles_bytes += tile * d * 2 + tile * DOWN_COLS * 4 + 3 * tile * de * 4
    return pl.pallas_call(
        _experts_kernel,
        grid_spec=grid_spec,
        out_shape=jax.ShapeDtypeStruct((n_tiles * tile * chunks, LANES), F32),
        compiler_params=_params(("arbitrary",), (weights_bytes + tiles_bytes) // MIB + 3),
        name="experts",
    )(tile_expert, next_expert, n_used, src, src, h_tok, wg, wu, wd)


def _combine_kernel(idx0_ref, idx1_ref, idx0_next_ref, idx1_next_ref, y_hbm, x_ref, route_ref, gf_ref, *refs,
                    n_first, final_norm):
    outs, (buf0, buf1, res_ref, sem) = refs[:-4], refs[-4:]
    i = pl.program_id(0)
    tm, d = x_ref.shape
    chunks = d // LANES
    slot = lax.rem(i, 2)
    def gather(idx_ref, buf, k, into_slot):
        return _token_copies(y_hbm, lambda r: idx_ref[0, 0, r], buf, into_slot * tm, tm, chunks,
                             sem.at[k, into_slot], to_hbm=False)

    gathers = [gather(idx0_ref, buf0, 0, slot), gather(idx1_ref, buf1, 1, slot)]
    gathers_next = [gather(idx0_next_ref, buf0, 0, 1 - slot), gather(idx1_next_ref, buf1, 1, 1 - slot)]

    @pl.when(i == 0)
    def _():
        for start, _ in gathers:
            start()

    @pl.when(i + 1 < pl.num_programs(0))
    def _():
        for start, _ in gathers_next:
            start()

    for _, wait in gathers:
        wait()
    g0 = route_ref[:, ROUTE_G0:ROUTE_G0 + 1]
    g1 = route_ref[:, ROUTE_G1:ROUTE_G1 + 1]
    ssq = jnp.zeros((tm, LANES), F32)
    for c in range(chunks):
        lanes = slice(c * LANES, (c + 1) * LANES)
        y0 = buf0[_chunk(c, slot * tm, tm, chunks)]
        y1 = buf1[_chunk(c, slot * tm, tm, chunks)]
        r = x_ref[:, lanes] + (g0 * y0 + g1 * y1)
        res_ref[:, lanes] = r
        ssq = ssq + r * r
    if final_norm:
        scale = lax.rsqrt(jnp.sum(ssq, axis=-1, keepdims=True) * (1.0 / d) + EPS)
        result = lambda: res_ref[...] * scale * gf_ref[...]
    else:
        result = lambda: res_ref[...]
    if len(outs) == 1:
        outs[0][...] = result()
    else:
        @pl.when(i < n_first)
        def _():
            outs[0][...] = result()

        @pl.when(i >= n_first)
        def _():
            outs[1][...] = result()


def _combine(x, y_sorted, pos, route, g_final, *, split_rows, final_norm):
    t, d = x.shape
    assert y_sorted.shape[1] == LANES
    tm = _pick(t if split_rows is None else math.gcd(split_rows, t - split_rows), 128, GATHER_UNROLL)
    n_blk = t // tm
    idx = [pos[:, k].reshape(n_blk, 1, tm) for k in range(TOP_K_IN_GROUP)]
    smem = pl.BlockSpec((1, 1, tm), lambda i: (i, 0, 0), memory_space=pltpu.SMEM)
    smem_next = pl.BlockSpec((1, 1, tm), lambda i: (jnp.minimum(i + 1, n_blk - 1), 0, 0), memory_space=pltpu.SMEM)
    row = pl.BlockSpec((tm, d), lambda i: (i, 0))
    if split_rows is None:
        n_first = n_blk
        out_specs = [row]
        out_shape = [jax.ShapeDtypeStruct((t, d), F32)]
    else:
        n_first = split_rows // tm
        out_specs = [pl.BlockSpec((tm, d), lambda i: (jnp.minimum(i, n_first - 1), 0)),
                     pl.BlockSpec((tm, d), lambda i: (jnp.maximum(i - n_first, 0), 0))]
        out_shape = [jax.ShapeDtypeStruct((split_rows, d), F32), jax.ShapeDtypeStruct((t - split_rows, d), F32)]
    return pl.pallas_call(
        functools.partial(_combine_kernel, n_first=n_first, final_norm=final_norm),
        grid=(n_blk,),
        in_specs=[smem, smem, smem_next, smem_next, pl.BlockSpec(memory_space=pl.ANY), row,
                  pl.BlockSpec((tm, LANES), lambda i: (i, 0)), pl.BlockSpec((1, d), lambda i: (0, 0))],
        out_specs=out_specs,
        out_shape=out_shape,
        scratch_shapes=[pltpu.VMEM((2 * tm * _token_stride(d // LANES), LANES), F32),
                        pltpu.VMEM((2 * tm * _token_stride(d // LANES), LANES), F32), pltpu.VMEM((tm, d), F32),
                        pltpu.SemaphoreType.DMA((TOP_K_IN_GROUP, 2))],
        compiler_params=_params(("arbitrary",), 32),
        name="combine",
    )(idx[0], idx[1], idx[0], idx[1], y_sorted, x, route, g_final.reshape(1, d))


def kernel(x_prompt, x_sample, mem_prompt, state_gla, state_conv, cache_mem_k, cache_mem_v, norm_mix_g, w_in, w_alpha_up, b_alpha, gla_norm_g, w_branch_a, conv_dw_w, conv_dw_b, conv_ln_g, conv_ln_b, w_branch_b, w_out, norm_ca_g, norm_mem_g, w_ca_q, w_ca_k, w_ca_v, w_ca_o, norm_ffn_g, w_router_group, b_router_group, w_router_expert, b_router_expert, w_exp_gate, w_exp_up, w_exp_down, norm_final_g):
    depth = w_in.shape[0]
    bp, tp, d = x_prompt.shape
    bs, ts, _ = x_sample.shape
    heads, dk, dv = state_gla.shape[2:]
    key, val = heads * dk, heads * dv
    rank = w_alpha_up.shape[1]
    cd = state_conv.shape[3]
    n_mem, ca_heads, ca_hd = cache_mem_k.shape[2:]
    ca = ca_heads * ca_hd
    n_experts = w_router_expert.shape[2]
    rows_p, rows_s = bp * tp, bs * ts
    off_a = 2 * key + 2 * val
    off_u = off_a + rank
    off_g = off_u + 2 * cd
    assert rank <= LANES and w_in.shape[2] == off_g + 2 * d
    bf = lambda a: a.astype(BF16)

    x = (x_prompt.reshape(rows_p, d), x_sample.reshape(rows_s, d))
    outs = dict(gla_p=[], conv_p=[], mk_p=[], mv_p=[], gla_s=[], conv_s=[])
    for l in range(depth):
        w_in_t = jnp.swapaxes(w_in, 1, 2)[l]
        w_qkvr = _cast_weight(w_in_t, 0, off_a)
        w_ug = _cast_weight(w_in_t, off_u, 2 * cd + 2 * d)
        w_a = bf(jnp.pad(w_in[l, :, off_a:off_u], ((0, 0), (0, LANES - rank))))
        h = _rmsnorm(x, norm_mix_g[l], BF16)
        z = _mm([h], [w_qkvr], [0], _ep_identity, F32, w_rows=[True], tm_pref=WIDE_ROWS, name="in_qkvr")
        a_low = _mm([h], [w_a], [0], _ep_identity, F32, name="in_alow")
        ug = _mm([h], [w_ug, w_ug], [0, 0], _ep_glu, F32, n=cd, w_col0=[0, cd], w_rows=[True, True],
                 tm_pref=WIDE_ROWS, name="in_glu")
        w_up = bf(jnp.pad(w_alpha_up[l], ((0, LANES - rank), (0, 0))))
        gla_args = dict(heads=heads, dk=dk, dv=dv)
        og_p, st_p = _gla(z, a_low, w_up, b_alpha[l], gla_norm_g[l], None, row0=0, n_seq=bp, seq_len=tp,
                          name="gla_prompt", **gla_args)
        og_s, st_s = _gla(z, a_low, w_up, b_alpha[l], gla_norm_g[l], state_gla[l], row0=rows_p, n_seq=bs,
                          seq_len=ts, name="gla_sample", **gla_args)
        conv_w = (conv_dw_w[l], conv_dw_b[l], conv_ln_g[l], conv_ln_b[l])
        c_p = _conv_prompt(ug, *conv_w, n_seq=bp, seq_len=tp)
        c_s, buf_s = _conv_sample(ug, state_conv[l], *conv_w, row0=rows_p, seq_len=ts)
        og = (og_p, og_s)
        c = (c_p, c_s)
        merged = _mm([og, c, h], [w_ug, w_ug, bf(w_branch_a[l]), bf(w_branch_b[l])], [2, 2, 0, 1],
                     _ep_gated_merge, BF16, n=d, w_col0=[2 * cd, 2 * cd + d, 0, 0],
                     w_rows=[True, True, False, False], tn_pref=256, name="merge")
        x = _mm([merged], [bf(w_out[l])], [0], _ep_residual, F32, extras=[x], name="out_proj")
        q = _mm([x], [bf(w_ca_q[l])], [0], _ep_identity, BF16, norm_gain=norm_ca_g[l], name="ca_q")
        m = _rmsnorm(mem_prompt.reshape(bp * n_mem, d), norm_mem_g[l], BF16)
        mk = _mm([m], [bf(w_ca_k[l])], [0], _ep_identity, F32, name="mem_k")
        mv = _mm([m], [bf(w_ca_v[l])], [0], _ep_identity, F32, name="mem_v")
        ao_p = _attn(q, mk, mv, row0=0, n_seq=bp, seq_len=tp, n_mem=n_mem, heads=ca_heads, name="attn_prompt")
        ao_s = _attn(q, cache_mem_k[l].reshape(bs * n_mem, ca), cache_mem_v[l].reshape(bs * n_mem, ca),
                     row0=rows_p, n_seq=bs, seq_len=ts, n_mem=n_mem, heads=ca_heads, name="attn_sample")
        ao = (ao_p, ao_s)
        x = _mm([ao], [bf(w_ca_o[l])], [0], _ep_residual, F32, extras=[x], name="ca_out")
        route, h_tok = _router(x, norm_ffn_g[l], w_router_group[l], b_router_group[l], w_router_expert[l],
                               b_router_expert[l])
        e_ids = route[:, ROUTE_E0:ROUTE_E1 + 1].astype(jnp.int32)
        tile = _pick(rows_p + rows_s, 256, SUBLANES)
        pos, src, tile_expert, next_expert, n_used = _dispatch_plan(e_ids, n_experts, tile)
        y_sorted = _experts(h_tok, src, tile_expert, next_expert, n_used, w_exp_gate[l], w_exp_up[l],
                            w_exp_down[l])
        last = l == depth - 1
        res = _combine(x, y_sorted, pos, route, norm_final_g, split_rows=rows_p if last else None,
                       final_norm=last)
        if not last:
            x = res[0]
        outs["gla_p"].append(st_p)
        hist = conv_dw_w.shape[1] - 1
        outs["conv_p"].append(jnp.stack([ug[(b + 1) * tp - hist:(b + 1) * tp] for b in range(bp)]))
        outs["mk_p"].append(mk.reshape(bp, n_mem, ca_heads, ca_hd))
        outs["mv_p"].append(mv.reshape(bp, n_mem, ca_heads, ca_hd))
        outs["gla_s"].append(st_s)
        outs["conv_s"].append(buf_s)
    y_prompt = res[0].reshape(bp, tp, d)
    y_sample = res[1].reshape(bs, ts, d)
    return (y_prompt, y_sample, jnp.stack(outs["gla_p"]), jnp.stack(outs["conv_p"]), jnp.stack(outs["mk_p"]),
            jnp.stack(outs["mv_p"]), jnp.stack(outs["gla_s"]), jnp.stack(outs["conv_s"]))
```

```python
import functools
import math

import jax
import jax.numpy as jnp
from jax import lax
from jax.experimental import pallas as pl
from jax.experimental.pallas import tpu as pltpu

F32 = jnp.float32
BF16 = jnp.bfloat16

EPS = 1e-6
GLA_TAU = 16.0
GLA_CHUNK = 64
TOP_K_IN_GROUP = 2
LANES = 128
SUBLANES = 8
MIB = 1 << 20


def _pick(n, pref, mult):
    for d in range(min(pref, n), 0, -1):
        if n % d == 0 and d % mult == 0:
            return d
    return n


def _params(semantics, vmem_mib):
    return pltpu.CompilerParams(dimension_semantics=semantics, vmem_limit_bytes=vmem_mib * MIB)


def _sigmoid(x):
    return 1.0 / (1.0 + jnp.exp(-x))


def _silu(x):
    return x * _sigmoid(x)


def _dot(a, b):
    return jnp.dot(a, b, preferred_element_type=F32)


def _dot_nt(a, b):
    return lax.dot_general(a, b, (((1,), (1,)), ((), ())), preferred_element_type=F32)


def _dot_tn(a, b):
    return lax.dot_general(a, b, (((0,), (0,)), ((), ())), preferred_element_type=F32)


def _split_bf16(x, parts):
    out = []
    for _ in range(parts - 1):
        p = x.astype(BF16)
        out.append(p)
        x = x - p.astype(F32)
    out.append(x.astype(BF16))
    return out


def _rmsnorm_rows(x, g):
    return x * lax.rsqrt(jnp.mean(x * x, axis=-1, keepdims=True) + EPS) * g


def _parts(a):
    return a if isinstance(a, tuple) else (a,)


def _rows_of(a):
    return sum(p.shape[0] for p in _parts(a))


def _part_specs(a, tm, cols, col_of):
    parts = _parts(a)
    if len(parts) == 1:
        return [pl.BlockSpec((tm, cols), lambda i, *j: (i, col_of(*j)))]
    n_first = parts[0].shape[0] // tm
    assert parts[0].shape[0] % tm == 0 and parts[1].shape[0] % tm == 0
    return [pl.BlockSpec((tm, cols), lambda i, *j: (jnp.minimum(i, n_first - 1), col_of(*j))),
            pl.BlockSpec((tm, cols), lambda i, *j: (jnp.maximum(i - n_first, 0), col_of(*j)))]


def _part_tile(a, pref, mult):
    return _pick(math.gcd(*[p.shape[0] for p in _parts(a)]), pref, mult)


def _select_part(refs, n_first):
    if len(refs) == 1:
        return refs[0][...]
    return jnp.where(pl.program_id(0) < n_first, refs[0][...], refs[1][...])


def _rmsnorm_kernel(*refs, n_first):
    x_refs, (g_ref, o_ref) = refs[:-2], refs[-2:]
    o_ref[...] = _rmsnorm_rows(_select_part(x_refs, n_first), g_ref[...]).astype(o_ref.dtype)


def _rmsnorm(x, g, out_dtype):
    t, d = _rows_of(x), _parts(x)[0].shape[1]
    tm = _part_tile(x, 128, 16)
    return pl.pallas_call(
        functools.partial(_rmsnorm_kernel, n_first=_parts(x)[0].shape[0] // tm),
        grid=(t // tm,),
        in_specs=_part_specs(x, tm, d, lambda: 0) + [pl.BlockSpec((1, d), lambda i: (0, 0))],
        out_specs=pl.BlockSpec((tm, d), lambda i: (i, 0)),
        out_shape=jax.ShapeDtypeStruct((t, d), out_dtype),
        compiler_params=_params(("parallel",), 32),
        name="rmsnorm",
    )(*_parts(x), g.reshape(1, d))


def _cast_weight_kernel(w_hbm, o_ref, buf, sem, *, row0):
    j = pl.program_id(0)
    tn = o_ref.shape[0]
    slot = lax.rem(j, 2)

    def copy(block, into):
        rows = pl.ds(pl.multiple_of(row0 + block * tn, SUBLANES), tn)
        return pltpu.make_async_copy(w_hbm.at[rows, :], buf.at[into], sem.at[into])

    @pl.when(j == 0)
    def _():
        copy(j, slot).start()

    @pl.when(j + 1 < pl.num_programs(0))
    def _():
        copy(j + 1, 1 - slot).start()

    copy(j, slot).wait()
    o_ref[...] = buf[slot].astype(BF16)


def _cast_weight(w, row0, n_rows):
    _, k = w.shape
    tn = _pick(n_rows, 256, 16)
    assert row0 % SUBLANES == 0 and n_rows % tn == 0
    return pl.pallas_call(
        functools.partial(_cast_weight_kernel, row0=row0),
        grid=(n_rows // tn,),
        in_specs=[pl.BlockSpec(memory_space=pl.ANY)],
        out_specs=pl.BlockSpec((tn, k), lambda j: (j, 0)),
        out_shape=jax.ShapeDtypeStruct((n_rows, k), BF16),
        scratch_shapes=[pltpu.VMEM((2, tn, k), F32), pltpu.SemaphoreType.DMA((2,))],
        compiler_params=_params(("arbitrary",), 32),
        name="cast_weight",
    )(w)


def _ep_identity(accs, extras):
    return accs[0]


def _ep_glu(accs, extras):
    return accs[0] * _sigmoid(accs[1])


def _ep_residual(accs, extras):
    return extras[0] + accs[0]


def _ep_gated_merge(accs, extras):
    return _sigmoid(accs[0]) * accs[2] + _sigmoid(accs[1]) * accs[3]


WIDE_ROWS = 1088


def _side_cast_step(step, n_chunks, src_hbm, dst_hbm, row0, in_buf, out_buf, in_sem, out_sem):
    rows = in_buf.shape[1]
    slot = lax.rem(step, 2)

    def read(chunk, into):
        start = pl.multiple_of(row0 + chunk * rows, SUBLANES)
        return pltpu.make_async_copy(src_hbm.at[pl.ds(start, rows), :], in_buf.at[into], in_sem.at[into])

    def write(chunk):
        start = pl.multiple_of(chunk * rows, 16)
        return pltpu.make_async_copy(out_buf, dst_hbm.at[pl.ds(start, rows), :], out_sem)

    @pl.when(step < n_chunks)
    def _():
        pl.when(step == 0)(lambda: read(step, slot).start())
        pl.when(step + 1 < n_chunks)(lambda: read(step + 1, 1 - slot).start())
        read(step, slot).wait()
        pl.when(step > 0)(lambda: write(step - 1).wait())
        out_buf[...] = in_buf[slot].astype(BF16)
        write(step).start()
        pl.when(step == n_chunks - 1)(lambda: write(step).wait())


def _mm_kernel(*refs, x_of_w, w_rows, x_parts, e_parts, n_first, has_norm, side, epilogue):
    refs = list(refs)
    take = lambda k: [refs.pop(0) for _ in range(k)]
    x_refs = [take(p) for p in x_parts]
    g_ref = refs.pop(0) if has_norm else None
    w_refs = take(len(x_of_w))
    e_refs = [take(p) for p in e_parts]
    side_src = take(len(side))
    o_ref = refs.pop(0)
    side_dst = take(len(side))
    i = pl.program_id(0)
    step = i * pl.num_programs(1) + pl.program_id(1)
    side_scratch = [refs[len(refs) - 4 * len(side) + 4 * k:len(refs) - 4 * len(side) + 4 * k + 4]
                    for k in range(len(side))]
    for (row0, n_chunks), src, dst, (in_buf, out_buf, in_sem, out_sem) in zip(side, side_src, side_dst, side_scratch):
        _side_cast_step(step, n_chunks, src, dst, row0, in_buf, out_buf, in_sem, out_sem)
    first_col = pl.program_id(1) == 0
    xs = []
    for k, parts in enumerate(x_refs):
        norm = has_norm and k == 0
        if len(parts) == 1 and not norm:
            xs.append(parts[0])
            continue
        scratch = refs.pop(0)

        def fill(src, scratch=scratch, norm=norm):
            val = src[...]
            if norm:
                val = _rmsnorm_rows(val, g_ref[...])
            scratch[...] = val.astype(scratch.dtype)

        if len(parts) == 1:
            pl.when(first_col)(functools.partial(fill, parts[0]))
        else:
            pl.when(jnp.logical_and(first_col, i < n_first))(functools.partial(fill, parts[0]))
            pl.when(jnp.logical_and(first_col, i >= n_first))(functools.partial(fill, parts[1]))
        xs.append(scratch)
    accs = [(_dot_nt if w_rows[k] else _dot)(xs[x_of_w[k]][...], w_refs[k][...]) for k in range(len(x_of_w))]
    o_ref[...] = epilogue(accs, [_select_part(e, n_first) for e in e_refs]).astype(o_ref.dtype)


def _mm(xs, ws, x_of_w, epilogue, out_dtype, extras=(), tm_pref=512, tn_pref=512, name="mm", n=None, w_col0=None,
        w_rows=None, norm_gain=None, side_cast=()):
    t = _rows_of(xs[0])
    w_rows = [False] * len(ws) if w_rows is None else w_rows
    w_col0 = [0] * len(ws) if w_col0 is None else w_col0
    n = ws[0].shape[0 if w_rows[0] else 1] if n is None else n
    tm = min(_part_tile(a, tm_pref, 16) for a in list(xs) + list(extras))
    tn = _pick(math.gcd(n, *w_col0), tn_pref, LANES)
    split = [a for a in list(xs) + list(extras) if len(_parts(a)) == 2]
    assert all(a[0].shape[0] == split[0][0].shape[0] and a[0].shape[0] % tm == 0 for a in split)
    n_first = split[0][0].shape[0] // tm if split else 0
    k_of = lambda x: _parts(x)[0].shape[1]
    in_specs, args = [], []
    for x in xs:
        in_specs += _part_specs(x, tm, k_of(x), lambda j: 0)
        args += _parts(x)
    if norm_gain is not None:
        in_specs.append(pl.BlockSpec((1, k_of(xs[0])), lambda i, j: (0, 0)))
        args.append(norm_gain.reshape(1, -1))
    for w, c0, rows in zip(ws, w_col0, w_rows):
        if rows:
            in_specs.append(pl.BlockSpec((tn, w.shape[1]), lambda i, j, b0=c0 // tn: (b0 + j, 0)))
        else:
            in_specs.append(pl.BlockSpec((w.shape[0], tn), lambda i, j, b0=c0 // tn: (0, b0 + j)))
    args += ws
    for e in extras:
        in_specs += _part_specs(e, tm, tn, lambda j: j)
        args += _parts(e)
    staged = [x for k, x in enumerate(xs) if len(_parts(x)) == 2 or (norm_gain is not None and k == 0)]
    scratch = [pltpu.VMEM((tm, k_of(x)), BF16) for x in staged]
    steps = (t // tm) * (n // tn)
    side, side_out_shape, side_bytes = [], [], 0
    for src, row0, n_rows in side_cast:
        rows = next(r for r in range(16, n_rows + 1, 16) if n_rows % r == 0 and n_rows // r <= steps)
        assert row0 % SUBLANES == 0
        side.append((row0, n_rows // rows))
        in_specs.append(pl.BlockSpec(memory_space=pl.ANY))
        args.append(src)
        side_out_shape.append(jax.ShapeDtypeStruct((n_rows, src.shape[1]), BF16))
        scratch += [pltpu.VMEM((2, rows, src.shape[1]), F32), pltpu.VMEM((rows, src.shape[1]), BF16),
                    pltpu.SemaphoreType.DMA((2,)), pltpu.SemaphoreType.DMA(())]
        side_bytes += rows * src.shape[1] * (2 * 4 + 2)
    block_bytes = sum(tm * p.shape[1] * p.dtype.itemsize for x in xs for p in _parts(x))
    block_bytes += sum(w.size // (w.shape[0 if rows else 1] // tn) * w.dtype.itemsize for w, rows in zip(ws, w_rows))
    block_bytes += sum(tm * tn * p.dtype.itemsize for e in extras for p in _parts(e))
    block_bytes += tm * tn * jnp.dtype(out_dtype).itemsize
    scratch_bytes = sum(tm * k_of(x) * 2 for x in staged)
    norm_bytes = 2 * tm * k_of(xs[0]) * 4 if norm_gain is not None else 0
    acc_bytes = len(ws) * tm * tn * 4
    vmem = (2 * block_bytes + 2 * acc_bytes + scratch_bytes + norm_bytes + side_bytes) // MIB + 4
    outs = pl.pallas_call(
        functools.partial(_mm_kernel, x_of_w=tuple(x_of_w), w_rows=tuple(w_rows),
                          x_parts=tuple(len(_parts(x)) for x in xs), e_parts=tuple(len(_parts(e)) for e in extras),
                          n_first=n_first, has_norm=norm_gain is not None, side=tuple(side), epilogue=epilogue),
        grid=(t // tm, n // tn),
        in_specs=in_specs,
        out_specs=[pl.BlockSpec((tm, tn), lambda i, j: (i, j))] + [pl.BlockSpec(memory_space=pl.ANY)] * len(side),
        out_shape=[jax.ShapeDtypeStruct((t, n), out_dtype)] + side_out_shape,
        scratch_shapes=scratch,
        compiler_params=_params(("arbitrary", "arbitrary") if side else ("parallel", "arbitrary"), vmem),
        name=name,
    )(*args)
    return (outs[0], outs[1:]) if side else outs[0]


def _log_sigmoid(x):
    return jnp.minimum(x, 0.0) - jnp.log1p(jnp.exp(-jnp.abs(x)))


def _gla_kernel(*refs, heads, seq_rows, has_h0, scale):
    if has_h0:
        q_ref, k_ref, v_ref, r_ref, a_ref, wup_ref, bal_ref, gn_ref, h0_ref, og_ref, st_ref = refs
    else:
        q_ref, k_ref, v_ref, r_ref, a_ref, wup_ref, bal_ref, gn_ref, og_ref, st_ref = refs
    rows, key = q_ref.shape
    val = v_ref.shape[1]
    dk, dv = key // heads, val // heads
    nseq = rows // seq_rows

    @pl.when(pl.program_id(1) == 0)
    def _():
        if has_h0:
            st_ref[...] = h0_ref[...]
        else:
            st_ref[...] = jnp.zeros(st_ref.shape, F32)

    la = _dot(a_ref[...].astype(BF16), wup_ref[...]) + bal_ref[...]
    log_a = _log_sigmoid(la) * (1.0 / GLA_TAU)

    ri = lax.broadcasted_iota(jnp.int32, (rows, rows), 0)
    ci = lax.broadcasted_iota(jnp.int32, (rows, rows), 1)
    causal = ci <= ri
    if nseq > 1:
        shift = seq_rows.bit_length() - 1
        causal = jnp.logical_and(causal, (ri >> shift) == (ci >> shift))
        row_seq = lax.broadcasted_iota(jnp.int32, (rows, 1), 0) >> shift
    tri = jnp.where(causal, 1.0, 0.0).astype(BF16)
    b = sum(_dot(tri, part) for part in _split_bf16(log_a, 3))

    q = q_ref[...] * scale
    k = k_ref[...]
    v = v_ref[...]
    r = r_ref[...]
    gn = gn_ref[...]
    for h in range(heads):
        ks = slice(h * dk, (h + 1) * dk)
        vs = slice(h * dv, (h + 1) * dv)
        bh = b[:, ks]
        q_e = (q[:, ks] * jnp.exp(bh)).astype(BF16)
        k_e = (k[:, ks] * jnp.exp(-bh)).astype(BF16)
        vh = v[:, vs].astype(BF16)
        scores = jnp.where(causal, _dot_nt(q_e, k_e), 0.0)
        o = _dot(scores.astype(BF16), vh)
        for s in range(nseq):
            b_last = bh[(s + 1) * seq_rows - 1:(s + 1) * seq_rows, :]
            k_s = k[:, ks] * jnp.exp(b_last - bh)
            if nseq > 1:
                in_seq = row_seq == s
                k_s = jnp.where(in_seq, k_s, 0.0)
            state = st_ref[s, h]
            o_inter = _dot(q_e, state.astype(BF16))
            o = o + (jnp.where(in_seq, o_inter, 0.0) if nseq > 1 else o_inter)
            d_state = _dot_tn(k_s.astype(BF16), vh)
            decay_col = jnp.broadcast_to(jnp.exp(b_last), (LANES, dk)).T[:, 0:1]
            st_ref[s, h] = decay_col * state + d_state
        mu = jnp.mean(o, axis=-1, keepdims=True)
        oc = o - mu
        o_n = oc * lax.rsqrt(jnp.mean(oc * oc, axis=-1, keepdims=True) + EPS) * gn[:, vs]
        og_ref[:, vs] = (o_n * _silu(r[:, vs])).astype(og_ref.dtype)


def _gla(z, a_low, w_up, b_alpha, gn, h0, *, row0, n_seq, seq_len, heads, dk, dv, name):
    key, val = heads * dk, heads * dv
    assert val == 2 * key and z.shape[1] == 2 * key + 2 * val
    chunk = min(GLA_CHUNK, seq_len)
    assert seq_len % chunk == 0
    if chunk % SUBLANES == 0:
        rows, n_chunks = chunk, seq_len // chunk
    else:
        assert SUBLANES % seq_len == 0 and chunk == seq_len
        rows, n_chunks = SUBLANES, 1
    nseq = rows // chunk
    assert n_seq % nseq == 0 and row0 % rows == 0
    blk0 = row0 // rows
    row_map = lambda g, c: blk0 + g * n_chunks + c
    in_specs = [
        pl.BlockSpec((rows, key), lambda g, c: (row_map(g, c), 0)),
        pl.BlockSpec((rows, key), lambda g, c: (row_map(g, c), 1)),
        pl.BlockSpec((rows, val), lambda g, c: (row_map(g, c), 1)),
        pl.BlockSpec((rows, val), lambda g, c: (row_map(g, c), 2)),
        pl.BlockSpec((rows, LANES), lambda g, c: (row_map(g, c), 0)),
        pl.BlockSpec((LANES, key), lambda g, c: (0, 0)),
        pl.BlockSpec((1, key), lambda g, c: (0, 0)),
        pl.BlockSpec((1, val), lambda g, c: (0, 0)),
    ]
    args = [z, z, z, z, a_low, w_up, b_alpha.reshape(1, key), gn.reshape(1, val)]
    state_spec = pl.BlockSpec((nseq, heads, dk, dv), lambda g, c: (g, 0, 0, 0))
    if h0 is not None:
        in_specs.append(state_spec)
        args.append(h0)
    n_rows = n_seq * seq_len
    return pl.pallas_call(
        functools.partial(_gla_kernel, heads=heads, seq_rows=chunk, has_h0=h0 is not None, scale=dk ** -0.5),
        grid=(n_seq // nseq, n_chunks),
        in_specs=in_specs,
        out_specs=[pl.BlockSpec((rows, val), lambda g, c: (g * n_chunks + c, 0)), state_spec],
        out_shape=[jax.ShapeDtypeStruct((n_rows, val), BF16),
                   jax.ShapeDtypeStruct((n_seq, heads, dk, dv), F32)],
        compiler_params=_params(("parallel", "arbitrary"), 48),
        name=name,
    )(*args)


CONV_LANES = 512
CONV_ROWS = 32
HIST_ROWS = 32


def _ln_silu(x, g, b):
    mu = jnp.mean(x, axis=-1, keepdims=True)
    xc = x - mu
    var = jnp.mean(xc * xc, axis=-1, keepdims=True)
    return _silu(xc * lax.rsqrt(var + EPS) * g + b)


def _conv_strip(ext_ref, w_ref, row_start, n_rows, lanes, width):
    n_lanes = lanes.stop - lanes.start
    total = None
    for r in range(SUBLANES):
        taps = [j for j in range(width) if (row_start + j) % SUBLANES == r]
        if not taps:
            continue
        rows = n_rows if r == 0 else n_rows + SUBLANES
        acc = jnp.zeros((rows, n_lanes), F32)
        for j in taps:
            start = row_start + j - r
            acc = acc + ext_ref[start:start + rows, lanes] * w_ref[j:j + 1, lanes]
        part = acc[r:r + n_rows, :]
        total = part if total is None else total + part
    return total


def _conv_prompt_kernel(u_ref, w_ref, wb_ref, g_ref, b_ref, c_ref, ext_ref, conv_ref, *, width):
    tb, cd = u_ref.shape
    first = HIST_ROWS - (width - 1)

    @pl.when(pl.program_id(1) == 0)
    def _():
        ext_ref[0:HIST_ROWS, :] = jnp.zeros((HIST_ROWS, cd), F32)

    ext_ref[HIST_ROWS:HIST_ROWS + tb, :] = u_ref[...]
    for r0 in range(0, tb, CONV_ROWS):
        for c0 in range(0, cd, CONV_LANES):
            lanes = slice(c0, c0 + CONV_LANES)
            conv_ref[r0:r0 + CONV_ROWS, lanes] = _conv_strip(ext_ref, w_ref, first + r0, CONV_ROWS, lanes, width)
    ext_ref[0:HIST_ROWS, :] = ext_ref[tb:tb + HIST_ROWS, :]
    c_ref[...] = _ln_silu(conv_ref[...] + wb_ref[...], g_ref[...], b_ref[...]).astype(c_ref.dtype)


def _conv_prompt(ug, w, wb, g, b, *, n_seq, seq_len):
    width, cd = w.shape
    assert width - 1 <= HIST_ROWS and cd % CONV_LANES == 0
    tb = _pick(seq_len, 64, CONV_ROWS)
    assert tb % CONV_ROWS == 0 and tb >= HIST_ROWS
    n_blk = seq_len // tb
    vec = pl.BlockSpec((1, cd), lambda s, t: (0, 0))
    return pl.pallas_call(
        functools.partial(_conv_prompt_kernel, width=width),
        grid=(n_seq, n_blk),
        in_specs=[pl.BlockSpec((tb, cd), lambda s, t: (s * n_blk + t, 0)),
                  pl.BlockSpec((width, cd), lambda s, t: (0, 0)), vec, vec, vec],
        out_specs=pl.BlockSpec((tb, cd), lambda s, t: (s * n_blk + t, 0)),
        out_shape=jax.ShapeDtypeStruct((n_seq * seq_len, cd), BF16),
        scratch_shapes=[pltpu.VMEM((HIST_ROWS + tb, cd), F32), pltpu.VMEM((tb, cd), F32)],
        compiler_params=_params(("parallel", "arbitrary"), 32),
        name="conv_prompt",
    )(ug, w, wb.reshape(1, cd), g.reshape(1, cd), b.reshape(1, cd))


def _conv_sample_kernel(u_ref, buf_ref, w_ref, wb_ref, g_ref, b_ref, c_ref, nbuf_ref, ext_ref, conv_ref,
                        *, width, seq_len):
    n_seq, hist, cd = buf_ref.shape
    pad_rows = ext_ref.shape[0] - hist - seq_len
    for s in range(n_seq):
        ext_ref[0:hist, :] = buf_ref[s]
        ext_ref[hist:hist + seq_len, :] = u_ref[s * seq_len:(s + 1) * seq_len, :]
        ext_ref[hist + seq_len:, :] = jnp.zeros((pad_rows, cd), F32)
        for c0 in range(0, cd, CONV_LANES):
            lanes = slice(c0, c0 + CONV_LANES)
            acc = _conv_strip(ext_ref, w_ref, 0, SUBLANES, lanes, width)
            conv_ref[s * seq_len:(s + 1) * seq_len, lanes] = acc[0:seq_len, :]
        nbuf_ref[s] = ext_ref[seq_len:seq_len + hist, :]
    c_ref[...] = _ln_silu(conv_ref[...] + wb_ref[...], g_ref[...], b_ref[...]).astype(c_ref.dtype)


def _conv_sample(ug, buf, w, wb, g, b, *, row0, seq_len):
    width, cd = w.shape
    n_seq, hist, _ = buf.shape
    assert hist == width - 1 and seq_len <= SUBLANES and cd % CONV_LANES == 0
    sb = _pick(n_seq, 8, 1)
    rows = sb * seq_len
    assert rows % 16 == 0 and row0 % rows == 0
    blk0 = row0 // rows
    ext_rows = -(-(hist + SUBLANES) // SUBLANES) * SUBLANES
    vec = pl.BlockSpec((1, cd), lambda i: (0, 0))
    return pl.pallas_call(
        functools.partial(_conv_sample_kernel, width=width, seq_len=seq_len),
        grid=(n_seq // sb,),
        in_specs=[pl.BlockSpec((rows, cd), lambda i: (blk0 + i, 0)),
                  pl.BlockSpec((sb, hist, cd), lambda i: (i, 0, 0)),
                  pl.BlockSpec((width, cd), lambda i: (0, 0)), vec, vec, vec],
        out_specs=[pl.BlockSpec((rows, cd), lambda i: (i, 0)),
                   pl.BlockSpec((sb, hist, cd), lambda i: (i, 0, 0))],
        out_shape=[jax.ShapeDtypeStruct((n_seq * seq_len, cd), BF16),
                   jax.ShapeDtypeStruct((n_seq, hist, cd), F32)],
        scratch_shapes=[pltpu.VMEM((ext_rows, cd), F32), pltpu.VMEM((rows, cd), F32)],
        compiler_params=_params(("parallel",), 32),
        name="conv_sample",
    )(ug, buf, w, wb.reshape(1, cd), g.reshape(1, cd), b.reshape(1, cd))


def _attn_kernel(q_ref, k_ref, v_ref, o_ref, *, heads, seq_rows, scale):
    rows, ca = q_ref.shape
    nseq = rows // seq_rows
    n_mem = k_ref.shape[0] // nseq
    hd = ca // heads
    q = q_ref[...]
    if nseq > 1:
        shift = seq_rows.bit_length() - 1
        row_seq = lax.broadcasted_iota(jnp.int32, (rows, 1), 0) >> shift
    for h in range(heads):
        hs = slice(h * hd, (h + 1) * hd)
        out = None
        for s in range(nseq):
            ms = slice(s * n_mem, (s + 1) * n_mem)
            sc = _dot_nt(q[:, hs], k_ref[ms, hs].astype(BF16)) * scale
            e = jnp.exp(sc - jnp.max(sc, axis=-1, keepdims=True))
            p = e / jnp.sum(e, axis=-1, keepdims=True)
            o = _dot(p.astype(BF16), v_ref[ms, hs].astype(BF16))
            out = o if out is None else jnp.where(row_seq == s, o, out)
        o_ref[:, hs] = out.astype(o_ref.dtype)


def _attn(q, k, v, *, row0, n_seq, seq_len, n_mem, heads, name):
    ca = q.shape[1]
    if seq_len % SUBLANES == 0:
        rows, nseq = _pick(seq_len, 256, 16), 1
    else:
        assert SUBLANES % seq_len == 0
        rows, nseq = 16, 16 // seq_len
    assert row0 % rows == 0 and (n_seq * seq_len) % rows == 0
    blk0 = row0 // rows
    per_seq = max(seq_len // rows, 1)
    kv_spec = pl.BlockSpec((nseq * n_mem, ca), lambda i: (i // per_seq, 0))
    return pl.pallas_call(
        functools.partial(_attn_kernel, heads=heads, seq_rows=seq_len if nseq > 1 else rows,
                          scale=(ca // heads) ** -0.5),
        grid=(n_seq * seq_len // rows,),
        in_specs=[pl.BlockSpec((rows, ca), lambda i: (blk0 + i, 0)), kv_spec, kv_spec],
        out_specs=pl.BlockSpec((rows, ca), lambda i: (i, 0)),
        out_shape=jax.ShapeDtypeStruct((n_seq * seq_len, ca), BF16),
        compiler_params=_params(("parallel",), 32),
        name=name,
    )(q, k, v)


ROUTE_E0, ROUTE_E1, ROUTE_G0, ROUTE_G1 = 0, 1, 2, 3
GATHER_UNROLL = 8
HIGH_HALF = 0xFFFF0000


def _token_stride(chunks):
    return chunks + SUBLANES


def _chunk(c, t0, n, chunks):
    stride = _token_stride(chunks)
    return pl.ds(t0 * stride + c, n, stride=stride), slice(None)


def _token_copies(hbm_ref, hbm_token, buf_ref, t0, n_tokens, chunks, sem, to_hbm):
    stride = _token_stride(chunks)

    def pair(hbm_rows, buf_rows):
        return (buf_rows, hbm_rows) if to_hbm else (hbm_rows, buf_rows)

    def start():
        def body(j, carry):
            for k in range(GATHER_UNROLL):
                r = j * GATHER_UNROLL + k
                src, dst = pair(hbm_ref.at[pl.ds(hbm_token(r) * chunks, chunks), :],
                                buf_ref.at[pl.ds((t0 + r) * stride, chunks), :])
                pltpu.make_async_copy(src, dst, sem).start()
            return carry

        lax.fori_loop(0, n_tokens // GATHER_UNROLL, body, 0)

    def wait():
        src, dst = pair(hbm_ref.at[pl.ds(0, n_tokens * chunks), :],
                        buf_ref.at[pl.ds(t0 * stride, n_tokens * chunks), :])
        pltpu.make_async_copy(src, dst, sem).wait()

    return start, wait


def _router_kernel(x_ref, g_ref, w_ref, b_ref, o_ref, htok_hbm, tokbuf, sem, *, n_groups, per_group):
    tm, d = x_ref.shape
    words = d // (2 * LANES)
    h = _rmsnorm_rows(x_ref[...], g_ref[...])
    bits = pltpu.bitcast(h.astype(BF16).astype(F32), jnp.uint32)
    for c in range(words):
        low = bits[:, 2 * c * LANES:(2 * c + 1) * LANES]
        high = bits[:, (2 * c + 1) * LANES:(2 * c + 2) * LANES]
        tokbuf[_chunk(c, 0, tm, words)] = (low >> 16) | (high & jnp.uint32(HIGH_HALF))
    row0 = pl.program_id(0) * tm
    start_tok_copy, wait_tok_copy = _token_copies(htok_hbm, lambda r: row0 + r, tokbuf, 0, tm, words, sem,
                                                  to_hbm=True)
    start_tok_copy()
    h_hi, h_lo = _split_bf16(h, 2)

    w_hi, w_lo = _split_bf16(w_ref[...], 2)
    logits = _dot(h_hi, w_hi) + (_dot(h_hi, w_lo) + _dot(h_lo, w_hi)) + b_ref[...]
    n_experts = n_groups * per_group
    lane_i = lax.broadcasted_iota(jnp.int32, logits.shape, 1)
    lane = lane_i.astype(F32)
    neg = jnp.float32(-jnp.inf)

    def first_argmax(x, m):
        return jnp.min(jnp.where(x == m, lane, float(LANES)), axis=-1, keepdims=True)

    lg = jnp.where(jnp.logical_and(lane_i >= n_experts, lane_i < n_experts + n_groups), logits, neg)
    mg = jnp.max(lg, axis=-1, keepdims=True)
    pg_top = 1.0 / jnp.sum(jnp.exp(lg - mg), axis=-1, keepdims=True)
    gsel = first_argmax(lg, mg) - float(n_experts)
    shift = per_group.bit_length() - 1
    in_group = jnp.logical_and(lane_i < n_experts, (lane_i >> shift).astype(F32) == gsel)
    le = jnp.where(in_group, logits, neg)
    m0 = jnp.max(le, axis=-1, keepdims=True)
    z = jnp.sum(jnp.exp(le - m0), axis=-1, keepdims=True)
    e0 = first_argmax(le, m0)
    le1 = jnp.where(lane == e0, neg, le)
    m1 = jnp.max(le1, axis=-1, keepdims=True)
    e1 = first_argmax(le1, m1)
    p0 = 1.0 / z
    p1 = jnp.exp(m1 - m0) / z
    den = p0 + p1
    g0 = pg_top * (p0 / den)
    g1 = pg_top * (p1 / den)
    rec = jnp.where(lane_i == ROUTE_E0, e0, 0.0)
    rec = jnp.where(lane_i == ROUTE_E1, e1, rec)
    rec = jnp.where(lane_i == ROUTE_G0, g0, rec)
    rec = jnp.where(lane_i == ROUTE_G1, g1, rec)
    o_ref[...] = rec
    wait_tok_copy()


def _router(x, g, w_group, b_group, w_expert, b_expert):
    t, d = x.shape
    n_groups, n_experts = w_group.shape[1], w_expert.shape[1]
    per_group = n_experts // n_groups
    assert n_experts + n_groups <= LANES and per_group & (per_group - 1) == 0
    pad = lambda a: jnp.pad(a, ((0, 0), (0, LANES - a.shape[1])))
    w_both = pad(jnp.concatenate([w_expert, w_group], axis=1))
    b_both = pad(jnp.concatenate([b_expert, b_group]).reshape(1, -1))
    words = d // (2 * LANES)
    tm = _pick(t, 256, GATHER_UNROLL)
    return pl.pallas_call(
        functools.partial(_router_kernel, n_groups=n_groups, per_group=per_group),
        grid=(t // tm,),
        in_specs=[pl.BlockSpec((tm, d), lambda i: (i, 0)), pl.BlockSpec((1, d), lambda i: (0, 0)),
                  pl.BlockSpec((d, LANES), lambda i: (0, 0)), pl.BlockSpec((1, LANES), lambda i: (0, 0))],
        out_specs=[pl.BlockSpec((tm, LANES), lambda i: (i, 0)), pl.BlockSpec(memory_space=pl.ANY)],
        out_shape=[jax.ShapeDtypeStruct((t, LANES), F32), jax.ShapeDtypeStruct((t * words, LANES), jnp.uint32)],
        scratch_shapes=[pltpu.VMEM((tm * _token_stride(words), LANES), jnp.uint32), pltpu.SemaphoreType.DMA(())],
        compiler_params=_params(("parallel",), 32),
        name="router",
    )(x, g.reshape(1, d), w_both, b_both)


def _dispatch_plan(e_ids, n_experts, tile):
    t, k = e_ids.shape
    n_pairs = t * k
    n_tiles = n_pairs // tile + n_experts
    flat = e_ids.reshape(n_pairs)
    onehot = (flat[:, None] == jnp.arange(n_experts, dtype=jnp.int32)[None, :]).astype(jnp.int32)
    counts = jnp.sum(onehot, axis=0)
    tiles_per = (counts + tile - 1) // tile
    tile_end = jnp.cumsum(tiles_per)
    tile_start = tile_end - tiles_per
    rank = jnp.take_along_axis(jnp.cumsum(onehot, axis=0), flat[:, None], axis=1)[:, 0] - 1
    pos = tile_start[flat] * tile + rank
    token = jnp.arange(n_pairs, dtype=jnp.int32) // k
    src = jnp.zeros((n_tiles * tile,), jnp.int32).at[pos].set(token)
    n_used = tile_end[-1]
    tile_ids = jnp.minimum(jnp.arange(n_tiles, dtype=jnp.int32), n_used - 1)
    tile_expert = jnp.searchsorted(tile_end, tile_ids, side="right").astype(jnp.int32)
    return pos.reshape(t, k), src.reshape(n_tiles, 1, tile), tile_expert, n_used.reshape(1).astype(jnp.int32)


DOWN_COLS = 1024


def _experts_kernel(texp_ref, nused_ref, idx_ref, idx_next_ref, h_hbm, wg_ref, wu_ref, wd_ref,
                    y_hbm, xbuf, ybuf, hbuf, gsem, ysem):
    i = pl.program_id(0)
    n_used = nused_ref[0]
    tile, d = hbuf.shape
    chunks = d // LANES
    words = chunks // 2
    slot = lax.rem(i, 2)
    wg_bf, wu_bf, wd_bf = wg_ref.at[0], wu_ref.at[0], wd_ref.at[0]
    start_tokens, wait_tokens = _token_copies(h_hbm, lambda r: idx_ref[0, 0, r], xbuf, slot * tile, tile, words,
                                              gsem.at[slot], to_hbm=False)
    start_next_tokens, _ = _token_copies(h_hbm, lambda r: idx_next_ref[0, 0, r], xbuf, (1 - slot) * tile, tile,
                                         words, gsem.at[1 - slot], to_hbm=False)

    def y_copy(step):
        return _token_copies(y_hbm, lambda r: step * tile + r, ybuf, 0, tile, chunks, ysem, to_hbm=True)

    @pl.when(i < n_used)
    def _():
        pl.when(i == 0)(start_tokens)
        pl.when(i + 1 < n_used)(start_next_tokens)
        wait_tokens()
        for c in range(words):
            word = xbuf[_chunk(c, slot * tile, tile, words)]
            low = pltpu.bitcast(word << 16, F32)
            high = pltpu.bitcast(word & jnp.uint32(HIGH_HALF), F32)
            hbuf[:, 2 * c * LANES:(2 * c + 1) * LANES] = low.astype(BF16)
            hbuf[:, (2 * c + 1) * LANES:(2 * c + 2) * LANES] = high.astype(BF16)
        h = hbuf[...]
        a = _dot(h, wg_bf[...])
        u = _dot(h, wu_bf[...])
        hid = (_silu(a) * u).astype(BF16)

        @pl.when(i > 0)
        def _():
            y_copy(i - 1)[1]()

        for c0 in range(0, d, DOWN_COLS):
            y = _dot(hid, wd_bf[:, c0:c0 + DOWN_COLS])
            for c in range(DOWN_COLS // LANES):
                ybuf[_chunk(c0 // LANES + c, 0, tile, chunks)] = y[:, c * LANES:(c + 1) * LANES]

    @pl.when(i >= n_used)
    def _():
        y_copy(i - 1)[1]()

        @pl.when(i == n_used)
        def _():
            ybuf[...] = jnp.zeros(ybuf.shape, F32)

    y_copy(i)[0]()

    @pl.when(i == pl.num_programs(0) - 1)
    def _():
        y_copy(i)[1]()


def _experts(h_tok, src, tile_expert, n_used, wg, wu, wd):
    d = wg.shape[1]
    chunks = d // LANES
    words = chunks // 2
    n_tiles, _, tile = src.shape
    de = wg.shape[2]
    assert d % 512 == 0 and de % 64 == 0 and d % DOWN_COLS == 0 and tile % GATHER_UNROLL == 0
    idx_spec = lambda ahead: pl.BlockSpec(
        (1, 1, tile), lambda i, texp, nused: (jnp.minimum(i + ahead, nused[0] - 1), 0, 0),
        memory_space=pltpu.SMEM)
    hbm = pl.BlockSpec(memory_space=pl.ANY)
    grid_spec = pltpu.PrefetchScalarGridSpec(
        num_scalar_prefetch=2,
        grid=(n_tiles,),
        in_specs=[idx_spec(0), idx_spec(1), hbm,
                  pl.BlockSpec((1, d, de), lambda i, texp, nused: (texp[i], 0, 0)),
                  pl.BlockSpec((1, d, de), lambda i, texp, nused: (texp[i], 0, 0)),
                  pl.BlockSpec((1, de, d), lambda i, texp, nused: (texp[i], 0, 0))],
        out_specs=hbm,
        scratch_shapes=[pltpu.VMEM((2 * tile * _token_stride(words), LANES), jnp.uint32),
                        pltpu.VMEM((tile * _token_stride(chunks), LANES), F32), pltpu.VMEM((tile, d), BF16),
                        pltpu.SemaphoreType.DMA((2,)), pltpu.SemaphoreType.DMA(())],
    )
    weights_bytes = 2 * 3 * d * de * 2
    tiles_bytes = tile * (2 * _token_stride(words) + _token_stride(chunks)) * LANES * 4
    tiles_bytes += tile * d * 2 + tile * DOWN_COLS * 4 + 3 * tile * de * 4
    return pl.pallas_call(
        _experts_kernel,
        grid_spec=grid_spec,
        out_shape=jax.ShapeDtypeStruct((n_tiles * tile * chunks, LANES), F32),
        compiler_params=_params(("arbitrary",), (weights_bytes + tiles_bytes) // MIB + 3),
        name="experts",
    )(tile_expert, n_used, src, src, h_tok, wg, wu, wd)


def _combine_kernel(idx0_ref, idx1_ref, idx0_next_ref, idx1_next_ref, y_hbm, x_ref, route_ref, gf_ref, *refs,
                    n_first, final_norm):
    outs, (buf0, buf1, res_ref, sem) = refs[:-4], refs[-4:]
    i = pl.program_id(0)
    tm, d = x_ref.shape
    chunks = d // LANES
    slot = lax.rem(i, 2)
    def gather(idx_ref, buf, k, into_slot):
        return _token_copies(y_hbm, lambda r: idx_ref[0, 0, r], buf, into_slot * tm, tm, chunks,
                             sem.at[k, into_slot], to_hbm=False)

    gathers = [gather(idx0_ref, buf0, 0, slot), gather(idx1_ref, buf1, 1, slot)]
    gathers_next = [gather(idx0_next_ref, buf0, 0, 1 - slot), gather(idx1_next_ref, buf1, 1, 1 - slot)]

    @pl.when(i == 0)
    def _():
        for start, _ in gathers:
            start()

    @pl.when(i + 1 < pl.num_programs(0))
    def _():
        for start, _ in gathers_next:
            start()

    for _, wait in gathers:
        wait()
    g0 = route_ref[:, ROUTE_G0:ROUTE_G0 + 1]
    g1 = route_ref[:, ROUTE_G1:ROUTE_G1 + 1]
    ssq = jnp.zeros((tm, LANES), F32)
    for c in range(chunks):
        lanes = slice(c * LANES, (c + 1) * LANES)
        y0 = buf0[_chunk(c, slot * tm, tm, chunks)]
        y1 = buf1[_chunk(c, slot * tm, tm, chunks)]
        r = x_ref[:, lanes] + (g0 * y0 + g1 * y1)
        res_ref[:, lanes] = r
        ssq = ssq + r * r
    if final_norm:
        scale = lax.rsqrt(jnp.sum(ssq, axis=-1, keepdims=True) * (1.0 / d) + EPS)
        result = lambda: res_ref[...] * scale * gf_ref[...]
    else:
        result = lambda: res_ref[...]
    if len(outs) == 1:
        outs[0][...] = result()
    else:
        @pl.when(i < n_first)
        def _():
            outs[0][...] = result()

        @pl.when(i >= n_first)
        def _():
            outs[1][...] = result()


def _combine(x, y_sorted, pos, route, g_final, *, split_rows, final_norm):
    t, d = x.shape
    assert y_sorted.shape[1] == LANES
    tm = _pick(t if split_rows is None else math.gcd(split_rows, t - split_rows), 128, GATHER_UNROLL)
    n_blk = t // tm
    idx = [pos[:, k].reshape(n_blk, 1, tm) for k in range(TOP_K_IN_GROUP)]
    smem = pl.BlockSpec((1, 1, tm), lambda i: (i, 0, 0), memory_space=pltpu.SMEM)
    smem_next = pl.BlockSpec((1, 1, tm), lambda i: (jnp.minimum(i + 1, n_blk - 1), 0, 0), memory_space=pltpu.SMEM)
    row = pl.BlockSpec((tm, d), lambda i: (i, 0))
    if split_rows is None:
        n_first = n_blk
        out_specs = [row]
        out_shape = [jax.ShapeDtypeStruct((t, d), F32)]
    else:
        n_first = split_rows // tm
        out_specs = [pl.BlockSpec((tm, d), lambda i: (jnp.minimum(i, n_first - 1), 0)),
                     pl.BlockSpec((tm, d), lambda i: (jnp.maximum(i - n_first, 0), 0))]
        out_shape = [jax.ShapeDtypeStruct((split_rows, d), F32), jax.ShapeDtypeStruct((t - split_rows, d), F32)]
    return pl.pallas_call(
        functools.partial(_combine_kernel, n_first=n_first, final_norm=final_norm),
        grid=(n_blk,),
        in_specs=[smem, smem, smem_next, smem_next, pl.BlockSpec(memory_space=pl.ANY), row,
                  pl.BlockSpec((tm, LANES), lambda i: (i, 0)), pl.BlockSpec((1, d), lambda i: (0, 0))],
        out_specs=out_specs,
        out_shape=out_shape,
        scratch_shapes=[pltpu.VMEM((2 * tm * _token_stride(d // LANES), LANES), F32),
                        pltpu.VMEM((2 * tm * _token_stride(d // LANES), LANES), F32), pltpu.VMEM((tm, d), F32),
                        pltpu.SemaphoreType.DMA((TOP_K_IN_GROUP, 2))],
        compiler_params=_params(("arbitrary",), 32),
        name="combine",
    )(idx[0], idx[1], idx[0], idx[1], y_sorted, x, route, g_final.reshape(1, d))


def kernel(x_prompt, x_sample, mem_prompt, state_gla, state_conv, cache_mem_k, cache_mem_v, norm_mix_g, w_in, w_alpha_up, b_alpha, gla_norm_g, w_branch_a, conv_dw_w, conv_dw_b, conv_ln_g, conv_ln_b, w_branch_b, w_out, norm_ca_g, norm_mem_g, w_ca_q, w_ca_k, w_ca_v, w_ca_o, norm_ffn_g, w_router_group, b_router_group, w_router_expert, b_router_expert, w_exp_gate, w_exp_up, w_exp_down, norm_final_g):
    depth = w_in.shape[0]
    bp, tp, d = x_prompt.shape
    bs, ts, _ = x_sample.shape
    heads, dk, dv = state_gla.shape[2:]
    key, val = heads * dk, heads * dv
    rank = w_alpha_up.shape[1]
    cd = state_conv.shape[3]
    n_mem, ca_heads, ca_hd = cache_mem_k.shape[2:]
    ca = ca_heads * ca_hd
    n_experts = w_router_expert.shape[2]
    rows_p, rows_s = bp * tp, bs * ts
    off_a = 2 * key + 2 * val
    off_u = off_a + rank
    off_g = off_u + 2 * cd
    assert rank <= LANES and w_in.shape[2] == off_g + 2 * d
    bf = lambda a: a.astype(BF16)

    x = (x_prompt.reshape(rows_p, d), x_sample.reshape(rows_s, d))
    outs = dict(gla_p=[], conv_p=[], mk_p=[], mv_p=[], gla_s=[], conv_s=[])
    for l in range(depth):
        w_in_t = jnp.swapaxes(w_in, 1, 2)[l]
        w_qkvr = _cast_weight(w_in_t, 0, off_a)
        w_a = bf(jnp.pad(w_in[l, :, off_a:off_u], ((0, 0), (0, LANES - rank))))
        h = _rmsnorm(x, norm_mix_g[l], BF16)
        z, (w_ug,) = _mm([h], [w_qkvr], [0], _ep_identity, F32, w_rows=[True], tm_pref=WIDE_ROWS,
                         side_cast=[(w_in_t, off_u, 2 * cd + 2 * d)], name="in_qkvr")
        a_low = _mm([h], [w_a], [0], _ep_identity, F32, name="in_alow")
        ug = _mm([h], [w_ug, w_ug], [0, 0], _ep_glu, F32, n=cd, w_col0=[0, cd], w_rows=[True, True],
                 tm_pref=WIDE_ROWS, name="in_glu")
        w_up = bf(jnp.pad(w_alpha_up[l], ((0, LANES - rank), (0, 0))))
        gla_args = dict(heads=heads, dk=dk, dv=dv)
        og_p, st_p = _gla(z, a_low, w_up, b_alpha[l], gla_norm_g[l], None, row0=0, n_seq=bp, seq_len=tp,
                          name="gla_prompt", **gla_args)
        og_s, st_s = _gla(z, a_low, w_up, b_alpha[l], gla_norm_g[l], state_gla[l], row0=rows_p, n_seq=bs,
                          seq_len=ts, name="gla_sample", **gla_args)
        conv_w = (conv_dw_w[l], conv_dw_b[l], conv_ln_g[l], conv_ln_b[l])
        c_p = _conv_prompt(ug, *conv_w, n_seq=bp, seq_len=tp)
        c_s, buf_s = _conv_sample(ug, state_conv[l], *conv_w, row0=rows_p, seq_len=ts)
        og = (og_p, og_s)
        c = (c_p, c_s)
        flat = lambda w: w[l].reshape(-1, w.shape[-1])
        expert_casts = [(flat(w), 0, w.shape[1] * w.shape[2]) for w in (w_exp_gate, w_exp_up, w_exp_down)]
        merged, (wg_bf, wu_bf, wd_bf) = _mm(
            [og, c, h], [w_ug, w_ug, bf(w_branch_a[l]), bf(w_branch_b[l])], [2, 2, 0, 1], _ep_gated_merge, BF16,
            n=d, w_col0=[2 * cd, 2 * cd + d, 0, 0], w_rows=[True, True, False, False], tn_pref=256,
            side_cast=expert_casts, name="merge")
        x = _mm([merged], [bf(w_out[l])], [0], _ep_residual, F32, extras=[x], name="out_proj")
        q = _mm([x], [bf(w_ca_q[l])], [0], _ep_identity, BF16, norm_gain=norm_ca_g[l], name="ca_q")
        m = _rmsnorm(mem_prompt.reshape(bp * n_mem, d), norm_mem_g[l], BF16)
        mk = _mm([m], [bf(w_ca_k[l])], [0], _ep_identity, F32, name="mem_k")
        mv = _mm([m], [bf(w_ca_v[l])], [0], _ep_identity, F32, name="mem_v")
        ao_p = _attn(q, mk, mv, row0=0, n_seq=bp, seq_len=tp, n_mem=n_mem, heads=ca_heads, name="attn_prompt")
        ao_s = _attn(q, cache_mem_k[l].reshape(bs * n_mem, ca), cache_mem_v[l].reshape(bs * n_mem, ca),
                     row0=rows_p, n_seq=bs, seq_len=ts, n_mem=n_mem, heads=ca_heads, name="attn_sample")
        ao = (ao_p, ao_s)
        x = _mm([ao], [bf(w_ca_o[l])], [0], _ep_residual, F32, extras=[x], name="ca_out")
        route, h_tok = _router(x, norm_ffn_g[l], w_router_group[l], b_router_group[l], w_router_expert[l],
                               b_router_expert[l])
        e_ids = route[:, ROUTE_E0:ROUTE_E1 + 1].astype(jnp.int32)
        tile = _pick(rows_p + rows_s, 256, SUBLANES)
        pos, src, tile_expert, n_used = _dispatch_plan(e_ids, n_experts, tile)
        y_sorted = _experts(h_tok, src, tile_expert, n_used, wg_bf.reshape(w_exp_gate.shape[1:]),
                            wu_bf.reshape(w_exp_up.shape[1:]), wd_bf.reshape(w_exp_down.shape[1:]))
        last = l == depth - 1
        res = _combine(x, y_sorted, pos, route, norm_final_g, split_rows=rows_p if last else None,
                       final_norm=last)
        if not last:
            x = res[0]
        outs["gla_p"].append(st_p)
        hist = conv_dw_w.shape[1] - 1
        outs["conv_p"].append(jnp.stack([ug[(b + 1) * tp - hist:(b + 1) * tp] for b in range(bp)]))
        outs["mk_p"].append(mk.reshape(bp, n_mem, ca_heads, ca_hd))
        outs["mv_p"].append(mv.reshape(bp, n_mem, ca_heads, ca_hd))
        outs["gla_s"].append(st_s)
        outs["conv_s"].append(buf_s)
    y_prompt = res[0].reshape(bp, tp, d)
    y_sample = res[1].reshape(bs, ts, d)
    return (y_prompt, y_sample, jnp.stack(outs["gla_p"]), jnp.stack(outs["conv_p"]), jnp.stack(outs["mk_p"]),
            jnp.stack(outs["mv_p"]), jnp.stack(outs["gla_s"]), jnp.stack(outs["conv_s"]))
```

```python
import functools
import math

import jax
import jax.numpy as jnp
from jax import lax
from jax.experimental import pallas as pl
from jax.experimental.pallas import tpu as pltpu

F32 = jnp.float32
BF16 = jnp.bfloat16

EPS = 1e-6
GLA_TAU = 16.0
GLA_CHUNK = 64
TOP_K_IN_GROUP = 2
LANES = 128
SUBLANES = 8
MIB = 1 << 20


def _pick(n, pref, mult):
    for d in range(min(pref, n), 0, -1):
        if n % d == 0 and d % mult == 0:
            return d
    return n


def _params(semantics, vmem_mib):
    return pltpu.CompilerParams(dimension_semantics=semantics, vmem_limit_bytes=vmem_mib * MIB)


def _sigmoid(x):
    return 1.0 / (1.0 + jnp.exp(-x))


def _silu(x):
    return x * _sigmoid(x)


def _dot(a, b):
    return jnp.dot(a, b, preferred_element_type=F32)


def _dot_nt(a, b):
    return lax.dot_general(a, b, (((1,), (1,)), ((), ())), preferred_element_type=F32)


def _dot_tn(a, b):
    return lax.dot_general(a, b, (((0,), (0,)), ((), ())), preferred_element_type=F32)


def _split_bf16(x, parts):
    out = []
    for _ in range(parts - 1):
        p = x.astype(BF16)
        out.append(p)
        x = x - p.astype(F32)
    out.append(x.astype(BF16))
    return out


def _rmsnorm_rows(x, g):
    return x * lax.rsqrt(jnp.mean(x * x, axis=-1, keepdims=True) + EPS) * g


def _parts(a):
    return a if isinstance(a, tuple) else (a,)


def _rows_of(a):
    return sum(p.shape[0] for p in _parts(a))


def _part_specs(a, tm, cols, col_of):
    parts = _parts(a)
    if len(parts) == 1:
        return [pl.BlockSpec((tm, cols), lambda i, *j: (i, col_of(*j)))]
    n_first = parts[0].shape[0] // tm
    assert parts[0].shape[0] % tm == 0 and parts[1].shape[0] % tm == 0
    return [pl.BlockSpec((tm, cols), lambda i, *j: (jnp.minimum(i, n_first - 1), col_of(*j))),
            pl.BlockSpec((tm, cols), lambda i, *j: (jnp.maximum(i - n_first, 0), col_of(*j)))]


def _part_tile(a, pref, mult):
    return _pick(math.gcd(*[p.shape[0] for p in _parts(a)]), pref, mult)


def _select_part(refs, n_first):
    if len(refs) == 1:
        return refs[0][...]
    return jnp.where(pl.program_id(0) < n_first, refs[0][...], refs[1][...])


def _rmsnorm_kernel(*refs, n_first, n_out):
    x_refs, g_ref, o_refs = refs[:-1 - n_out], refs[-1 - n_out], refs[-n_out:]
    x = _select_part(x_refs, n_first)
    o_refs[0][...] = _rmsnorm_rows(x, g_ref[...]).astype(o_refs[0].dtype)
    if n_out == 2:
        o_refs[1][...] = x


def _rmsnorm(x, g, out_dtype, stacked_copy=False):
    t, d = _rows_of(x), _parts(x)[0].shape[1]
    tm = _part_tile(x, 128, 16)
    row = pl.BlockSpec((tm, d), lambda i: (i, 0))
    n_out = 2 if stacked_copy else 1
    out = pl.pallas_call(
        functools.partial(_rmsnorm_kernel, n_first=_parts(x)[0].shape[0] // tm, n_out=n_out),
        grid=(t // tm,),
        in_specs=_part_specs(x, tm, d, lambda: 0) + [pl.BlockSpec((1, d), lambda i: (0, 0))],
        out_specs=[row] * n_out,
        out_shape=[jax.ShapeDtypeStruct((t, d), out_dtype), jax.ShapeDtypeStruct((t, d), _parts(x)[0].dtype)][:n_out],
        compiler_params=_params(("parallel",), 32),
        name="rmsnorm",
    )(*_parts(x), g.reshape(1, d))
    return out if stacked_copy else out[0]


def _cast_weight_kernel(w_hbm, o_ref, buf, sem, *, row0):
    j = pl.program_id(0)
    tn = o_ref.shape[0]
    slot = lax.rem(j, 2)

    def copy(block, into):
        rows = pl.ds(pl.multiple_of(row0 + block * tn, SUBLANES), tn)
        return pltpu.make_async_copy(w_hbm.at[rows, :], buf.at[into], sem.at[into])

    @pl.when(j == 0)
    def _():
        copy(j, slot).start()

    @pl.when(j + 1 < pl.num_programs(0))
    def _():
        copy(j + 1, 1 - slot).start()

    copy(j, slot).wait()
    o_ref[...] = buf[slot].astype(BF16)


def _cast_weight(w, row0, n_rows):
    _, k = w.shape
    tn = _pick(n_rows, 256, 16)
    assert row0 % SUBLANES == 0 and n_rows % tn == 0
    return pl.pallas_call(
        functools.partial(_cast_weight_kernel, row0=row0),
        grid=(n_rows // tn,),
        in_specs=[pl.BlockSpec(memory_space=pl.ANY)],
        out_specs=pl.BlockSpec((tn, k), lambda j: (j, 0)),
        out_shape=jax.ShapeDtypeStruct((n_rows, k), BF16),
        scratch_shapes=[pltpu.VMEM((2, tn, k), F32), pltpu.SemaphoreType.DMA((2,))],
        compiler_params=_params(("arbitrary",), 32),
        name="cast_weight",
    )(w)


def _ep_identity(accs, extras):
    return accs[0]


def _ep_glu(accs, extras):
    return accs[0] * _sigmoid(accs[1])


def _ep_residual(accs, extras):
    return extras[0] + accs[0]


def _ep_gated_merge(accs, extras):
    return _sigmoid(accs[0]) * accs[2] + _sigmoid(accs[1]) * accs[3]


WIDE_ROWS = 1088


def _side_cast_step(step, n_chunks, src_hbm, dst_hbm, row0, in_buf, out_buf, in_sem, out_sem):
    rows = in_buf.shape[1]
    slot = lax.rem(step, 2)

    def read(chunk, into):
        start = pl.multiple_of(row0 + chunk * rows, SUBLANES)
        return pltpu.make_async_copy(src_hbm.at[pl.ds(start, rows), :], in_buf.at[into], in_sem.at[into])

    def write(chunk):
        start = pl.multiple_of(chunk * rows, 16)
        return pltpu.make_async_copy(out_buf, dst_hbm.at[pl.ds(start, rows), :], out_sem)

    @pl.when(step < n_chunks)
    def _():
        pl.when(step == 0)(lambda: read(step, slot).start())
        pl.when(step + 1 < n_chunks)(lambda: read(step + 1, 1 - slot).start())
        read(step, slot).wait()
        pl.when(step > 0)(lambda: write(step - 1).wait())
        out_buf[...] = in_buf[slot].astype(BF16)
        write(step).start()
        pl.when(step == n_chunks - 1)(lambda: write(step).wait())


def _mm_kernel(*refs, x_of_w, w_rows, x_parts, e_parts, n_first, has_norm, side, epilogue):
    refs = list(refs)
    take = lambda k: [refs.pop(0) for _ in range(k)]
    x_refs = [take(p) for p in x_parts]
    g_ref = refs.pop(0) if has_norm else None
    w_refs = take(len(x_of_w))
    e_refs = [take(p) for p in e_parts]
    side_src = take(len(side))
    o_ref = refs.pop(0)
    side_dst = take(len(side))
    i = pl.program_id(0)
    step = i * pl.num_programs(1) + pl.program_id(1)
    side_scratch = [refs[len(refs) - 4 * len(side) + 4 * k:len(refs) - 4 * len(side) + 4 * k + 4]
                    for k in range(len(side))]
    for (row0, n_chunks), src, dst, (in_buf, out_buf, in_sem, out_sem) in zip(side, side_src, side_dst, side_scratch):
        _side_cast_step(step, n_chunks, src, dst, row0, in_buf, out_buf, in_sem, out_sem)
    first_col = pl.program_id(1) == 0
    xs = []
    for k, parts in enumerate(x_refs):
        norm = has_norm and k == 0
        if len(parts) == 1 and not norm:
            xs.append(parts[0])
            continue
        scratch = refs.pop(0)

        def fill(src, scratch=scratch, norm=norm):
            val = src[...]
            if norm:
                val = _rmsnorm_rows(val, g_ref[...])
            scratch[...] = val.astype(scratch.dtype)

        if len(parts) == 1:
            pl.when(first_col)(functools.partial(fill, parts[0]))
        else:
            pl.when(jnp.logical_and(first_col, i < n_first))(functools.partial(fill, parts[0]))
            pl.when(jnp.logical_and(first_col, i >= n_first))(functools.partial(fill, parts[1]))
        xs.append(scratch)
    accs = [(_dot_nt if w_rows[k] else _dot)(xs[x_of_w[k]][...], w_refs[k][...]) for k in range(len(x_of_w))]
    o_ref[...] = epilogue(accs, [_select_part(e, n_first) for e in e_refs]).astype(o_ref.dtype)


def _mm(xs, ws, x_of_w, epilogue, out_dtype, extras=(), tm_pref=512, tn_pref=512, name="mm", n=None, w_col0=None,
        w_rows=None, norm_gain=None, side_cast=()):
    t = _rows_of(xs[0])
    w_rows = [False] * len(ws) if w_rows is None else w_rows
    w_col0 = [0] * len(ws) if w_col0 is None else w_col0
    n = ws[0].shape[0 if w_rows[0] else 1] if n is None else n
    tm = min(_part_tile(a, tm_pref, 16) for a in list(xs) + list(extras))
    tn = _pick(math.gcd(n, *w_col0), tn_pref, LANES)
    split = [a for a in list(xs) + list(extras) if len(_parts(a)) == 2]
    assert all(a[0].shape[0] == split[0][0].shape[0] and a[0].shape[0] % tm == 0 for a in split)
    n_first = split[0][0].shape[0] // tm if split else 0
    k_of = lambda x: _parts(x)[0].shape[1]
    in_specs, args = [], []
    for x in xs:
        in_specs += _part_specs(x, tm, k_of(x), lambda j: 0)
        args += _parts(x)
    if norm_gain is not None:
        in_specs.append(pl.BlockSpec((1, k_of(xs[0])), lambda i, j: (0, 0)))
        args.append(norm_gain.reshape(1, -1))
    for w, c0, rows in zip(ws, w_col0, w_rows):
        if rows:
            in_specs.append(pl.BlockSpec((tn, w.shape[1]), lambda i, j, b0=c0 // tn: (b0 + j, 0)))
        else:
            in_specs.append(pl.BlockSpec((w.shape[0], tn), lambda i, j, b0=c0 // tn: (0, b0 + j)))
    args += ws
    for e in extras:
        in_specs += _part_specs(e, tm, tn, lambda j: j)
        args += _parts(e)
    staged = [x for k, x in enumerate(xs) if len(_parts(x)) == 2 or (norm_gain is not None and k == 0)]
    scratch = [pltpu.VMEM((tm, k_of(x)), BF16) for x in staged]
    steps = (t // tm) * (n // tn)
    side, side_out_shape, side_bytes = [], [], 0
    for src, row0, n_rows in side_cast:
        rows = next(r for r in range(16, n_rows + 1, 16) if n_rows % r == 0 and n_rows // r <= steps)
        assert row0 % SUBLANES == 0
        side.append((row0, n_rows // rows))
        in_specs.append(pl.BlockSpec(memory_space=pl.ANY))
        args.append(src)
        side_out_shape.append(jax.ShapeDtypeStruct((n_rows, src.shape[1]), BF16))
        scratch += [pltpu.VMEM((2, rows, src.shape[1]), F32), pltpu.VMEM((rows, src.shape[1]), BF16),
                    pltpu.SemaphoreType.DMA((2,)), pltpu.SemaphoreType.DMA(())]
        side_bytes += rows * src.shape[1] * (2 * 4 + 2)
    block_bytes = sum(tm * p.shape[1] * p.dtype.itemsize for x in xs for p in _parts(x))
    block_bytes += sum(w.size // (w.shape[0 if rows else 1] // tn) * w.dtype.itemsize for w, rows in zip(ws, w_rows))
    block_bytes += sum(tm * tn * p.dtype.itemsize for e in extras for p in _parts(e))
    block_bytes += tm * tn * jnp.dtype(out_dtype).itemsize
    scratch_bytes = sum(tm * k_of(x) * 2 for x in staged)
    norm_bytes = 2 * tm * k_of(xs[0]) * 4 if norm_gain is not None else 0
    acc_bytes = len(ws) * tm * tn * 4
    vmem = (2 * block_bytes + 2 * acc_bytes + scratch_bytes + norm_bytes + side_bytes) // MIB + 4
    outs = pl.pallas_call(
        functools.partial(_mm_kernel, x_of_w=tuple(x_of_w), w_rows=tuple(w_rows),
                          x_parts=tuple(len(_parts(x)) for x in xs), e_parts=tuple(len(_parts(e)) for e in extras),
                          n_first=n_first, has_norm=norm_gain is not None, side=tuple(side), epilogue=epilogue),
        grid=(t // tm, n // tn),
        in_specs=in_specs,
        out_specs=[pl.BlockSpec((tm, tn), lambda i, j: (i, j))] + [pl.BlockSpec(memory_space=pl.ANY)] * len(side),
        out_shape=[jax.ShapeDtypeStruct((t, n), out_dtype)] + side_out_shape,
        scratch_shapes=scratch,
        compiler_params=_params(("arbitrary", "arbitrary") if side else ("parallel", "arbitrary"), vmem),
        name=name,
    )(*args)
    return (outs[0], outs[1:]) if side else outs[0]


def _log_sigmoid(x):
    return jnp.minimum(x, 0.0) - jnp.log1p(jnp.exp(-jnp.abs(x)))


def _gla_kernel(*refs, heads, seq_rows, has_h0, scale):
    if has_h0:
        q_ref, k_ref, v_ref, r_ref, a_ref, wup_ref, bal_ref, gn_ref, h0_ref, og_ref, st_ref = refs
    else:
        q_ref, k_ref, v_ref, r_ref, a_ref, wup_ref, bal_ref, gn_ref, og_ref, st_ref = refs
    rows, key = q_ref.shape
    val = v_ref.shape[1]
    dk, dv = key // heads, val // heads
    nseq = rows // seq_rows

    @pl.when(pl.program_id(1) == 0)
    def _():
        if has_h0:
            st_ref[...] = h0_ref[...]
        else:
            st_ref[...] = jnp.zeros(st_ref.shape, F32)

    la = _dot(a_ref[...].astype(BF16), wup_ref[...]) + bal_ref[...]
    log_a = _log_sigmoid(la) * (1.0 / GLA_TAU)

    ri = lax.broadcasted_iota(jnp.int32, (rows, rows), 0)
    ci = lax.broadcasted_iota(jnp.int32, (rows, rows), 1)
    causal = ci <= ri
    if nseq > 1:
        shift = seq_rows.bit_length() - 1
        causal = jnp.logical_and(causal, (ri >> shift) == (ci >> shift))
        row_seq = lax.broadcasted_iota(jnp.int32, (rows, 1), 0) >> shift
    tri = jnp.where(causal, 1.0, 0.0).astype(BF16)
    b = sum(_dot(tri, part) for part in _split_bf16(log_a, 3))

    q = q_ref[...] * scale
    k = k_ref[...]
    v = v_ref[...]
    r = r_ref[...]
    gn = gn_ref[...]
    for h in range(heads):
        ks = slice(h * dk, (h + 1) * dk)
        vs = slice(h * dv, (h + 1) * dv)
        bh = b[:, ks]
        q_e = (q[:, ks] * jnp.exp(bh)).astype(BF16)
        k_e = (k[:, ks] * jnp.exp(-bh)).astype(BF16)
        vh = v[:, vs].astype(BF16)
        scores = jnp.where(causal, _dot_nt(q_e, k_e), 0.0)
        o = _dot(scores.astype(BF16), vh)
        for s in range(nseq):
            b_last = bh[(s + 1) * seq_rows - 1:(s + 1) * seq_rows, :]
            k_s = k[:, ks] * jnp.exp(b_last - bh)
            if nseq > 1:
                in_seq = row_seq == s
                k_s = jnp.where(in_seq, k_s, 0.0)
            state = st_ref[s, h]
            o_inter = _dot(q_e, state.astype(BF16))
            o = o + (jnp.where(in_seq, o_inter, 0.0) if nseq > 1 else o_inter)
            d_state = _dot_tn(k_s.astype(BF16), vh)
            decay_col = jnp.broadcast_to(jnp.exp(b_last), (LANES, dk)).T[:, 0:1]
            st_ref[s, h] = decay_col * state + d_state
        mu = jnp.mean(o, axis=-1, keepdims=True)
        oc = o - mu
        o_n = oc * lax.rsqrt(jnp.mean(oc * oc, axis=-1, keepdims=True) + EPS) * gn[:, vs]
        og_ref[:, vs] = (o_n * _silu(r[:, vs])).astype(og_ref.dtype)


def _gla(z, a_low, w_up, b_alpha, gn, h0, *, row0, n_seq, seq_len, heads, dk, dv, name):
    key, val = heads * dk, heads * dv
    assert val == 2 * key and z.shape[1] == 2 * key + 2 * val
    chunk = min(GLA_CHUNK, seq_len)
    assert seq_len % chunk == 0
    if chunk % SUBLANES == 0:
        rows, n_chunks = chunk, seq_len // chunk
    else:
        assert SUBLANES % seq_len == 0 and chunk == seq_len
        rows, n_chunks = SUBLANES, 1
    nseq = rows // chunk
    assert n_seq % nseq == 0 and row0 % rows == 0
    blk0 = row0 // rows
    row_map = lambda g, c: blk0 + g * n_chunks + c
    in_specs = [
        pl.BlockSpec((rows, key), lambda g, c: (row_map(g, c), 0)),
        pl.BlockSpec((rows, key), lambda g, c: (row_map(g, c), 1)),
        pl.BlockSpec((rows, val), lambda g, c: (row_map(g, c), 1)),
        pl.BlockSpec((rows, val), lambda g, c: (row_map(g, c), 2)),
        pl.BlockSpec((rows, LANES), lambda g, c: (row_map(g, c), 0)),
        pl.BlockSpec((LANES, key), lambda g, c: (0, 0)),
        pl.BlockSpec((1, key), lambda g, c: (0, 0)),
        pl.BlockSpec((1, val), lambda g, c: (0, 0)),
    ]
    args = [z, z, z, z, a_low, w_up, b_alpha.reshape(1, key), gn.reshape(1, val)]
    state_spec = pl.BlockSpec((nseq, heads, dk, dv), lambda g, c: (g, 0, 0, 0))
    if h0 is not None:
        in_specs.append(state_spec)
        args.append(h0)
    n_rows = n_seq * seq_len
    return pl.pallas_call(
        functools.partial(_gla_kernel, heads=heads, seq_rows=chunk, has_h0=h0 is not None, scale=dk ** -0.5),
        grid=(n_seq // nseq, n_chunks),
        in_specs=in_specs,
        out_specs=[pl.BlockSpec((rows, val), lambda g, c: (g * n_chunks + c, 0)), state_spec],
        out_shape=[jax.ShapeDtypeStruct((n_rows, val), BF16),
                   jax.ShapeDtypeStruct((n_seq, heads, dk, dv), F32)],
        compiler_params=_params(("parallel", "arbitrary"), 48),
        name=name,
    )(*args)


CONV_LANES = 512
CONV_ROWS = 32
HIST_ROWS = 32


def _ln_silu(x, g, b):
    mu = jnp.mean(x, axis=-1, keepdims=True)
    xc = x - mu
    var = jnp.mean(xc * xc, axis=-1, keepdims=True)
    return _silu(xc * lax.rsqrt(var + EPS) * g + b)


def _conv_strip(ext_ref, w_ref, row_start, n_rows, lanes, width):
    n_lanes = lanes.stop - lanes.start
    total = None
    for r in range(SUBLANES):
        taps = [j for j in range(width) if (row_start + j) % SUBLANES == r]
        if not taps:
            continue
        rows = n_rows if r == 0 else n_rows + SUBLANES
        acc = jnp.zeros((rows, n_lanes), F32)
        for j in taps:
            start = row_start + j - r
            acc = acc + ext_ref[start:start + rows, lanes] * w_ref[j:j + 1, lanes]
        part = acc[r:r + n_rows, :]
        total = part if total is None else total + part
    return total


def _conv_prompt_kernel(u_ref, w_ref, wb_ref, g_ref, b_ref, c_ref, ext_ref, conv_ref, *, width):
    tb, cd = u_ref.shape
    first = HIST_ROWS - (width - 1)

    @pl.when(pl.program_id(1) == 0)
    def _():
        ext_ref[0:HIST_ROWS, :] = jnp.zeros((HIST_ROWS, cd), F32)

    ext_ref[HIST_ROWS:HIST_ROWS + tb, :] = u_ref[...]
    for r0 in range(0, tb, CONV_ROWS):
        for c0 in range(0, cd, CONV_LANES):
            lanes = slice(c0, c0 + CONV_LANES)
            conv_ref[r0:r0 + CONV_ROWS, lanes] = _conv_strip(ext_ref, w_ref, first + r0, CONV_ROWS, lanes, width)
    ext_ref[0:HIST_ROWS, :] = ext_ref[tb:tb + HIST_ROWS, :]
    c_ref[...] = _ln_silu(conv_ref[...] + wb_ref[...], g_ref[...], b_ref[...]).astype(c_ref.dtype)


def _conv_prompt(ug, w, wb, g, b, *, n_seq, seq_len):
    width, cd = w.shape
    assert width - 1 <= HIST_ROWS and cd % CONV_LANES == 0
    tb = _pick(seq_len, 64, CONV_ROWS)
    assert tb % CONV_ROWS == 0 and tb >= HIST_ROWS
    n_blk = seq_len // tb
    vec = pl.BlockSpec((1, cd), lambda s, t: (0, 0))
    return pl.pallas_call(
        functools.partial(_conv_prompt_kernel, width=width),
        grid=(n_seq, n_blk),
        in_specs=[pl.BlockSpec((tb, cd), lambda s, t: (s * n_blk + t, 0)),
                  pl.BlockSpec((width, cd), lambda s, t: (0, 0)), vec, vec, vec],
        out_specs=pl.BlockSpec((tb, cd), lambda s, t: (s * n_blk + t, 0)),
        out_shape=jax.ShapeDtypeStruct((n_seq * seq_len, cd), BF16),
        scratch_shapes=[pltpu.VMEM((HIST_ROWS + tb, cd), F32), pltpu.VMEM((tb, cd), F32)],
        compiler_params=_params(("parallel", "arbitrary"), 32),
        name="conv_prompt",
    )(ug, w, wb.reshape(1, cd), g.reshape(1, cd), b.reshape(1, cd))


def _conv_sample_kernel(u_ref, buf_ref, w_ref, wb_ref, g_ref, b_ref, c_ref, nbuf_ref, ext_ref, conv_ref,
                        *, width, seq_len):
    n_seq, hist, cd = buf_ref.shape
    pad_rows = ext_ref.shape[0] - hist - seq_len
    for s in range(n_seq):
        ext_ref[0:hist, :] = buf_ref[s]
        ext_ref[hist:hist + seq_len, :] = u_ref[s * seq_len:(s + 1) * seq_len, :]
        ext_ref[hist + seq_len:, :] = jnp.zeros((pad_rows, cd), F32)
        for c0 in range(0, cd, CONV_LANES):
            lanes = slice(c0, c0 + CONV_LANES)
            acc = _conv_strip(ext_ref, w_ref, 0, SUBLANES, lanes, width)
            conv_ref[s * seq_len:(s + 1) * seq_len, lanes] = acc[0:seq_len, :]
        nbuf_ref[s] = ext_ref[seq_len:seq_len + hist, :]
    c_ref[...] = _ln_silu(conv_ref[...] + wb_ref[...], g_ref[...], b_ref[...]).astype(c_ref.dtype)


def _conv_sample(ug, buf, w, wb, g, b, *, row0, seq_len):
    width, cd = w.shape
    n_seq, hist, _ = buf.shape
    assert hist == width - 1 and seq_len <= SUBLANES and cd % CONV_LANES == 0
    sb = _pick(n_seq, 8, 1)
    rows = sb * seq_len
    assert rows % 16 == 0 and row0 % rows == 0
    blk0 = row0 // rows
    ext_rows = -(-(hist + SUBLANES) // SUBLANES) * SUBLANES
    vec = pl.BlockSpec((1, cd), lambda i: (0, 0))
    return pl.pallas_call(
        functools.partial(_conv_sample_kernel, width=width, seq_len=seq_len),
        grid=(n_seq // sb,),
        in_specs=[pl.BlockSpec((rows, cd), lambda i: (blk0 + i, 0)),
                  pl.BlockSpec((sb, hist, cd), lambda i: (i, 0, 0)),
                  pl.BlockSpec((width, cd), lambda i: (0, 0)), vec, vec, vec],
        out_specs=[pl.BlockSpec((rows, cd), lambda i: (i, 0)),
                   pl.BlockSpec((sb, hist, cd), lambda i: (i, 0, 0))],
        out_shape=[jax.ShapeDtypeStruct((n_seq * seq_len, cd), BF16),
                   jax.ShapeDtypeStruct((n_seq, hist, cd), F32)],
        scratch_shapes=[pltpu.VMEM((ext_rows, cd), F32), pltpu.VMEM((rows, cd), F32)],
        compiler_params=_params(("parallel",), 32),
        name="conv_sample",
    )(ug, buf, w, wb.reshape(1, cd), g.reshape(1, cd), b.reshape(1, cd))


def _attn_kernel(q_ref, k_ref, v_ref, o_ref, *, heads, seq_rows, scale):
    rows, ca = q_ref.shape
    nseq = rows // seq_rows
    n_mem = k_ref.shape[0] // nseq
    hd = ca // heads
    q = q_ref[...]
    if nseq > 1:
        shift = seq_rows.bit_length() - 1
        row_seq = lax.broadcasted_iota(jnp.int32, (rows, 1), 0) >> shift
    for h in range(heads):
        hs = slice(h * hd, (h + 1) * hd)
        out = None
        for s in range(nseq):
            ms = slice(s * n_mem, (s + 1) * n_mem)
            sc = _dot_nt(q[:, hs], k_ref[ms, hs].astype(BF16)) * scale
            e = jnp.exp(sc - jnp.max(sc, axis=-1, keepdims=True))
            p = e / jnp.sum(e, axis=-1, keepdims=True)
            o = _dot(p.astype(BF16), v_ref[ms, hs].astype(BF16))
            out = o if out is None else jnp.where(row_seq == s, o, out)
        o_ref[:, hs] = out.astype(o_ref.dtype)


def _attn(q, k, v, *, row0, n_seq, seq_len, n_mem, heads, name):
    ca = q.shape[1]
    if seq_len % SUBLANES == 0:
        rows, nseq = _pick(seq_len, 256, 16), 1
    else:
        assert SUBLANES % seq_len == 0
        rows, nseq = 16, 16 // seq_len
    assert row0 % rows == 0 and (n_seq * seq_len) % rows == 0
    blk0 = row0 // rows
    per_seq = max(seq_len // rows, 1)
    kv_spec = pl.BlockSpec((nseq * n_mem, ca), lambda i: (i // per_seq, 0))
    return pl.pallas_call(
        functools.partial(_attn_kernel, heads=heads, seq_rows=seq_len if nseq > 1 else rows,
                          scale=(ca // heads) ** -0.5),
        grid=(n_seq * seq_len // rows,),
        in_specs=[pl.BlockSpec((rows, ca), lambda i: (blk0 + i, 0)), kv_spec, kv_spec],
        out_specs=pl.BlockSpec((rows, ca), lambda i: (i, 0)),
        out_shape=jax.ShapeDtypeStruct((n_seq * seq_len, ca), BF16),
        compiler_params=_params(("parallel",), 32),
        name=name,
    )(q, k, v)


ROUTE_E0, ROUTE_E1, ROUTE_G0, ROUTE_G1 = 0, 1, 2, 3
GATHER_UNROLL = 8
HIGH_HALF = 0xFFFF0000


def _token_stride(chunks):
    return chunks + SUBLANES


def _chunk(c, t0, n, chunks):
    stride = _token_stride(chunks)
    return pl.ds(t0 * stride + c, n, stride=stride), slice(None)


def _token_copies(hbm_ref, hbm_token, buf_ref, t0, n_tokens, chunks, sem, to_hbm):
    stride = _token_stride(chunks)

    def pair(hbm_rows, buf_rows):
        return (buf_rows, hbm_rows) if to_hbm else (hbm_rows, buf_rows)

    def start():
        def body(j, carry):
            for k in range(GATHER_UNROLL):
                r = j * GATHER_UNROLL + k
                src, dst = pair(hbm_ref.at[pl.ds(hbm_token(r) * chunks, chunks), :],
                                buf_ref.at[pl.ds((t0 + r) * stride, chunks), :])
                pltpu.make_async_copy(src, dst, sem).start()
            return carry

        lax.fori_loop(0, n_tokens // GATHER_UNROLL, body, 0)

    def wait():
        src, dst = pair(hbm_ref.at[pl.ds(0, n_tokens * chunks), :],
                        buf_ref.at[pl.ds(t0 * stride, n_tokens * chunks), :])
        pltpu.make_async_copy(src, dst, sem).wait()

    return start, wait


def _router_kernel(x_ref, g_ref, w_ref, b_ref, o_ref, htok_hbm, tokbuf, sem, *, n_groups, per_group):
    tm, d = x_ref.shape
    words = d // (2 * LANES)
    h = _rmsnorm_rows(x_ref[...], g_ref[...])
    bits = pltpu.bitcast(h.astype(BF16).astype(F32), jnp.uint32)
    for c in range(words):
        low = bits[:, 2 * c * LANES:(2 * c + 1) * LANES]
        high = bits[:, (2 * c + 1) * LANES:(2 * c + 2) * LANES]
        tokbuf[_chunk(c, 0, tm, words)] = (low >> 16) | (high & jnp.uint32(HIGH_HALF))
    row0 = pl.program_id(0) * tm
    start_tok_copy, wait_tok_copy = _token_copies(htok_hbm, lambda r: row0 + r, tokbuf, 0, tm, words, sem,
                                                  to_hbm=True)
    start_tok_copy()
    h_hi, h_lo = _split_bf16(h, 2)

    w_hi, w_lo = _split_bf16(w_ref[...], 2)
    logits = _dot(h_hi, w_hi) + (_dot(h_hi, w_lo) + _dot(h_lo, w_hi)) + b_ref[...]
    n_experts = n_groups * per_group
    lane_i = lax.broadcasted_iota(jnp.int32, logits.shape, 1)
    lane = lane_i.astype(F32)
    neg = jnp.float32(-jnp.inf)

    def first_argmax(x, m):
        return jnp.min(jnp.where(x == m, lane, float(LANES)), axis=-1, keepdims=True)

    lg = jnp.where(jnp.logical_and(lane_i >= n_experts, lane_i < n_experts + n_groups), logits, neg)
    mg = jnp.max(lg, axis=-1, keepdims=True)
    pg_top = 1.0 / jnp.sum(jnp.exp(lg - mg), axis=-1, keepdims=True)
    gsel = first_argmax(lg, mg) - float(n_experts)
    shift = per_group.bit_length() - 1
    in_group = jnp.logical_and(lane_i < n_experts, (lane_i >> shift).astype(F32) == gsel)
    le = jnp.where(in_group, logits, neg)
    m0 = jnp.max(le, axis=-1, keepdims=True)
    z = jnp.sum(jnp.exp(le - m0), axis=-1, keepdims=True)
    e0 = first_argmax(le, m0)
    le1 = jnp.where(lane == e0, neg, le)
    m1 = jnp.max(le1, axis=-1, keepdims=True)
    e1 = first_argmax(le1, m1)
    p0 = 1.0 / z
    p1 = jnp.exp(m1 - m0) / z
    den = p0 + p1
    g0 = pg_top * (p0 / den)
    g1 = pg_top * (p1 / den)
    rec = jnp.where(lane_i == ROUTE_E0, e0, 0.0)
    rec = jnp.where(lane_i == ROUTE_E1, e1, rec)
    rec = jnp.where(lane_i == ROUTE_G0, g0, rec)
    rec = jnp.where(lane_i == ROUTE_G1, g1, rec)
    o_ref[...] = rec
    wait_tok_copy()


def _router(x, g, w_group, b_group, w_expert, b_expert):
    t, d = x.shape
    n_groups, n_experts = w_group.shape[1], w_expert.shape[1]
    per_group = n_experts // n_groups
    assert n_experts + n_groups <= LANES and per_group & (per_group - 1) == 0
    pad = lambda a: jnp.pad(a, ((0, 0), (0, LANES - a.shape[1])))
    w_both = pad(jnp.concatenate([w_expert, w_group], axis=1))
    b_both = pad(jnp.concatenate([b_expert, b_group]).reshape(1, -1))
    words = d // (2 * LANES)
    tm = _pick(t, 256, GATHER_UNROLL)
    return pl.pallas_call(
        functools.partial(_router_kernel, n_groups=n_groups, per_group=per_group),
        grid=(t // tm,),
        in_specs=[pl.BlockSpec((tm, d), lambda i: (i, 0)), pl.BlockSpec((1, d), lambda i: (0, 0)),
                  pl.BlockSpec((d, LANES), lambda i: (0, 0)), pl.BlockSpec((1, LANES), lambda i: (0, 0))],
        out_specs=[pl.BlockSpec((tm, LANES), lambda i: (i, 0)), pl.BlockSpec(memory_space=pl.ANY)],
        out_shape=[jax.ShapeDtypeStruct((t, LANES), F32), jax.ShapeDtypeStruct((t * words, LANES), jnp.uint32)],
        scratch_shapes=[pltpu.VMEM((tm * _token_stride(words), LANES), jnp.uint32), pltpu.SemaphoreType.DMA(())],
        compiler_params=_params(("parallel",), 32),
        name="router",
    )(x, g.reshape(1, d), w_both, b_both)


def _dispatch_plan(e_ids, n_experts, tile):
    t, k = e_ids.shape
    n_pairs = t * k
    n_tiles = n_pairs // tile + n_experts
    flat = e_ids.reshape(n_pairs)
    onehot = (flat[:, None] == jnp.arange(n_experts, dtype=jnp.int32)[None, :]).astype(jnp.int32)
    counts = jnp.sum(onehot, axis=0)
    tiles_per = (counts + tile - 1) // tile
    tile_end = jnp.cumsum(tiles_per)
    tile_start = tile_end - tiles_per
    rank = jnp.take_along_axis(jnp.cumsum(onehot, axis=0), flat[:, None], axis=1)[:, 0] - 1
    pos = tile_start[flat] * tile + rank
    token = jnp.arange(n_pairs, dtype=jnp.int32) // k
    src = jnp.zeros((n_tiles * tile,), jnp.int32).at[pos].set(token)
    n_used = tile_end[-1]
    tile_ids = jnp.minimum(jnp.arange(n_tiles, dtype=jnp.int32), n_used - 1)
    tile_expert = jnp.searchsorted(tile_end, tile_ids, side="right").astype(jnp.int32)
    return pos.reshape(t, k), src.reshape(n_tiles, 1, tile), tile_expert, n_used.reshape(1).astype(jnp.int32)


DOWN_COLS = 1024


def _experts_kernel(texp_ref, nused_ref, idx_ref, idx_next_ref, h_hbm, wg_ref, wu_ref, wd_ref,
                    y_hbm, xbuf, ybuf, hbuf, gsem, ysem):
    i = pl.program_id(0)
    n_used = nused_ref[0]
    tile, d = hbuf.shape
    chunks = d // LANES
    words = chunks // 2
    slot = lax.rem(i, 2)
    wg_bf, wu_bf, wd_bf = wg_ref.at[0], wu_ref.at[0], wd_ref.at[0]
    start_tokens, wait_tokens = _token_copies(h_hbm, lambda r: idx_ref[0, 0, r], xbuf, slot * tile, tile, words,
                                              gsem.at[slot], to_hbm=False)
    start_next_tokens, _ = _token_copies(h_hbm, lambda r: idx_next_ref[0, 0, r], xbuf, (1 - slot) * tile, tile,
                                         words, gsem.at[1 - slot], to_hbm=False)

    def y_copy(step):
        return _token_copies(y_hbm, lambda r: step * tile + r, ybuf, 0, tile, chunks, ysem, to_hbm=True)

    @pl.when(i < n_used)
    def _():
        pl.when(i == 0)(start_tokens)
        pl.when(i + 1 < n_used)(start_next_tokens)
        wait_tokens()
        for c in range(words):
            word = xbuf[_chunk(c, slot * tile, tile, words)]
            low = pltpu.bitcast(word << 16, F32)
            high = pltpu.bitcast(word & jnp.uint32(HIGH_HALF), F32)
            hbuf[:, 2 * c * LANES:(2 * c + 1) * LANES] = low.astype(BF16)
            hbuf[:, (2 * c + 1) * LANES:(2 * c + 2) * LANES] = high.astype(BF16)
        h = hbuf[...]
        a = _dot(h, wg_bf[...])
        u = _dot(h, wu_bf[...])
        hid = (_silu(a) * u).astype(BF16)

        @pl.when(i > 0)
        def _():
            y_copy(i - 1)[1]()

        for c0 in range(0, d, DOWN_COLS):
            y = _dot(hid, wd_bf[:, c0:c0 + DOWN_COLS])
            for c in range(DOWN_COLS // LANES):
                ybuf[_chunk(c0 // LANES + c, 0, tile, chunks)] = y[:, c * LANES:(c + 1) * LANES]

    @pl.when(i >= n_used)
    def _():
        y_copy(i - 1)[1]()

        @pl.when(i == n_used)
        def _():
            ybuf[...] = jnp.zeros(ybuf.shape, F32)

    y_copy(i)[0]()

    @pl.when(i == pl.num_programs(0) - 1)
    def _():
        y_copy(i)[1]()


def _experts(h_tok, src, tile_expert, n_used, wg, wu, wd):
    d = wg.shape[1]
    chunks = d // LANES
    words = chunks // 2
    n_tiles, _, tile = src.shape
    de = wg.shape[2]
    assert d % 512 == 0 and de % 64 == 0 and d % DOWN_COLS == 0 and tile % GATHER_UNROLL == 0
    idx_spec = lambda ahead: pl.BlockSpec(
        (1, 1, tile), lambda i, texp, nused: (jnp.minimum(i + ahead, nused[0] - 1), 0, 0),
        memory_space=pltpu.SMEM)
    hbm = pl.BlockSpec(memory_space=pl.ANY)
    grid_spec = pltpu.PrefetchScalarGridSpec(
        num_scalar_prefetch=2,
        grid=(n_tiles,),
        in_specs=[idx_spec(0), idx_spec(1), hbm,
                  pl.BlockSpec((1, d, de), lambda i, texp, nused: (texp[i], 0, 0)),
                  pl.BlockSpec((1, d, de), lambda i, texp, nused: (texp[i], 0, 0)),
                  pl.BlockSpec((1, de, d), lambda i, texp, nused: (texp[i], 0, 0))],
        out_specs=hbm,
        scratch_shapes=[pltpu.VMEM((2 * tile * _token_stride(words), LANES), jnp.uint32),
                        pltpu.VMEM((tile * _token_stride(chunks), LANES), F32), pltpu.VMEM((tile, d), BF16),
                        pltpu.SemaphoreType.DMA((2,)), pltpu.SemaphoreType.DMA(())],
    )
    weights_bytes = 2 * 3 * d * de * 2
    tiles_bytes = tile * (2 * _token_stride(words) + _token_stride(chunks)) * LANES * 4
    tiles_bytes += tile * d * 2 + tile * DOWN_COLS * 4 + 3 * tile * de * 4
    return pl.pallas_call(
        _experts_kernel,
        grid_spec=grid_spec,
        out_shape=jax.ShapeDtypeStruct((n_tiles * tile * chunks, LANES), F32),
        compiler_params=_params(("arbitrary",), (weights_bytes + tiles_bytes) // MIB + 3),
        name="experts",
    )(tile_expert, n_used, src, src, h_tok, wg, wu, wd)


def _combine_kernel(idx0_ref, idx1_ref, idx0_next_ref, idx1_next_ref, y_hbm, x_ref, route_ref, gf_ref, *refs,
                    n_first, final_norm):
    outs, (buf0, buf1, res_ref, sem) = refs[:-4], refs[-4:]
    i = pl.program_id(0)
    tm, d = x_ref.shape
    chunks = d // LANES
    slot = lax.rem(i, 2)
    def gather(idx_ref, buf, k, into_slot):
        return _token_copies(y_hbm, lambda r: idx_ref[0, 0, r], buf, into_slot * tm, tm, chunks,
                             sem.at[k, into_slot], to_hbm=False)

    gathers = [gather(idx0_ref, buf0, 0, slot), gather(idx1_ref, buf1, 1, slot)]
    gathers_next = [gather(idx0_next_ref, buf0, 0, 1 - slot), gather(idx1_next_ref, buf1, 1, 1 - slot)]

    @pl.when(i == 0)
    def _():
        for start, _ in gathers:
            start()

    @pl.when(i + 1 < pl.num_programs(0))
    def _():
        for start, _ in gathers_next:
            start()

    for _, wait in gathers:
        wait()
    g0 = route_ref[:, ROUTE_G0:ROUTE_G0 + 1]
    g1 = route_ref[:, ROUTE_G1:ROUTE_G1 + 1]
    ssq = jnp.zeros((tm, LANES), F32)
    for c in range(chunks):
        lanes = slice(c * LANES, (c + 1) * LANES)
        y0 = buf0[_chunk(c, slot * tm, tm, chunks)]
        y1 = buf1[_chunk(c, slot * tm, tm, chunks)]
        r = x_ref[:, lanes] + (g0 * y0 + g1 * y1)
        res_ref[:, lanes] = r
        ssq = ssq + r * r
    if final_norm:
        scale = lax.rsqrt(jnp.sum(ssq, axis=-1, keepdims=True) * (1.0 / d) + EPS)
        result = lambda: res_ref[...] * scale * gf_ref[...]
    else:
        result = lambda: res_ref[...]
    if len(outs) == 1:
        outs[0][...] = result()
    else:
        @pl.when(i < n_first)
        def _():
            outs[0][...] = result()

        @pl.when(i >= n_first)
        def _():
            outs[1][...] = result()


def _combine(x, y_sorted, pos, route, g_final, *, split_rows, final_norm):
    t, d = x.shape
    assert y_sorted.shape[1] == LANES
    tm = _pick(t if split_rows is None else math.gcd(split_rows, t - split_rows), 128, GATHER_UNROLL)
    n_blk = t // tm
    idx = [pos[:, k].reshape(n_blk, 1, tm) for k in range(TOP_K_IN_GROUP)]
    smem = pl.BlockSpec((1, 1, tm), lambda i: (i, 0, 0), memory_space=pltpu.SMEM)
    smem_next = pl.BlockSpec((1, 1, tm), lambda i: (jnp.minimum(i + 1, n_blk - 1), 0, 0), memory_space=pltpu.SMEM)
    row = pl.BlockSpec((tm, d), lambda i: (i, 0))
    if split_rows is None:
        n_first = n_blk
        out_specs = [row]
        out_shape = [jax.ShapeDtypeStruct((t, d), F32)]
    else:
        n_first = split_rows // tm
        out_specs = [pl.BlockSpec((tm, d), lambda i: (jnp.minimum(i, n_first - 1), 0)),
                     pl.BlockSpec((tm, d), lambda i: (jnp.maximum(i - n_first, 0), 0))]
        out_shape = [jax.ShapeDtypeStruct((split_rows, d), F32), jax.ShapeDtypeStruct((t - split_rows, d), F32)]
    return pl.pallas_call(
        functools.partial(_combine_kernel, n_first=n_first, final_norm=final_norm),
        grid=(n_blk,),
        in_specs=[smem, smem, smem_next, smem_next, pl.BlockSpec(memory_space=pl.ANY), row,
                  pl.BlockSpec((tm, LANES), lambda i: (i, 0)), pl.BlockSpec((1, d), lambda i: (0, 0))],
        out_specs=out_specs,
        out_shape=out_shape,
        scratch_shapes=[pltpu.VMEM((2 * tm * _token_stride(d // LANES), LANES), F32),
                        pltpu.VMEM((2 * tm * _token_stride(d // LANES), LANES), F32), pltpu.VMEM((tm, d), F32),
                        pltpu.SemaphoreType.DMA((TOP_K_IN_GROUP, 2))],
        compiler_params=_params(("arbitrary",), 32),
        name="combine",
    )(idx[0], idx[1], idx[0], idx[1], y_sorted, x, route, g_final.reshape(1, d))


def kernel(x_prompt, x_sample, mem_prompt, state_gla, state_conv, cache_mem_k, cache_mem_v, norm_mix_g, w_in, w_alpha_up, b_alpha, gla_norm_g, w_branch_a, conv_dw_w, conv_dw_b, conv_ln_g, conv_ln_b, w_branch_b, w_out, norm_ca_g, norm_mem_g, w_ca_q, w_ca_k, w_ca_v, w_ca_o, norm_ffn_g, w_router_group, b_router_group, w_router_expert, b_router_expert, w_exp_gate, w_exp_up, w_exp_down, norm_final_g):
    depth = w_in.shape[0]
    bp, tp, d = x_prompt.shape
    bs, ts, _ = x_sample.shape
    heads, dk, dv = state_gla.shape[2:]
    key, val = heads * dk, heads * dv
    rank = w_alpha_up.shape[1]
    cd = state_conv.shape[3]
    n_mem, ca_heads, ca_hd = cache_mem_k.shape[2:]
    ca = ca_heads * ca_hd
    n_experts = w_router_expert.shape[2]
    rows_p, rows_s = bp * tp, bs * ts
    off_a = 2 * key + 2 * val
    off_u = off_a + rank
    off_g = off_u + 2 * cd
    assert rank <= LANES and w_in.shape[2] == off_g + 2 * d
    bf = lambda a: a.astype(BF16)

    x = (x_prompt.reshape(rows_p, d), x_sample.reshape(rows_s, d))
    outs = dict(gla_p=[], conv_p=[], mk_p=[], mv_p=[], gla_s=[], conv_s=[])
    for l in range(depth):
        w_in_t = jnp.swapaxes(w_in, 1, 2)[l]
        w_qkvr = _cast_weight(w_in_t, 0, off_a)
        w_a = bf(jnp.pad(w_in[l, :, off_a:off_u], ((0, 0), (0, LANES - rank))))
        h, x = _rmsnorm(x, norm_mix_g[l], BF16, stacked_copy=True)
        dense = (w_branch_a, w_branch_b, w_out, w_ca_q, w_ca_k, w_ca_v, w_ca_o)
        z, (w_ug, wa_bf, wb_bf, wo_bf, wcq_bf, wck_bf, wcv_bf, wco_bf) = _mm(
            [h], [w_qkvr], [0], _ep_identity, F32, w_rows=[True], tm_pref=WIDE_ROWS,
            side_cast=[(w_in_t, off_u, 2 * cd + 2 * d)] + [(w[l], 0, w.shape[1]) for w in dense], name="in_qkvr")
        a_low = _mm([h], [w_a], [0], _ep_identity, F32, name="in_alow")
        ug = _mm([h], [w_ug, w_ug], [0, 0], _ep_glu, F32, n=cd, w_col0=[0, cd], w_rows=[True, True],
                 tm_pref=WIDE_ROWS, name="in_glu")
        w_up = bf(jnp.pad(w_alpha_up[l], ((0, LANES - rank), (0, 0))))
        gla_args = dict(heads=heads, dk=dk, dv=dv)
        og_p, st_p = _gla(z, a_low, w_up, b_alpha[l], gla_norm_g[l], None, row0=0, n_seq=bp, seq_len=tp,
                          name="gla_prompt", **gla_args)
        og_s, st_s = _gla(z, a_low, w_up, b_alpha[l], gla_norm_g[l], state_gla[l], row0=rows_p, n_seq=bs,
                          seq_len=ts, name="gla_sample", **gla_args)
        conv_w = (conv_dw_w[l], conv_dw_b[l], conv_ln_g[l], conv_ln_b[l])
        c_p = _conv_prompt(ug, *conv_w, n_seq=bp, seq_len=tp)
        c_s, buf_s = _conv_sample(ug, state_conv[l], *conv_w, row0=rows_p, seq_len=ts)
        og = (og_p, og_s)
        c = (c_p, c_s)
        flat = lambda w: w[l].reshape(-1, w.shape[-1])
        expert_casts = [(flat(w), 0, w.shape[1] * w.shape[2]) for w in (w_exp_gate, w_exp_up, w_exp_down)]
        merged, (wg_bf, wu_bf, wd_bf) = _mm(
            [og, c, h], [w_ug, w_ug, wa_bf, wb_bf], [2, 2, 0, 1], _ep_gated_merge, BF16,
            n=d, w_col0=[2 * cd, 2 * cd + d, 0, 0], w_rows=[True, True, False, False], tn_pref=256,
            side_cast=expert_casts, name="merge")
        x = _mm([merged], [wo_bf], [0], _ep_residual, F32, extras=[x], name="out_proj")
        q = _mm([x], [wcq_bf], [0], _ep_identity, BF16, norm_gain=norm_ca_g[l], name="ca_q")
        m = _rmsnorm(mem_prompt.reshape(bp * n_mem, d), norm_mem_g[l], BF16)
        mk = _mm([m], [wck_bf], [0], _ep_identity, F32, name="mem_k")
        mv = _mm([m], [wcv_bf], [0], _ep_identity, F32, name="mem_v")
        ao_p = _attn(q, mk, mv, row0=0, n_seq=bp, seq_len=tp, n_mem=n_mem, heads=ca_heads, name="attn_prompt")
        ao_s = _attn(q, cache_mem_k[l].reshape(bs * n_mem, ca), cache_mem_v[l].reshape(bs * n_mem, ca),
                     row0=rows_p, n_seq=bs, seq_len=ts, n_mem=n_mem, heads=ca_heads, name="attn_sample")
        ao = (ao_p, ao_s)
        x = _mm([ao], [wco_bf], [0], _ep_residual, F32, extras=[x], name="ca_out")
        route, h_tok = _router(x, norm_ffn_g[l], w_router_group[l], b_router_group[l], w_router_expert[l],
                               b_router_expert[l])
        e_ids = route[:, ROUTE_E0:ROUTE_E1 + 1].astype(jnp.int32)
        tile = _pick(rows_p + rows_s, 256, SUBLANES)
        pos, src, tile_expert, n_used = _dispatch_plan(e_ids, n_experts, tile)
        y_sorted = _experts(h_tok, src, tile_expert, n_used, wg_bf.reshape(w_exp_gate.shape[1:]),
                            wu_bf.reshape(w_exp_up.shape[1:]), wd_bf.reshape(w_exp_down.shape[1:]))
        last = l == depth - 1
        res = _combine(x, y_sorted, pos, route, norm_final_g, split_rows=rows_p if last else None,
                       final_norm=last)
        if not last:
            x = res[0]
        outs["gla_p"].append(st_p)
        hist = conv_dw_w.shape[1] - 1
        outs["conv_p"].append(jnp.stack([ug[(b + 1) * tp - hist:(b + 1) * tp] for b in range(bp)]))
        outs["mk_p"].append(mk.reshape(bp, n_mem, ca_heads, ca_hd))
        outs["mv_p"].append(mv.reshape(bp, n_mem, ca_heads, ca_hd))
        outs["gla_s"].append(st_s)
        outs["conv_s"].append(buf_s)
    y_prompt = res[0].reshape(bp, tp, d)
    y_sample = res[1].reshape(bs, ts, d)
    return (y_prompt, y_sample, jnp.stack(outs["gla_p"]), jnp.stack(outs["conv_p"]), jnp.stack(outs["mk_p"]),
            jnp.stack(outs["mv_p"]), jnp.stack(outs["gla_s"]), jnp.stack(outs["conv_s"]))
```

```python
import functools
import math

import jax
import jax.numpy as jnp
from jax import lax
from jax.experimental import pallas as pl
from jax.experimental.pallas import tpu as pltpu

F32 = jnp.float32
BF16 = jnp.bfloat16

EPS = 1e-6
GLA_TAU = 16.0
GLA_CHUNK = 64
TOP_K_IN_GROUP = 2
LANES = 128
SUBLANES = 8
MIB = 1 << 20


def _pick(n, pref, mult):
    for d in range(min(pref, n), 0, -1):
        if n % d == 0 and d % mult == 0:
            return d
    return n


def _params(semantics, vmem_mib):
    return pltpu.CompilerParams(dimension_semantics=semantics, vmem_limit_bytes=vmem_mib * MIB)


def _sigmoid(x):
    return 1.0 / (1.0 + jnp.exp(-x))


def _silu(x):
    return x * _sigmoid(x)


def _dot(a, b):
    return jnp.dot(a, b, preferred_element_type=F32)


def _dot_nt(a, b):
    return lax.dot_general(a, b, (((1,), (1,)), ((), ())), preferred_element_type=F32)


def _dot_tn(a, b):
    return lax.dot_general(a, b, (((0,), (0,)), ((), ())), preferred_element_type=F32)


def _split_bf16(x, parts):
    out = []
    for _ in range(parts - 1):
        p = x.astype(BF16)
        out.append(p)
        x = x - p.astype(F32)
    out.append(x.astype(BF16))
    return out


def _rmsnorm_rows(x, g):
    return x * lax.rsqrt(jnp.mean(x * x, axis=-1, keepdims=True) + EPS) * g


def _parts(a):
    return a if isinstance(a, tuple) else (a,)


def _rows_of(a):
    return sum(p.shape[0] for p in _parts(a))


def _part_specs(a, tm, cols, col_of):
    parts = _parts(a)
    if len(parts) == 1:
        return [pl.BlockSpec((tm, cols), lambda i, *j: (i, col_of(*j)))]
    n_first = parts[0].shape[0] // tm
    assert parts[0].shape[0] % tm == 0 and parts[1].shape[0] % tm == 0
    return [pl.BlockSpec((tm, cols), lambda i, *j: (jnp.minimum(i, n_first - 1), col_of(*j))),
            pl.BlockSpec((tm, cols), lambda i, *j: (jnp.maximum(i - n_first, 0), col_of(*j)))]


def _part_tile(a, pref, mult):
    return _pick(math.gcd(*[p.shape[0] for p in _parts(a)]), pref, mult)


def _select_part(refs, n_first):
    if len(refs) == 1:
        return refs[0][...]
    return jnp.where(pl.program_id(0) < n_first, refs[0][...], refs[1][...])


def _rmsnorm_kernel(*refs, n_first):
    x_refs, (g_ref, o_ref) = refs[:-2], refs[-2:]
    o_ref[...] = _rmsnorm_rows(_select_part(x_refs, n_first), g_ref[...]).astype(o_ref.dtype)


def _rmsnorm(x, g, out_dtype):
    t, d = _rows_of(x), _parts(x)[0].shape[1]
    tm = _part_tile(x, 128, 16)
    return pl.pallas_call(
        functools.partial(_rmsnorm_kernel, n_first=_parts(x)[0].shape[0] // tm),
        grid=(t // tm,),
        in_specs=_part_specs(x, tm, d, lambda: 0) + [pl.BlockSpec((1, d), lambda i: (0, 0))],
        out_specs=pl.BlockSpec((tm, d), lambda i: (i, 0)),
        out_shape=jax.ShapeDtypeStruct((t, d), out_dtype),
        compiler_params=_params(("parallel",), 32),
        name="rmsnorm",
    )(*_parts(x), g.reshape(1, d))


def _cast_weight_kernel(w_hbm, o_ref, buf, sem, *, row0):
    j = pl.program_id(0)
    tn = o_ref.shape[0]
    slot = lax.rem(j, 2)

    def copy(block, into):
        rows = pl.ds(pl.multiple_of(row0 + block * tn, SUBLANES), tn)
        return pltpu.make_async_copy(w_hbm.at[rows, :], buf.at[into], sem.at[into])

    @pl.when(j == 0)
    def _():
        copy(j, slot).start()

    @pl.when(j + 1 < pl.num_programs(0))
    def _():
        copy(j + 1, 1 - slot).start()

    copy(j, slot).wait()
    o_ref[...] = buf[slot].astype(BF16)


def _cast_weight(w, row0, n_rows):
    _, k = w.shape
    tn = _pick(n_rows, 256, 16)
    assert row0 % SUBLANES == 0 and n_rows % tn == 0
    return pl.pallas_call(
        functools.partial(_cast_weight_kernel, row0=row0),
        grid=(n_rows // tn,),
        in_specs=[pl.BlockSpec(memory_space=pl.ANY)],
        out_specs=pl.BlockSpec((tn, k), lambda j: (j, 0)),
        out_shape=jax.ShapeDtypeStruct((n_rows, k), BF16),
        scratch_shapes=[pltpu.VMEM((2, tn, k), F32), pltpu.SemaphoreType.DMA((2,))],
        compiler_params=_params(("arbitrary",), 32),
        name="cast_weight",
    )(w)


def _ep_identity(accs, extras):
    return accs[0]


def _ep_glu(accs, extras):
    return accs[0] * _sigmoid(accs[1])


def _ep_residual(accs, extras):
    return extras[0] + accs[0]


def _ep_gated_merge(accs, extras):
    return _sigmoid(accs[0]) * accs[2] + _sigmoid(accs[1]) * accs[3]


WIDE_ROWS = 1088


def _side_cast_step(step, n_chunks, src_hbm, dst_hbm, row0, in_buf, out_buf, in_sem, out_sem):
    rows = in_buf.shape[1]
    slot = lax.rem(step, 2)

    def read(chunk, into):
        start = pl.multiple_of(row0 + chunk * rows, SUBLANES)
        return pltpu.make_async_copy(src_hbm.at[pl.ds(start, rows), :], in_buf.at[into], in_sem.at[into])

    def write(chunk):
        start = pl.multiple_of(chunk * rows, 16)
        return pltpu.make_async_copy(out_buf, dst_hbm.at[pl.ds(start, rows), :], out_sem)

    @pl.when(step < n_chunks)
    def _():
        pl.when(step == 0)(lambda: read(step, slot).start())
        pl.when(step + 1 < n_chunks)(lambda: read(step + 1, 1 - slot).start())
        read(step, slot).wait()
        pl.when(step > 0)(lambda: write(step - 1).wait())
        out_buf[...] = in_buf[slot].astype(BF16)
        write(step).start()
        pl.when(step == n_chunks - 1)(lambda: write(step).wait())


def _mm_kernel(*refs, x_of_w, w_rows, x_parts, e_parts, n_first, has_norm, side, epilogue):
    refs = list(refs)
    take = lambda k: [refs.pop(0) for _ in range(k)]
    x_refs = [take(p) for p in x_parts]
    g_ref = refs.pop(0) if has_norm else None
    w_refs = take(len(x_of_w))
    e_refs = [take(p) for p in e_parts]
    side_src = take(len(side))
    o_ref = refs.pop(0)
    side_dst = take(len(side))
    i = pl.program_id(0)
    step = i * pl.num_programs(1) + pl.program_id(1)
    side_scratch = [refs[len(refs) - 4 * len(side) + 4 * k:len(refs) - 4 * len(side) + 4 * k + 4]
                    for k in range(len(side))]
    for (row0, n_chunks), src, dst, (in_buf, out_buf, in_sem, out_sem) in zip(side, side_src, side_dst, side_scratch):
        _side_cast_step(step, n_chunks, src, dst, row0, in_buf, out_buf, in_sem, out_sem)
    first_col = pl.program_id(1) == 0
    xs = []
    for k, parts in enumerate(x_refs):
        norm = has_norm and k == 0
        if len(parts) == 1 and not norm:
            xs.append(parts[0])
            continue
        scratch = refs.pop(0)

        def fill(src, scratch=scratch, norm=norm):
            val = src[...]
            if norm:
                val = _rmsnorm_rows(val, g_ref[...])
            scratch[...] = val.astype(scratch.dtype)

        if len(parts) == 1:
            pl.when(first_col)(functools.partial(fill, parts[0]))
        else:
            pl.when(jnp.logical_and(first_col, i < n_first))(functools.partial(fill, parts[0]))
            pl.when(jnp.logical_and(first_col, i >= n_first))(functools.partial(fill, parts[1]))
        xs.append(scratch)
    accs = [(_dot_nt if w_rows[k] else _dot)(xs[x_of_w[k]][...], w_refs[k][...]) for k in range(len(x_of_w))]
    o_ref[...] = epilogue(accs, [_select_part(e, n_first) for e in e_refs]).astype(o_ref.dtype)


def _mm(xs, ws, x_of_w, epilogue, out_dtype, extras=(), tm_pref=512, tn_pref=512, name="mm", n=None, w_col0=None,
        w_rows=None, norm_gain=None, side_cast=()):
    t = _rows_of(xs[0])
    w_rows = [False] * len(ws) if w_rows is None else w_rows
    w_col0 = [0] * len(ws) if w_col0 is None else w_col0
    n = ws[0].shape[0 if w_rows[0] else 1] if n is None else n
    tm = min(_part_tile(a, tm_pref, 16) for a in list(xs) + list(extras))
    tn = _pick(math.gcd(n, *w_col0), tn_pref, LANES)
    split = [a for a in list(xs) + list(extras) if len(_parts(a)) == 2]
    assert all(a[0].shape[0] == split[0][0].shape[0] and a[0].shape[0] % tm == 0 for a in split)
    n_first = split[0][0].shape[0] // tm if split else 0
    k_of = lambda x: _parts(x)[0].shape[1]
    in_specs, args = [], []
    for x in xs:
        in_specs += _part_specs(x, tm, k_of(x), lambda j: 0)
        args += _parts(x)
    if norm_gain is not None:
        in_specs.append(pl.BlockSpec((1, k_of(xs[0])), lambda i, j: (0, 0)))
        args.append(norm_gain.reshape(1, -1))
    for w, c0, rows in zip(ws, w_col0, w_rows):
        if rows:
            in_specs.append(pl.BlockSpec((tn, w.shape[1]), lambda i, j, b0=c0 // tn: (b0 + j, 0)))
        else:
            in_specs.append(pl.BlockSpec((w.shape[0], tn), lambda i, j, b0=c0 // tn: (0, b0 + j)))
    args += ws
    for e in extras:
        in_specs += _part_specs(e, tm, tn, lambda j: j)
        args += _parts(e)
    staged = [x for k, x in enumerate(xs) if len(_parts(x)) == 2 or (norm_gain is not None and k == 0)]
    scratch = [pltpu.VMEM((tm, k_of(x)), BF16) for x in staged]
    steps = (t // tm) * (n // tn)
    side, side_out_shape, side_bytes = [], [], 0
    for src, row0, n_rows in side_cast:
        rows = next(r for r in range(16, n_rows + 1, 16) if n_rows % r == 0 and n_rows // r <= steps)
        assert row0 % SUBLANES == 0
        side.append((row0, n_rows // rows))
        in_specs.append(pl.BlockSpec(memory_space=pl.ANY))
        args.append(src)
        side_out_shape.append(jax.ShapeDtypeStruct((n_rows, src.shape[1]), BF16))
        scratch += [pltpu.VMEM((2, rows, src.shape[1]), F32), pltpu.VMEM((rows, src.shape[1]), BF16),
                    pltpu.SemaphoreType.DMA((2,)), pltpu.SemaphoreType.DMA(())]
        side_bytes += rows * src.shape[1] * (2 * 4 + 2)
    block_bytes = sum(tm * p.shape[1] * p.dtype.itemsize for x in xs for p in _parts(x))
    block_bytes += sum(w.size // (w.shape[0 if rows else 1] // tn) * w.dtype.itemsize for w, rows in zip(ws, w_rows))
    block_bytes += sum(tm * tn * p.dtype.itemsize for e in extras for p in _parts(e))
    block_bytes += tm * tn * jnp.dtype(out_dtype).itemsize
    scratch_bytes = sum(tm * k_of(x) * 2 for x in staged)
    norm_bytes = 2 * tm * k_of(xs[0]) * 4 if norm_gain is not None else 0
    acc_bytes = len(ws) * tm * tn * 4
    vmem = (2 * block_bytes + 2 * acc_bytes + scratch_bytes + norm_bytes + side_bytes) // MIB + 4
    outs = pl.pallas_call(
        functools.partial(_mm_kernel, x_of_w=tuple(x_of_w), w_rows=tuple(w_rows),
                          x_parts=tuple(len(_parts(x)) for x in xs), e_parts=tuple(len(_parts(e)) for e in extras),
                          n_first=n_first, has_norm=norm_gain is not None, side=tuple(side), epilogue=epilogue),
        grid=(t // tm, n // tn),
        in_specs=in_specs,
        out_specs=[pl.BlockSpec((tm, tn), lambda i, j: (i, j))] + [pl.BlockSpec(memory_space=pl.ANY)] * len(side),
        out_shape=[jax.ShapeDtypeStruct((t, n), out_dtype)] + side_out_shape,
        scratch_shapes=scratch,
        compiler_params=_params(("arbitrary", "arbitrary") if side else ("parallel", "arbitrary"), vmem),
        name=name,
    )(*args)
    return (outs[0], outs[1:]) if side else outs[0]


def _log_sigmoid(x):
    return jnp.minimum(x, 0.0) - jnp.log1p(jnp.exp(-jnp.abs(x)))


def _gla_kernel(*refs, heads, seq_rows, has_h0, scale):
    if has_h0:
        q_ref, k_ref, v_ref, r_ref, a_ref, wup_ref, bal_ref, gn_ref, h0_ref, og_ref, st_ref = refs
    else:
        q_ref, k_ref, v_ref, r_ref, a_ref, wup_ref, bal_ref, gn_ref, og_ref, st_ref = refs
    rows, key = q_ref.shape
    val = v_ref.shape[1]
    dk, dv = key // heads, val // heads
    nseq = rows // seq_rows

    @pl.when(pl.program_id(1) == 0)
    def _():
        if has_h0:
            st_ref[...] = h0_ref[...]
        else:
            st_ref[...] = jnp.zeros(st_ref.shape, F32)

    la = _dot(a_ref[...].astype(BF16), wup_ref[...]) + bal_ref[...]
    log_a = _log_sigmoid(la) * (1.0 / GLA_TAU)

    ri = lax.broadcasted_iota(jnp.int32, (rows, rows), 0)
    ci = lax.broadcasted_iota(jnp.int32, (rows, rows), 1)
    causal = ci <= ri
    if nseq > 1:
        shift = seq_rows.bit_length() - 1
        causal = jnp.logical_and(causal, (ri >> shift) == (ci >> shift))
        row_seq = lax.broadcasted_iota(jnp.int32, (rows, 1), 0) >> shift
    tri = jnp.where(causal, 1.0, 0.0).astype(BF16)
    b = sum(_dot(tri, part) for part in _split_bf16(log_a, 3))

    q = q_ref[...] * scale
    k = k_ref[...]
    v = v_ref[...]
    r = r_ref[...]
    gn = gn_ref[...]
    for h in range(heads):
        ks = slice(h * dk, (h + 1) * dk)
        vs = slice(h * dv, (h + 1) * dv)
        bh = b[:, ks]
        q_e = (q[:, ks] * jnp.exp(bh)).astype(BF16)
        k_e = (k[:, ks] * jnp.exp(-bh)).astype(BF16)
        vh = v[:, vs].astype(BF16)
        scores = jnp.where(causal, _dot_nt(q_e, k_e), 0.0)
        o = _dot(scores.astype(BF16), vh)
        for s in range(nseq):
            b_last = bh[(s + 1) * seq_rows - 1:(s + 1) * seq_rows, :]
            k_s = k[:, ks] * jnp.exp(b_last - bh)
            if nseq > 1:
                in_seq = row_seq == s
                k_s = jnp.where(in_seq, k_s, 0.0)
            state = st_ref[s, h]
            o_inter = _dot(q_e, state.astype(BF16))
            o = o + (jnp.where(in_seq, o_inter, 0.0) if nseq > 1 else o_inter)
            d_state = _dot_tn(k_s.astype(BF16), vh)
            decay_col = jnp.broadcast_to(jnp.exp(b_last), (LANES, dk)).T[:, 0:1]
            st_ref[s, h] = decay_col * state + d_state
        mu = jnp.mean(o, axis=-1, keepdims=True)
        oc = o - mu
        o_n = oc * lax.rsqrt(jnp.mean(oc * oc, axis=-1, keepdims=True) + EPS) * gn[:, vs]
        og_ref[:, vs] = (o_n * _silu(r[:, vs])).astype(og_ref.dtype)


def _gla(z, a_low, w_up, b_alpha, gn, h0, *, row0, n_seq, seq_len, heads, dk, dv, name):
    key, val = heads * dk, heads * dv
    assert val == 2 * key and z.shape[1] == 2 * key + 2 * val
    chunk = min(GLA_CHUNK, seq_len)
    assert seq_len % chunk == 0
    if chunk % SUBLANES == 0:
        rows, n_chunks = chunk, seq_len // chunk
    else:
        assert SUBLANES % seq_len == 0 and chunk == seq_len
        rows, n_chunks = SUBLANES, 1
    nseq = rows // chunk
    assert n_seq % nseq == 0 and row0 % rows == 0
    blk0 = row0 // rows
    row_map = lambda g, c: blk0 + g * n_chunks + c
    in_specs = [
        pl.BlockSpec((rows, key), lambda g, c: (row_map(g, c), 0)),
        pl.BlockSpec((rows, key), lambda g, c: (row_map(g, c), 1)),
        pl.BlockSpec((rows, val), lambda g, c: (row_map(g, c), 1)),
        pl.BlockSpec((rows, val), lambda g, c: (row_map(g, c), 2)),
        pl.BlockSpec((rows, LANES), lambda g, c: (row_map(g, c), 0)),
        pl.BlockSpec((LANES, key), lambda g, c: (0, 0)),
        pl.BlockSpec((1, key), lambda g, c: (0, 0)),
        pl.BlockSpec((1, val), lambda g, c: (0, 0)),
    ]
    args = [z, z, z, z, a_low, w_up, b_alpha.reshape(1, key), gn.reshape(1, val)]
    state_spec = pl.BlockSpec((nseq, heads, dk, dv), lambda g, c: (g, 0, 0, 0))
    if h0 is not None:
        in_specs.append(state_spec)
        args.append(h0)
    n_rows = n_seq * seq_len
    return pl.pallas_call(
        functools.partial(_gla_kernel, heads=heads, seq_rows=chunk, has_h0=h0 is not None, scale=dk ** -0.5),
        grid=(n_seq // nseq, n_chunks),
        in_specs=in_specs,
        out_specs=[pl.BlockSpec((rows, val), lambda g, c: (g * n_chunks + c, 0)), state_spec],
        out_shape=[jax.ShapeDtypeStruct((n_rows, val), BF16),
                   jax.ShapeDtypeStruct((n_seq, heads, dk, dv), F32)],
        compiler_params=_params(("parallel", "arbitrary"), 48),
        name=name,
    )(*args)


CONV_LANES = 512
CONV_ROWS = 32
HIST_ROWS = 32


def _ln_silu(x, g, b):
    mu = jnp.mean(x, axis=-1, keepdims=True)
    xc = x - mu
    var = jnp.mean(xc * xc, axis=-1, keepdims=True)
    return _silu(xc * lax.rsqrt(var + EPS) * g + b)


def _conv_strip(ext_ref, w_ref, row_start, n_rows, lanes, width):
    n_lanes = lanes.stop - lanes.start
    total = None
    for r in range(SUBLANES):
        taps = [j for j in range(width) if (row_start + j) % SUBLANES == r]
        if not taps:
            continue
        rows = n_rows if r == 0 else n_rows + SUBLANES
        acc = jnp.zeros((rows, n_lanes), F32)
        for j in taps:
            start = row_start + j - r
            acc = acc + ext_ref[start:start + rows, lanes] * w_ref[j:j + 1, lanes]
        part = acc[r:r + n_rows, :]
        total = part if total is None else total + part
    return total


def _conv_prompt_kernel(u_ref, w_ref, wb_ref, g_ref, b_ref, c_ref, ext_ref, conv_ref, *, width):
    tb, cd = u_ref.shape
    first = HIST_ROWS - (width - 1)

    @pl.when(pl.program_id(1) == 0)
    def _():
        ext_ref[0:HIST_ROWS, :] = jnp.zeros((HIST_ROWS, cd), F32)

    ext_ref[HIST_ROWS:HIST_ROWS + tb, :] = u_ref[...]
    for r0 in range(0, tb, CONV_ROWS):
        for c0 in range(0, cd, CONV_LANES):
            lanes = slice(c0, c0 + CONV_LANES)
            conv_ref[r0:r0 + CONV_ROWS, lanes] = _conv_strip(ext_ref, w_ref, first + r0, CONV_ROWS, lanes, width)
    ext_ref[0:HIST_ROWS, :] = ext_ref[tb:tb + HIST_ROWS, :]
    c_ref[...] = _ln_silu(conv_ref[...] + wb_ref[...], g_ref[...], b_ref[...]).astype(c_ref.dtype)


def _conv_prompt(ug, w, wb, g, b, *, n_seq, seq_len):
    width, cd = w.shape
    assert width - 1 <= HIST_ROWS and cd % CONV_LANES == 0
    tb = _pick(seq_len, 64, CONV_ROWS)
    assert tb % CONV_ROWS == 0 and tb >= HIST_ROWS
    n_blk = seq_len // tb
    vec = pl.BlockSpec((1, cd), lambda s, t: (0, 0))
    return pl.pallas_call(
        functools.partial(_conv_prompt_kernel, width=width),
        grid=(n_seq, n_blk),
        in_specs=[pl.BlockSpec((tb, cd), lambda s, t: (s * n_blk + t, 0)),
                  pl.BlockSpec((width, cd), lambda s, t: (0, 0)), vec, vec, vec],
        out_specs=pl.BlockSpec((tb, cd), lambda s, t: (s * n_blk + t, 0)),
        out_shape=jax.ShapeDtypeStruct((n_seq * seq_len, cd), BF16),
        scratch_shapes=[pltpu.VMEM((HIST_ROWS + tb, cd), F32), pltpu.VMEM((tb, cd), F32)],
        compiler_params=_params(("parallel", "arbitrary"), 32),
        name="conv_prompt",
    )(ug, w, wb.reshape(1, cd), g.reshape(1, cd), b.reshape(1, cd))


def _conv_sample_kernel(u_ref, buf_ref, w_ref, wb_ref, g_ref, b_ref, c_ref, nbuf_ref, ext_ref, conv_ref,
                        *, width, seq_len):
    n_seq, hist, cd = buf_ref.shape
    pad_rows = ext_ref.shape[0] - hist - seq_len
    for s in range(n_seq):
        ext_ref[0:hist, :] = buf_ref[s]
        ext_ref[hist:hist + seq_len, :] = u_ref[s * seq_len:(s + 1) * seq_len, :]
        ext_ref[hist + seq_len:, :] = jnp.zeros((pad_rows, cd), F32)
        for c0 in range(0, cd, CONV_LANES):
            lanes = slice(c0, c0 + CONV_LANES)
            acc = _conv_strip(ext_ref, w_ref, 0, SUBLANES, lanes, width)
            conv_ref[s * seq_len:(s + 1) * seq_len, lanes] = acc[0:seq_len, :]
        nbuf_ref[s] = ext_ref[seq_len:seq_len + hist, :]
    c_ref[...] = _ln_silu(conv_ref[...] + wb_ref[...], g_ref[...], b_ref[...]).astype(c_ref.dtype)


def _conv_sample(ug, buf, w, wb, g, b, *, row0, seq_len):
    width, cd = w.shape
    n_seq, hist, _ = buf.shape
    assert hist == width - 1 and seq_len <= SUBLANES and cd % CONV_LANES == 0
    sb = _pick(n_seq, 8, 1)
    rows = sb * seq_len
    assert rows % 16 == 0 and row0 % rows == 0
    blk0 = row0 // rows
    ext_rows = -(-(hist + SUBLANES) // SUBLANES) * SUBLANES
    vec = pl.BlockSpec((1, cd), lambda i: (0, 0))
    return pl.pallas_call(
        functools.partial(_conv_sample_kernel, width=width, seq_len=seq_len),
        grid=(n_seq // sb,),
        in_specs=[pl.BlockSpec((rows, cd), lambda i: (blk0 + i, 0)),
                  pl.BlockSpec((sb, hist, cd), lambda i: (i, 0, 0)),
                  pl.BlockSpec((width, cd), lambda i: (0, 0)), vec, vec, vec],
        out_specs=[pl.BlockSpec((rows, cd), lambda i: (i, 0)),
                   pl.BlockSpec((sb, hist, cd), lambda i: (i, 0, 0))],
        out_shape=[jax.ShapeDtypeStruct((n_seq * seq_len, cd), BF16),
                   jax.ShapeDtypeStruct((n_seq, hist, cd), F32)],
        scratch_shapes=[pltpu.VMEM((ext_rows, cd), F32), pltpu.VMEM((rows, cd), F32)],
        compiler_params=_params(("parallel",), 32),
        name="conv_sample",
    )(ug, buf, w, wb.reshape(1, cd), g.reshape(1, cd), b.reshape(1, cd))


def _attn_kernel(q_ref, k_ref, v_ref, o_ref, *, heads, seq_rows, scale):
    rows, ca = q_ref.shape
    nseq = rows // seq_rows
    n_mem = k_ref.shape[0] // nseq
    hd = ca // heads
    q = q_ref[...]
    if nseq > 1:
        shift = seq_rows.bit_length() - 1
        row_seq = lax.broadcasted_iota(jnp.int32, (rows, 1), 0) >> shift
    for h in range(heads):
        hs = slice(h * hd, (h + 1) * hd)
        out = None
        for s in range(nseq):
            ms = slice(s * n_mem, (s + 1) * n_mem)
            sc = _dot_nt(q[:, hs], k_ref[ms, hs].astype(BF16)) * scale
            e = jnp.exp(sc - jnp.max(sc, axis=-1, keepdims=True))
            p = e / jnp.sum(e, axis=-1, keepdims=True)
            o = _dot(p.astype(BF16), v_ref[ms, hs].astype(BF16))
            out = o if out is None else jnp.where(row_seq == s, o, out)
        o_ref[:, hs] = out.astype(o_ref.dtype)


def _attn(q, k, v, *, row0, n_seq, seq_len, n_mem, heads, name):
    ca = q.shape[1]
    if seq_len % SUBLANES == 0:
        rows, nseq = _pick(seq_len, 256, 16), 1
    else:
        assert SUBLANES % seq_len == 0
        rows, nseq = 16, 16 // seq_len
    assert row0 % rows == 0 and (n_seq * seq_len) % rows == 0
    blk0 = row0 // rows
    per_seq = max(seq_len // rows, 1)
    kv_spec = pl.BlockSpec((nseq * n_mem, ca), lambda i: (i // per_seq, 0))
    return pl.pallas_call(
        functools.partial(_attn_kernel, heads=heads, seq_rows=seq_len if nseq > 1 else rows,
                          scale=(ca // heads) ** -0.5),
        grid=(n_seq * seq_len // rows,),
        in_specs=[pl.BlockSpec((rows, ca), lambda i: (blk0 + i, 0)), kv_spec, kv_spec],
        out_specs=pl.BlockSpec((rows, ca), lambda i: (i, 0)),
        out_shape=jax.ShapeDtypeStruct((n_seq * seq_len, ca), BF16),
        compiler_params=_params(("parallel",), 32),
        name=name,
    )(q, k, v)


ROUTE_E0, ROUTE_E1, ROUTE_G0, ROUTE_G1 = 0, 1, 2, 3
GATHER_UNROLL = 8
HIGH_HALF = 0xFFFF0000


def _token_stride(chunks):
    return chunks + SUBLANES


def _chunk(c, t0, n, chunks):
    stride = _token_stride(chunks)
    return pl.ds(t0 * stride + c, n, stride=stride), slice(None)


def _token_copies(hbm_ref, hbm_token, buf_ref, t0, n_tokens, chunks, sem, to_hbm):
    stride = _token_stride(chunks)

    def pair(hbm_rows, buf_rows):
        return (buf_rows, hbm_rows) if to_hbm else (hbm_rows, buf_rows)

    def start():
        def body(j, carry):
            for k in range(GATHER_UNROLL):
                r = j * GATHER_UNROLL + k
                src, dst = pair(hbm_ref.at[pl.ds(hbm_token(r) * chunks, chunks), :],
                                buf_ref.at[pl.ds((t0 + r) * stride, chunks), :])
                pltpu.make_async_copy(src, dst, sem).start()
            return carry

        lax.fori_loop(0, n_tokens // GATHER_UNROLL, body, 0)

    def wait():
        src, dst = pair(hbm_ref.at[pl.ds(0, n_tokens * chunks), :],
                        buf_ref.at[pl.ds(t0 * stride, n_tokens * chunks), :])
        pltpu.make_async_copy(src, dst, sem).wait()

    return start, wait


def _router_kernel(x_ref, g_ref, w_ref, b_ref, o_ref, htok_hbm, tokbuf, sem, *, n_groups, per_group):
    tm, d = x_ref.shape
    words = d // (2 * LANES)
    h = _rmsnorm_rows(x_ref[...], g_ref[...])
    bits = pltpu.bitcast(h.astype(BF16).astype(F32), jnp.uint32)
    for c in range(words):
        low = bits[:, 2 * c * LANES:(2 * c + 1) * LANES]
        high = bits[:, (2 * c + 1) * LANES:(2 * c + 2) * LANES]
        tokbuf[_chunk(c, 0, tm, words)] = (low >> 16) | (high & jnp.uint32(HIGH_HALF))
    row0 = pl.program_id(0) * tm
    start_tok_copy, wait_tok_copy = _token_copies(htok_hbm, lambda r: row0 + r, tokbuf, 0, tm, words, sem,
                                                  to_hbm=True)
    start_tok_copy()
    h_hi, h_lo = _split_bf16(h, 2)

    w_hi, w_lo = _split_bf16(w_ref[...], 2)
    logits = _dot(h_hi, w_hi) + (_dot(h_hi, w_lo) + _dot(h_lo, w_hi)) + b_ref[...]
    n_experts = n_groups * per_group
    lane_i = lax.broadcasted_iota(jnp.int32, logits.shape, 1)
    lane = lane_i.astype(F32)
    neg = jnp.float32(-jnp.inf)

    def first_argmax(x, m):
        return jnp.min(jnp.where(x == m, lane, float(LANES)), axis=-1, keepdims=True)

    lg = jnp.where(jnp.logical_and(lane_i >= n_experts, lane_i < n_experts + n_groups), logits, neg)
    mg = jnp.max(lg, axis=-1, keepdims=True)
    pg_top = 1.0 / jnp.sum(jnp.exp(lg - mg), axis=-1, keepdims=True)
    gsel = first_argmax(lg, mg) - float(n_experts)
    shift = per_group.bit_length() - 1
    in_group = jnp.logical_and(lane_i < n_experts, (lane_i >> shift).astype(F32) == gsel)
    le = jnp.where(in_group, logits, neg)
    m0 = jnp.max(le, axis=-1, keepdims=True)
    z = jnp.sum(jnp.exp(le - m0), axis=-1, keepdims=True)
    e0 = first_argmax(le, m0)
    le1 = jnp.where(lane == e0, neg, le)
    m1 = jnp.max(le1, axis=-1, keepdims=True)
    e1 = first_argmax(le1, m1)
    p0 = 1.0 / z
    p1 = jnp.exp(m1 - m0) / z
    den = p0 + p1
    g0 = pg_top * (p0 / den)
    g1 = pg_top * (p1 / den)
    rec = jnp.where(lane_i == ROUTE_E0, e0, 0.0)
    rec = jnp.where(lane_i == ROUTE_E1, e1, rec)
    rec = jnp.where(lane_i == ROUTE_G0, g0, rec)
    rec = jnp.where(lane_i == ROUTE_G1, g1, rec)
    o_ref[...] = rec
    wait_tok_copy()


def _router(x, g, w_group, b_group, w_expert, b_expert):
    t, d = x.shape
    n_groups, n_experts = w_group.shape[1], w_expert.shape[1]
    per_group = n_experts // n_groups
    assert n_experts + n_groups <= LANES and per_group & (per_group - 1) == 0
    pad = lambda a: jnp.pad(a, ((0, 0), (0, LANES - a.shape[1])))
    w_both = pad(jnp.concatenate([w_expert, w_group], axis=1))
    b_both = pad(jnp.concatenate([b_expert, b_group]).reshape(1, -1))
    words = d // (2 * LANES)
    tm = _pick(t, 256, GATHER_UNROLL)
    return pl.pallas_call(
        functools.partial(_router_kernel, n_groups=n_groups, per_group=per_group),
        grid=(t // tm,),
        in_specs=[pl.BlockSpec((tm, d), lambda i: (i, 0)), pl.BlockSpec((1, d), lambda i: (0, 0)),
                  pl.BlockSpec((d, LANES), lambda i: (0, 0)), pl.BlockSpec((1, LANES), lambda i: (0, 0))],
        out_specs=[pl.BlockSpec((tm, LANES), lambda i: (i, 0)), pl.BlockSpec(memory_space=pl.ANY)],
        out_shape=[jax.ShapeDtypeStruct((t, LANES), F32), jax.ShapeDtypeStruct((t * words, LANES), jnp.uint32)],
        scratch_shapes=[pltpu.VMEM((tm * _token_stride(words), LANES), jnp.uint32), pltpu.SemaphoreType.DMA(())],
        compiler_params=_params(("parallel",), 32),
        name="router",
    )(x, g.reshape(1, d), w_both, b_both)


def _dispatch_plan(e_ids, n_experts, tile):
    t, k = e_ids.shape
    n_pairs = t * k
    n_tiles = n_pairs // tile + n_experts
    flat = e_ids.reshape(n_pairs)
    onehot = (flat[:, None] == jnp.arange(n_experts, dtype=jnp.int32)[None, :]).astype(jnp.int32)
    counts = jnp.sum(onehot, axis=0)
    tiles_per = (counts + tile - 1) // tile
    tile_end = jnp.cumsum(tiles_per)
    tile_start = tile_end - tiles_per
    rank = jnp.take_along_axis(jnp.cumsum(onehot, axis=0), flat[:, None], axis=1)[:, 0] - 1
    pos = tile_start[flat] * tile + rank
    token = jnp.arange(n_pairs, dtype=jnp.int32) // k
    src = jnp.zeros((n_tiles * tile,), jnp.int32).at[pos].set(token)
    n_used = tile_end[-1]
    tile_ids = jnp.minimum(jnp.arange(n_tiles, dtype=jnp.int32), n_used - 1)
    tile_expert = jnp.searchsorted(tile_end, tile_ids, side="right").astype(jnp.int32)
    return pos.reshape(t, k), src.reshape(n_tiles, 1, tile), tile_expert, n_used.reshape(1).astype(jnp.int32)


DOWN_COLS = 1024


def _experts_kernel(texp_ref, nused_ref, idx_ref, idx_next_ref, h_hbm, wg_ref, wu_ref, wd_ref,
                    y_hbm, xbuf, ybuf, hbuf, gsem, ysem):
    i = pl.program_id(0)
    n_used = nused_ref[0]
    tile, d = hbuf.shape
    chunks = d // LANES
    words = chunks // 2
    slot = lax.rem(i, 2)
    wg_bf, wu_bf, wd_bf = wg_ref.at[0], wu_ref.at[0], wd_ref.at[0]
    start_tokens, wait_tokens = _token_copies(h_hbm, lambda r: idx_ref[0, 0, r], xbuf, slot * tile, tile, words,
                                              gsem.at[slot], to_hbm=False)
    start_next_tokens, _ = _token_copies(h_hbm, lambda r: idx_next_ref[0, 0, r], xbuf, (1 - slot) * tile, tile,
                                         words, gsem.at[1 - slot], to_hbm=False)

    def y_copy(step):
        return _token_copies(y_hbm, lambda r: step * tile + r, ybuf, 0, tile, chunks, ysem, to_hbm=True)

    @pl.when(i < n_used)
    def _():
        pl.when(i == 0)(start_tokens)
        pl.when(i + 1 < n_used)(start_next_tokens)
        wait_tokens()
        for c in range(words):
            word = xbuf[_chunk(c, slot * tile, tile, words)]
            low = pltpu.bitcast(word << 16, F32)
            high = pltpu.bitcast(word & jnp.uint32(HIGH_HALF), F32)
            hbuf[:, 2 * c * LANES:(2 * c + 1) * LANES] = low.astype(BF16)
            hbuf[:, (2 * c + 1) * LANES:(2 * c + 2) * LANES] = high.astype(BF16)
        h = hbuf[...]
        a = _dot(h, wg_bf[...])
        u = _dot(h, wu_bf[...])
        hid = (_silu(a) * u).astype(BF16)

        @pl.when(i > 0)
        def _():
            y_copy(i - 1)[1]()

        for c0 in range(0, d, DOWN_COLS):
            y = _dot(hid, wd_bf[:, c0:c0 + DOWN_COLS])
            for c in range(DOWN_COLS // LANES):
                ybuf[_chunk(c0 // LANES + c, 0, tile, chunks)] = y[:, c * LANES:(c + 1) * LANES]

    @pl.when(i >= n_used)
    def _():
        y_copy(i - 1)[1]()

        @pl.when(i == n_used)
        def _():
            ybuf[...] = jnp.zeros(ybuf.shape, F32)

    y_copy(i)[0]()

    @pl.when(i == pl.num_programs(0) - 1)
    def _():
        y_copy(i)[1]()


def _experts(h_tok, src, tile_expert, n_used, wg, wu, wd):
    d = wg.shape[1]
    chunks = d // LANES
    words = chunks // 2
    n_tiles, _, tile = src.shape
    de = wg.shape[2]
    assert d % 512 == 0 and de % 64 == 0 and d % DOWN_COLS == 0 and tile % GATHER_UNROLL == 0
    idx_spec = lambda ahead: pl.BlockSpec(
        (1, 1, tile), lambda i, texp, nused: (jnp.minimum(i + ahead, nused[0] - 1), 0, 0),
        memory_space=pltpu.SMEM)
    hbm = pl.BlockSpec(memory_space=pl.ANY)
    grid_spec = pltpu.PrefetchScalarGridSpec(
        num_scalar_prefetch=2,
        grid=(n_tiles,),
        in_specs=[idx_spec(0), idx_spec(1), hbm,
                  pl.BlockSpec((1, d, de), lambda i, texp, nused: (texp[i], 0, 0)),
                  pl.BlockSpec((1, d, de), lambda i, texp, nused: (texp[i], 0, 0)),
                  pl.BlockSpec((1, de, d), lambda i, texp, nused: (texp[i], 0, 0))],
        out_specs=hbm,
        scratch_shapes=[pltpu.VMEM((2 * tile * _token_stride(words), LANES), jnp.uint32),
                        pltpu.VMEM((tile * _token_stride(chunks), LANES), F32), pltpu.VMEM((tile, d), BF16),
                        pltpu.SemaphoreType.DMA((2,)), pltpu.SemaphoreType.DMA(())],
    )
    weights_bytes = 2 * 3 * d * de * 2
    tiles_bytes = tile * (2 * _token_stride(words) + _token_stride(chunks)) * LANES * 4
    tiles_bytes += tile * d * 2 + tile * DOWN_COLS * 4 + 3 * tile * de * 4
    return pl.pallas_call(
        _experts_kernel,
        grid_spec=grid_spec,
        out_shape=jax.ShapeDtypeStruct((n_tiles * tile * chunks, LANES), F32),
        compiler_params=_params(("arbitrary",), (weights_bytes + tiles_bytes) // MIB + 3),
        name="experts",
    )(tile_expert, n_used, src, src, h_tok, wg, wu, wd)


def _combine_kernel(idx0_ref, idx1_ref, idx0_next_ref, idx1_next_ref, y_hbm, x_ref, route_ref, gf_ref, *refs,
                    n_first, final_norm):
    outs, (buf0, buf1, res_ref, sem) = refs[:-4], refs[-4:]
    i = pl.program_id(0)
    tm, d = x_ref.shape
    chunks = d // LANES
    slot = lax.rem(i, 2)
    def gather(idx_ref, buf, k, into_slot):
        return _token_copies(y_hbm, lambda r: idx_ref[0, 0, r], buf, into_slot * tm, tm, chunks,
                             sem.at[k, into_slot], to_hbm=False)

    gathers = [gather(idx0_ref, buf0, 0, slot), gather(idx1_ref, buf1, 1, slot)]
    gathers_next = [gather(idx0_next_ref, buf0, 0, 1 - slot), gather(idx1_next_ref, buf1, 1, 1 - slot)]

    @pl.when(i == 0)
    def _():
        for start, _ in gathers:
            start()

    @pl.when(i + 1 < pl.num_programs(0))
    def _():
        for start, _ in gathers_next:
            start()

    for _, wait in gathers:
        wait()
    g0 = route_ref[:, ROUTE_G0:ROUTE_G0 + 1]
    g1 = route_ref[:, ROUTE_G1:ROUTE_G1 + 1]
    ssq = jnp.zeros((tm, LANES), F32)
    for c in range(chunks):
        lanes = slice(c * LANES, (c + 1) * LANES)
        y0 = buf0[_chunk(c, slot * tm, tm, chunks)]
        y1 = buf1[_chunk(c, slot * tm, tm, chunks)]
        r = x_ref[:, lanes] + (g0 * y0 + g1 * y1)
        res_ref[:, lanes] = r
        ssq = ssq + r * r
    if final_norm:
        scale = lax.rsqrt(jnp.sum(ssq, axis=-1, keepdims=True) * (1.0 / d) + EPS)
        result = lambda: res_ref[...] * scale * gf_ref[...]
    else:
        result = lambda: res_ref[...]
    if len(outs) == 1:
        outs[0][...] = result()
    else:
        @pl.when(i < n_first)
        def _():
            outs[0][...] = result()

        @pl.when(i >= n_first)
        def _():
            outs[1][...] = result()


def _combine(x, y_sorted, pos, route, g_final, *, split_rows, final_norm):
    t, d = x.shape
    assert y_sorted.shape[1] == LANES
    tm = _pick(t if split_rows is None else math.gcd(split_rows, t - split_rows), 128, GATHER_UNROLL)
    n_blk = t // tm
    idx = [pos[:, k].reshape(n_blk, 1, tm) for k in range(TOP_K_IN_GROUP)]
    smem = pl.BlockSpec((1, 1, tm), lambda i: (i, 0, 0), memory_space=pltpu.SMEM)
    smem_next = pl.BlockSpec((1, 1, tm), lambda i: (jnp.minimum(i + 1, n_blk - 1), 0, 0), memory_space=pltpu.SMEM)
    row = pl.BlockSpec((tm, d), lambda i: (i, 0))
    if split_rows is None:
        n_first = n_blk
        out_specs = [row]
        out_shape = [jax.ShapeDtypeStruct((t, d), F32)]
    else:
        n_first = split_rows // tm
        out_specs = [pl.BlockSpec((tm, d), lambda i: (jnp.minimum(i, n_first - 1), 0)),
                     pl.BlockSpec((tm, d), lambda i: (jnp.maximum(i - n_first, 0), 0))]
        out_shape = [jax.ShapeDtypeStruct((split_rows, d), F32), jax.ShapeDtypeStruct((t - split_rows, d), F32)]
    return pl.pallas_call(
        functools.partial(_combine_kernel, n_first=n_first, final_norm=final_norm),
        grid=(n_blk,),
        in_specs=[smem, smem, smem_next, smem_next, pl.BlockSpec(memory_space=pl.ANY), row,
                  pl.BlockSpec((tm, LANES), lambda i: (i, 0)), pl.BlockSpec((1, d), lambda i: (0, 0))],
        out_specs=out_specs,
        out_shape=out_shape,
        scratch_shapes=[pltpu.VMEM((2 * tm * _token_stride(d // LANES), LANES), F32),
                        pltpu.VMEM((2 * tm * _token_stride(d // LANES), LANES), F32), pltpu.VMEM((tm, d), F32),
                        pltpu.SemaphoreType.DMA((TOP_K_IN_GROUP, 2))],
        compiler_params=_params(("arbitrary",), 32),
        name="combine",
    )(idx[0], idx[1], idx[0], idx[1], y_sorted, x, route, g_final.reshape(1, d))


def kernel(x_prompt, x_sample, mem_prompt, state_gla, state_conv, cache_mem_k, cache_mem_v, norm_mix_g, w_in, w_alpha_up, b_alpha, gla_norm_g, w_branch_a, conv_dw_w, conv_dw_b, conv_ln_g, conv_ln_b, w_branch_b, w_out, norm_ca_g, norm_mem_g, w_ca_q, w_ca_k, w_ca_v, w_ca_o, norm_ffn_g, w_router_group, b_router_group, w_router_expert, b_router_expert, w_exp_gate, w_exp_up, w_exp_down, norm_final_g):
    depth = w_in.shape[0]
    bp, tp, d = x_prompt.shape
    bs, ts, _ = x_sample.shape
    heads, dk, dv = state_gla.shape[2:]
    key, val = heads * dk, heads * dv
    rank = w_alpha_up.shape[1]
    cd = state_conv.shape[3]
    n_mem, ca_heads, ca_hd = cache_mem_k.shape[2:]
    ca = ca_heads * ca_hd
    n_experts = w_router_expert.shape[2]
    rows_p, rows_s = bp * tp, bs * ts
    off_a = 2 * key + 2 * val
    off_u = off_a + rank
    off_g = off_u + 2 * cd
    assert rank <= LANES and w_in.shape[2] == off_g + 2 * d
    bf = lambda a: a.astype(BF16)

    x = (x_prompt.reshape(rows_p, d), x_sample.reshape(rows_s, d))
    outs = dict(gla_p=[], conv_p=[], mk_p=[], mv_p=[], gla_s=[], conv_s=[])
    for l in range(depth):
        w_in_t = jnp.swapaxes(w_in, 1, 2)[l]
        w_qkvr = _cast_weight(w_in_t, 0, off_a)
        w_a = bf(jnp.pad(w_in[l, :, off_a:off_u], ((0, 0), (0, LANES - rank))))
        h = _rmsnorm(x, norm_mix_g[l], BF16)
        dense = (w_branch_a, w_branch_b, w_out, w_ca_q, w_ca_k, w_ca_v, w_ca_o)
        z, (w_ug, wa_bf, wb_bf, wo_bf, wcq_bf, wck_bf, wcv_bf, wco_bf) = _mm(
            [h], [w_qkvr], [0], _ep_identity, F32, w_rows=[True], tm_pref=WIDE_ROWS,
            side_cast=[(w_in_t, off_u, 2 * cd + 2 * d)] + [(w[l], 0, w.shape[1]) for w in dense], name="in_qkvr")
        a_low = _mm([h], [w_a], [0], _ep_identity, F32, name="in_alow")
        ug = _mm([h], [w_ug, w_ug], [0, 0], _ep_glu, F32, n=cd, w_col0=[0, cd], w_rows=[True, True],
                 tm_pref=WIDE_ROWS, name="in_glu")
        w_up = bf(jnp.pad(w_alpha_up[l], ((0, LANES - rank), (0, 0))))
        gla_args = dict(heads=heads, dk=dk, dv=dv)
        og_p, st_p = _gla(z, a_low, w_up, b_alpha[l], gla_norm_g[l], None, row0=0, n_seq=bp, seq_len=tp,
                          name="gla_prompt", **gla_args)
        og_s, st_s = _gla(z, a_low, w_up, b_alpha[l], gla_norm_g[l], state_gla[l], row0=rows_p, n_seq=bs,
                          seq_len=ts, name="gla_sample", **gla_args)
        conv_w = (conv_dw_w[l], conv_dw_b[l], conv_ln_g[l], conv_ln_b[l])
        c_p = _conv_prompt(ug, *conv_w, n_seq=bp, seq_len=tp)
        c_s, buf_s = _conv_sample(ug, state_conv[l], *conv_w, row0=rows_p, seq_len=ts)
        og = (og_p, og_s)
        c = (c_p, c_s)
        flat = lambda w: w[l].reshape(-1, w.shape[-1])
        expert_casts = [(flat(w), 0, w.shape[1] * w.shape[2]) for w in (w_exp_gate, w_exp_up, w_exp_down)]
        merged, (wg_bf, wu_bf, wd_bf) = _mm(
            [og, c, h], [w_ug, w_ug, wa_bf, wb_bf], [2, 2, 0, 1], _ep_gated_merge, BF16,
            n=d, w_col0=[2 * cd, 2 * cd + d, 0, 0], w_rows=[True, True, False, False], tn_pref=256,
            side_cast=expert_casts, name="merge")
        x = _mm([merged], [wo_bf], [0], _ep_residual, F32, extras=[x], name="out_proj")
        q = _mm([x], [wcq_bf], [0], _ep_identity, BF16, norm_gain=norm_ca_g[l], name="ca_q")
        m = _rmsnorm(mem_prompt.reshape(bp * n_mem, d), norm_mem_g[l], BF16)
        mk = _mm([m], [wck_bf], [0], _ep_identity, F32, name="mem_k")
        mv = _mm([m], [wcv_bf], [0], _ep_identity, F32, name="mem_v")
        ao_p = _attn(q, mk, mv, row0=0, n_seq=bp, seq_len=tp, n_mem=n_mem, heads=ca_heads, name="attn_prompt")
        ao_s = _attn(q, cache_mem_k[l].reshape(bs * n_mem, ca), cache_mem_v[l].reshape(bs * n_mem, ca),
                     row0=rows_p, n_seq=bs, seq_len=ts, n_mem=n_mem, heads=ca_heads, name="attn_sample")
        ao = (ao_p, ao_s)
        x = _mm([ao], [wco_bf], [0], _ep_residual, F32, extras=[x], name="ca_out")
        route, h_tok = _router(x, norm_ffn_g[l], w_router_group[l], b_router_group[l], w_router_expert[l],
                               b_router_expert[l])
        e_ids = route[:, ROUTE_E0:ROUTE_E1 + 1].astype(jnp.int32)
        tile = _pick(rows_p + rows_s, 256, SUBLANES)
        pos, src, tile_expert, n_used = _dispatch_plan(e_ids, n_experts, tile)
        y_sorted = _experts(h_tok, src, tile_expert, n_used, wg_bf.reshape(w_exp_gate.shape[1:]),
                            wu_bf.reshape(w_exp_up.shape[1:]), wd_bf.reshape(w_exp_down.shape[1:]))
        last = l == depth - 1
        res = _combine(x, y_sorted, pos, route, norm_final_g, split_rows=rows_p if last else None,
                       final_norm=last)
        if not last:
            x = res[0]
        outs["gla_p"].append(st_p)
        hist = conv_dw_w.shape[1] - 1
        outs["conv_p"].append(jnp.stack([ug[(b + 1) * tp - hist:(b + 1) * tp] for b in range(bp)]))
        outs["mk_p"].append(mk.reshape(bp, n_mem, ca_heads, ca_hd))
        outs["mv_p"].append(mv.reshape(bp, n_mem, ca_heads, ca_hd))
        outs["gla_s"].append(st_s)
        outs["conv_s"].append(buf_s)
    y_prompt = res[0].reshape(bp, tp, d)
    y_sample = res[1].reshape(bs, ts, d)
    return (y_prompt, y_sample, jnp.stack(outs["gla_p"]), jnp.stack(outs["conv_p"]), jnp.stack(outs["mk_p"]),
            jnp.stack(outs["mv_p"]), jnp.stack(outs["gla_s"]), jnp.stack(outs["conv_s"]))
```

```python
import functools
import math

import jax
import jax.numpy as jnp
from jax import lax
from jax.experimental import pallas as pl
from jax.experimental.pallas import tpu as pltpu

F32 = jnp.float32
BF16 = jnp.bfloat16

EPS = 1e-6
GLA_TAU = 16.0
GLA_CHUNK = 64
TOP_K_IN_GROUP = 2
LANES = 128
SUBLANES = 8
MIB = 1 << 20


def _pick(n, pref, mult):
    for d in range(min(pref, n), 0, -1):
        if n % d == 0 and d % mult == 0:
            return d
    return n


def _params(semantics, vmem_mib):
    return pltpu.CompilerParams(dimension_semantics=semantics, vmem_limit_bytes=vmem_mib * MIB)


def _sigmoid(x):
    return 1.0 / (1.0 + jnp.exp(-x))


def _silu(x):
    return x * _sigmoid(x)


def _dot(a, b):
    return jnp.dot(a, b, preferred_element_type=F32)


def _dot_nt(a, b):
    return lax.dot_general(a, b, (((1,), (1,)), ((), ())), preferred_element_type=F32)


def _dot_tn(a, b):
    return lax.dot_general(a, b, (((0,), (0,)), ((), ())), preferred_element_type=F32)


def _split_bf16(x, parts):
    out = []
    for _ in range(parts - 1):
        p = x.astype(BF16)
        out.append(p)
        x = x - p.astype(F32)
    out.append(x.astype(BF16))
    return out


def _rmsnorm_rows(x, g):
    return x * lax.rsqrt(jnp.mean(x * x, axis=-1, keepdims=True) + EPS) * g


def _parts(a):
    return a if isinstance(a, tuple) else (a,)


def _rows_of(a):
    return sum(p.shape[0] for p in _parts(a))


def _part_specs(a, tm, cols, col_of):
    parts = _parts(a)
    if len(parts) == 1:
        return [pl.BlockSpec((tm, cols), lambda i, *j: (i, col_of(*j)))]
    n_first = parts[0].shape[0] // tm
    assert parts[0].shape[0] % tm == 0 and parts[1].shape[0] % tm == 0
    return [pl.BlockSpec((tm, cols), lambda i, *j: (jnp.minimum(i, n_first - 1), col_of(*j))),
            pl.BlockSpec((tm, cols), lambda i, *j: (jnp.maximum(i - n_first, 0), col_of(*j)))]


def _part_tile(a, pref, mult):
    return _pick(math.gcd(*[p.shape[0] for p in _parts(a)]), pref, mult)


def _select_part(refs, n_first):
    if len(refs) == 1:
        return refs[0][...]
    return jnp.where(pl.program_id(0) < n_first, refs[0][...], refs[1][...])


def _rmsnorm_kernel(*refs, n_first):
    x_refs, (g_ref, o_ref) = refs[:-2], refs[-2:]
    o_ref[...] = _rmsnorm_rows(_select_part(x_refs, n_first), g_ref[...]).astype(o_ref.dtype)


def _rmsnorm(x, g, out_dtype):
    t, d = _rows_of(x), _parts(x)[0].shape[1]
    tm = _part_tile(x, 256, 16)
    return pl.pallas_call(
        functools.partial(_rmsnorm_kernel, n_first=_parts(x)[0].shape[0] // tm),
        grid=(t // tm,),
        in_specs=_part_specs(x, tm, d, lambda: 0) + [pl.BlockSpec((1, d), lambda i: (0, 0))],
        out_specs=pl.BlockSpec((tm, d), lambda i: (i, 0)),
        out_shape=jax.ShapeDtypeStruct((t, d), out_dtype),
        compiler_params=_params(("parallel",), 32),
        name="rmsnorm",
    )(*_parts(x), g.reshape(1, d))


def _cast_weight_kernel(w_hbm, o_ref, buf, sem, *, row0):
    j = pl.program_id(0)
    tn = o_ref.shape[0]
    slot = lax.rem(j, 2)

    def copy(block, into):
        rows = pl.ds(pl.multiple_of(row0 + block * tn, SUBLANES), tn)
        return pltpu.make_async_copy(w_hbm.at[rows, :], buf.at[into], sem.at[into])

    @pl.when(j == 0)
    def _():
        copy(j, slot).start()

    @pl.when(j + 1 < pl.num_programs(0))
    def _():
        copy(j + 1, 1 - slot).start()

    copy(j, slot).wait()
    o_ref[...] = buf[slot].astype(BF16)


def _cast_weight(w, row0, n_rows):
    _, k = w.shape
    tn = _pick(n_rows, 256, 16)
    assert row0 % SUBLANES == 0 and n_rows % tn == 0
    return pl.pallas_call(
        functools.partial(_cast_weight_kernel, row0=row0),
        grid=(n_rows // tn,),
        in_specs=[pl.BlockSpec(memory_space=pl.ANY)],
        out_specs=pl.BlockSpec((tn, k), lambda j: (j, 0)),
        out_shape=jax.ShapeDtypeStruct((n_rows, k), BF16),
        scratch_shapes=[pltpu.VMEM((2, tn, k), F32), pltpu.SemaphoreType.DMA((2,))],
        compiler_params=_params(("arbitrary",), 32),
        name="cast_weight",
    )(w)


def _ep_identity(accs, extras):
    return accs[0]


def _ep_glu(accs, extras):
    return accs[0] * _sigmoid(accs[1])


def _ep_residual(accs, extras):
    return extras[0] + accs[0]


def _ep_gated_merge(accs, extras):
    return _sigmoid(accs[0]) * accs[2] + _sigmoid(accs[1]) * accs[3]


WIDE_ROWS = 1088


def _side_cast_step(step, n_chunks, src_hbm, dst_hbm, row0, in_buf, out_buf, in_sem, out_sem):
    rows = in_buf.shape[1]
    slot = lax.rem(step, 2)

    def read(chunk, into):
        start = pl.multiple_of(row0 + chunk * rows, SUBLANES)
        return pltpu.make_async_copy(src_hbm.at[pl.ds(start, rows), :], in_buf.at[into], in_sem.at[into])

    def write(chunk):
        start = pl.multiple_of(chunk * rows, 16)
        return pltpu.make_async_copy(out_buf, dst_hbm.at[pl.ds(start, rows), :], out_sem)

    @pl.when(step < n_chunks)
    def _():
        pl.when(step == 0)(lambda: read(step, slot).start())
        pl.when(step + 1 < n_chunks)(lambda: read(step + 1, 1 - slot).start())
        read(step, slot).wait()
        pl.when(step > 0)(lambda: write(step - 1).wait())
        out_buf[...] = in_buf[slot].astype(BF16)
        write(step).start()
        pl.when(step == n_chunks - 1)(lambda: write(step).wait())


def _mm_kernel(*refs, x_of_w, w_rows, x_parts, e_parts, n_first, has_norm, side, epilogue):
    refs = list(refs)
    take = lambda k: [refs.pop(0) for _ in range(k)]
    x_refs = [take(p) for p in x_parts]
    g_ref = refs.pop(0) if has_norm else None
    w_refs = take(len(x_of_w))
    e_refs = [take(p) for p in e_parts]
    side_src = take(len(side))
    o_ref = refs.pop(0)
    side_dst = take(len(side))
    i = pl.program_id(0)
    step = i * pl.num_programs(1) + pl.program_id(1)
    side_scratch = [refs[len(refs) - 4 * len(side) + 4 * k:len(refs) - 4 * len(side) + 4 * k + 4]
                    for k in range(len(side))]
    for (row0, n_chunks), src, dst, (in_buf, out_buf, in_sem, out_sem) in zip(side, side_src, side_dst, side_scratch):
        _side_cast_step(step, n_chunks, src, dst, row0, in_buf, out_buf, in_sem, out_sem)
    first_col = pl.program_id(1) == 0
    xs = []
    for k, parts in enumerate(x_refs):
        norm = has_norm and k == 0
        if len(parts) == 1 and not norm:
            xs.append(parts[0])
            continue
        scratch = refs.pop(0)

        def fill(src, scratch=scratch, norm=norm):
            val = src[...]
            if norm:
                val = _rmsnorm_rows(val, g_ref[...])
            scratch[...] = val.astype(scratch.dtype)

        if len(parts) == 1:
            pl.when(first_col)(functools.partial(fill, parts[0]))
        else:
            pl.when(jnp.logical_and(first_col, i < n_first))(functools.partial(fill, parts[0]))
            pl.when(jnp.logical_and(first_col, i >= n_first))(functools.partial(fill, parts[1]))
        xs.append(scratch)
    accs = [(_dot_nt if w_rows[k] else _dot)(xs[x_of_w[k]][...], w_refs[k][...]) for k in range(len(x_of_w))]
    o_ref[...] = epilogue(accs, [_select_part(e, n_first) for e in e_refs]).astype(o_ref.dtype)


def _mm(xs, ws, x_of_w, epilogue, out_dtype, extras=(), tm_pref=512, tn_pref=512, name="mm", n=None, w_col0=None,
        w_rows=None, norm_gain=None, side_cast=()):
    t = _rows_of(xs[0])
    w_rows = [False] * len(ws) if w_rows is None else w_rows
    w_col0 = [0] * len(ws) if w_col0 is None else w_col0
    n = ws[0].shape[0 if w_rows[0] else 1] if n is None else n
    tm = min(_part_tile(a, tm_pref, 16) for a in list(xs) + list(extras))
    tn = _pick(math.gcd(n, *w_col0), tn_pref, LANES)
    split = [a for a in list(xs) + list(extras) if len(_parts(a)) == 2]
    assert all(a[0].shape[0] == split[0][0].shape[0] and a[0].shape[0] % tm == 0 for a in split)
    n_first = split[0][0].shape[0] // tm if split else 0
    k_of = lambda x: _parts(x)[0].shape[1]
    in_specs, args = [], []
    for x in xs:
        in_specs += _part_specs(x, tm, k_of(x), lambda j: 0)
        args += _parts(x)
    if norm_gain is not None:
        in_specs.append(pl.BlockSpec((1, k_of(xs[0])), lambda i, j: (0, 0)))
        args.append(norm_gain.reshape(1, -1))
    for w, c0, rows in zip(ws, w_col0, w_rows):
        if rows:
            in_specs.append(pl.BlockSpec((tn, w.shape[1]), lambda i, j, b0=c0 // tn: (b0 + j, 0)))
        else:
            in_specs.append(pl.BlockSpec((w.shape[0], tn), lambda i, j, b0=c0 // tn: (0, b0 + j)))
    args += ws
    for e in extras:
        in_specs += _part_specs(e, tm, tn, lambda j: j)
        args += _parts(e)
    staged = [x for k, x in enumerate(xs) if len(_parts(x)) == 2 or (norm_gain is not None and k == 0)]
    scratch = [pltpu.VMEM((tm, k_of(x)), BF16) for x in staged]
    steps = (t // tm) * (n // tn)
    side, side_out_shape, side_bytes = [], [], 0
    for src, row0, n_rows in side_cast:
        rows = next(r for r in range(16, n_rows + 1, 16) if n_rows % r == 0 and n_rows // r <= steps)
        assert row0 % SUBLANES == 0
        side.append((row0, n_rows // rows))
        in_specs.append(pl.BlockSpec(memory_space=pl.ANY))
        args.append(src)
        side_out_shape.append(jax.ShapeDtypeStruct((n_rows, src.shape[1]), BF16))
        scratch += [pltpu.VMEM((2, rows, src.shape[1]), F32), pltpu.VMEM((rows, src.shape[1]), BF16),
                    pltpu.SemaphoreType.DMA((2,)), pltpu.SemaphoreType.DMA(())]
        side_bytes += rows * src.shape[1] * (2 * 4 + 2)
    block_bytes = sum(tm * p.shape[1] * p.dtype.itemsize for x in xs for p in _parts(x))
    block_bytes += sum(w.size // (w.shape[0 if rows else 1] // tn) * w.dtype.itemsize for w, rows in zip(ws, w_rows))
    block_bytes += sum(tm * tn * p.dtype.itemsize for e in extras for p in _parts(e))
    block_bytes += tm * tn * jnp.dtype(out_dtype).itemsize
    scratch_bytes = sum(tm * k_of(x) * 2 for x in staged)
    norm_bytes = 2 * tm * k_of(xs[0]) * 4 if norm_gain is not None else 0
    acc_bytes = len(ws) * tm * tn * 4
    vmem = (2 * block_bytes + 2 * acc_bytes + scratch_bytes + norm_bytes + side_bytes) // MIB + 4
    outs = pl.pallas_call(
        functools.partial(_mm_kernel, x_of_w=tuple(x_of_w), w_rows=tuple(w_rows),
                          x_parts=tuple(len(_parts(x)) for x in xs), e_parts=tuple(len(_parts(e)) for e in extras),
                          n_first=n_first, has_norm=norm_gain is not None, side=tuple(side), epilogue=epilogue),
        grid=(t // tm, n // tn),
        in_specs=in_specs,
        out_specs=[pl.BlockSpec((tm, tn), lambda i, j: (i, j))] + [pl.BlockSpec(memory_space=pl.ANY)] * len(side),
        out_shape=[jax.ShapeDtypeStruct((t, n), out_dtype)] + side_out_shape,
        scratch_shapes=scratch,
        compiler_params=_params(("arbitrary", "arbitrary") if side else ("parallel", "arbitrary"), vmem),
        name=name,
    )(*args)
    return (outs[0], outs[1:]) if side else outs[0]


def _log_sigmoid(x):
    return jnp.minimum(x, 0.0) - jnp.log1p(jnp.exp(-jnp.abs(x)))


def _gla_kernel(*refs, heads, seq_rows, has_h0, scale):
    if has_h0:
        q_ref, k_ref, v_ref, r_ref, a_ref, wup_ref, bal_ref, gn_ref, h0_ref, og_ref, st_ref = refs
    else:
        q_ref, k_ref, v_ref, r_ref, a_ref, wup_ref, bal_ref, gn_ref, og_ref, st_ref = refs
    rows, key = q_ref.shape
    val = v_ref.shape[1]
    dk, dv = key // heads, val // heads
    nseq = rows // seq_rows

    @pl.when(pl.program_id(1) == 0)
    def _():
        if has_h0:
            st_ref[...] = h0_ref[...]
        else:
            st_ref[...] = jnp.zeros(st_ref.shape, F32)

    la = _dot(a_ref[...].astype(BF16), wup_ref[...]) + bal_ref[...]
    log_a = _log_sigmoid(la) * (1.0 / GLA_TAU)

    ri = lax.broadcasted_iota(jnp.int32, (rows, rows), 0)
    ci = lax.broadcasted_iota(jnp.int32, (rows, rows), 1)
    causal = ci <= ri
    if nseq > 1:
        shift = seq_rows.bit_length() - 1
        causal = jnp.logical_and(causal, (ri >> shift) == (ci >> shift))
        row_seq = lax.broadcasted_iota(jnp.int32, (rows, 1), 0) >> shift
    tri = jnp.where(causal, 1.0, 0.0).astype(BF16)
    b = sum(_dot(tri, part) for part in _split_bf16(log_a, 3))

    q = q_ref[...] * scale
    k = k_ref[...]
    v = v_ref[...]
    r = r_ref[...]
    gn = gn_ref[...]
    for h in range(heads):
        ks = slice(h * dk, (h + 1) * dk)
        vs = slice(h * dv, (h + 1) * dv)
        bh = b[:, ks]
        q_e = (q[:, ks] * jnp.exp(bh)).astype(BF16)
        k_e = (k[:, ks] * jnp.exp(-bh)).astype(BF16)
        vh = v[:, vs].astype(BF16)
        scores = jnp.where(causal, _dot_nt(q_e, k_e), 0.0)
        o = _dot(scores.astype(BF16), vh)
        for s in range(nseq):
            b_last = bh[(s + 1) * seq_rows - 1:(s + 1) * seq_rows, :]
            k_s = k[:, ks] * jnp.exp(b_last - bh)
            if nseq > 1:
                in_seq = row_seq == s
                k_s = jnp.where(in_seq, k_s, 0.0)
            state = st_ref[s, h]
            o_inter = _dot(q_e, state.astype(BF16))
            o = o + (jnp.where(in_seq, o_inter, 0.0) if nseq > 1 else o_inter)
            d_state = _dot_tn(k_s.astype(BF16), vh)
            decay_col = jnp.broadcast_to(jnp.exp(b_last), (LANES, dk)).T[:, 0:1]
            st_ref[s, h] = decay_col * state + d_state
        mu = jnp.mean(o, axis=-1, keepdims=True)
        oc = o - mu
        o_n = oc * lax.rsqrt(jnp.mean(oc * oc, axis=-1, keepdims=True) + EPS) * gn[:, vs]
        og_ref[:, vs] = (o_n * _silu(r[:, vs])).astype(og_ref.dtype)


def _gla(z, a_low, w_up, b_alpha, gn, h0, *, row0, n_seq, seq_len, heads, dk, dv, name):
    key, val = heads * dk, heads * dv
    assert val == 2 * key and z.shape[1] == 2 * key + 2 * val
    chunk = min(GLA_CHUNK, seq_len)
    assert seq_len % chunk == 0
    if chunk % SUBLANES == 0:
        rows, n_chunks = chunk, seq_len // chunk
    else:
        assert SUBLANES % seq_len == 0 and chunk == seq_len
        rows, n_chunks = SUBLANES, 1
    nseq = rows // chunk
    assert n_seq % nseq == 0 and row0 % rows == 0
    blk0 = row0 // rows
    row_map = lambda g, c: blk0 + g * n_chunks + c
    in_specs = [
        pl.BlockSpec((rows, key), lambda g, c: (row_map(g, c), 0)),
        pl.BlockSpec((rows, key), lambda g, c: (row_map(g, c), 1)),
        pl.BlockSpec((rows, val), lambda g, c: (row_map(g, c), 1)),
        pl.BlockSpec((rows, val), lambda g, c: (row_map(g, c), 2)),
        pl.BlockSpec((rows, LANES), lambda g, c: (row_map(g, c), 0)),
        pl.BlockSpec((LANES, key), lambda g, c: (0, 0)),
        pl.BlockSpec((1, key), lambda g, c: (0, 0)),
        pl.BlockSpec((1, val), lambda g, c: (0, 0)),
    ]
    args = [z, z, z, z, a_low, w_up, b_alpha.reshape(1, key), gn.reshape(1, val)]
    state_spec = pl.BlockSpec((nseq, heads, dk, dv), lambda g, c: (g, 0, 0, 0))
    if h0 is not None:
        in_specs.append(state_spec)
        args.append(h0)
    n_rows = n_seq * seq_len
    return pl.pallas_call(
        functools.partial(_gla_kernel, heads=heads, seq_rows=chunk, has_h0=h0 is not None, scale=dk ** -0.5),
        grid=(n_seq // nseq, n_chunks),
        in_specs=in_specs,
        out_specs=[pl.BlockSpec((rows, val), lambda g, c: (g * n_chunks + c, 0)), state_spec],
        out_shape=[jax.ShapeDtypeStruct((n_rows, val), BF16),
                   jax.ShapeDtypeStruct((n_seq, heads, dk, dv), F32)],
        compiler_params=_params(("parallel", "arbitrary"), 48),
        name=name,
    )(*args)


CONV_LANES = 512
CONV_ROWS = 32
HIST_ROWS = 32


def _ln_silu(x, g, b):
    mu = jnp.mean(x, axis=-1, keepdims=True)
    xc = x - mu
    var = jnp.mean(xc * xc, axis=-1, keepdims=True)
    return _silu(xc * lax.rsqrt(var + EPS) * g + b)


def _conv_strip(ext_ref, w_ref, row_start, n_rows, lanes, width):
    n_lanes = lanes.stop - lanes.start
    total = None
    for r in range(SUBLANES):
        taps = [j for j in range(width) if (row_start + j) % SUBLANES == r]
        if not taps:
            continue
        rows = n_rows if r == 0 else n_rows + SUBLANES
        acc = jnp.zeros((rows, n_lanes), F32)
        for j in taps:
            start = row_start + j - r
            acc = acc + ext_ref[start:start + rows, lanes] * w_ref[j:j + 1, lanes]
        part = acc[r:r + n_rows, :]
        total = part if total is None else total + part
    return total


def _conv_prompt_kernel(u_ref, w_ref, wb_ref, g_ref, b_ref, c_ref, ext_ref, conv_ref, *, width):
    tb, cd = u_ref.shape
    first = HIST_ROWS - (width - 1)

    @pl.when(pl.program_id(1) == 0)
    def _():
        ext_ref[0:HIST_ROWS, :] = jnp.zeros((HIST_ROWS, cd), F32)

    ext_ref[HIST_ROWS:HIST_ROWS + tb, :] = u_ref[...]
    for r0 in range(0, tb, CONV_ROWS):
        for c0 in range(0, cd, CONV_LANES):
            lanes = slice(c0, c0 + CONV_LANES)
            conv_ref[r0:r0 + CONV_ROWS, lanes] = _conv_strip(ext_ref, w_ref, first + r0, CONV_ROWS, lanes, width)
    ext_ref[0:HIST_ROWS, :] = ext_ref[tb:tb + HIST_ROWS, :]
    c_ref[...] = _ln_silu(conv_ref[...] + wb_ref[...], g_ref[...], b_ref[...]).astype(c_ref.dtype)


def _conv_prompt(ug, w, wb, g, b, *, n_seq, seq_len):
    width, cd = w.shape
    assert width - 1 <= HIST_ROWS and cd % CONV_LANES == 0
    tb = _pick(seq_len, 64, CONV_ROWS)
    assert tb % CONV_ROWS == 0 and tb >= HIST_ROWS
    n_blk = seq_len // tb
    vec = pl.BlockSpec((1, cd), lambda s, t: (0, 0))
    return pl.pallas_call(
        functools.partial(_conv_prompt_kernel, width=width),
        grid=(n_seq, n_blk),
        in_specs=[pl.BlockSpec((tb, cd), lambda s, t: (s * n_blk + t, 0)),
                  pl.BlockSpec((width, cd), lambda s, t: (0, 0)), vec, vec, vec],
        out_specs=pl.BlockSpec((tb, cd), lambda s, t: (s * n_blk + t, 0)),
        out_shape=jax.ShapeDtypeStruct((n_seq * seq_len, cd), BF16),
        scratch_shapes=[pltpu.VMEM((HIST_ROWS + tb, cd), F32), pltpu.VMEM((tb, cd), F32)],
        compiler_params=_params(("parallel", "arbitrary"), 32),
        name="conv_prompt",
    )(ug, w, wb.reshape(1, cd), g.reshape(1, cd), b.reshape(1, cd))


def _conv_sample_kernel(u_ref, buf_ref, w_ref, wb_ref, g_ref, b_ref, c_ref, nbuf_ref, ext_ref, conv_ref,
                        *, width, seq_len):
    n_seq, hist, cd = buf_ref.shape
    pad_rows = ext_ref.shape[0] - hist - seq_len
    for s in range(n_seq):
        ext_ref[0:hist, :] = buf_ref[s]
        ext_ref[hist:hist + seq_len, :] = u_ref[s * seq_len:(s + 1) * seq_len, :]
        ext_ref[hist + seq_len:, :] = jnp.zeros((pad_rows, cd), F32)
        for c0 in range(0, cd, CONV_LANES):
            lanes = slice(c0, c0 + CONV_LANES)
            acc = _conv_strip(ext_ref, w_ref, 0, SUBLANES, lanes, width)
            conv_ref[s * seq_len:(s + 1) * seq_len, lanes] = acc[0:seq_len, :]
        nbuf_ref[s] = ext_ref[seq_len:seq_len + hist, :]
    c_ref[...] = _ln_silu(conv_ref[...] + wb_ref[...], g_ref[...], b_ref[...]).astype(c_ref.dtype)


def _conv_sample(ug, buf, w, wb, g, b, *, row0, seq_len):
    width, cd = w.shape
    n_seq, hist, _ = buf.shape
    assert hist == width - 1 and seq_len <= SUBLANES and cd % CONV_LANES == 0
    sb = _pick(n_seq, 8, 1)
    rows = sb * seq_len
    assert rows % 16 == 0 and row0 % rows == 0
    blk0 = row0 // rows
    ext_rows = -(-(hist + SUBLANES) // SUBLANES) * SUBLANES
    vec = pl.BlockSpec((1, cd), lambda i: (0, 0))
    return pl.pallas_call(
        functools.partial(_conv_sample_kernel, width=width, seq_len=seq_len),
        grid=(n_seq // sb,),
        in_specs=[pl.BlockSpec((rows, cd), lambda i: (blk0 + i, 0)),
                  pl.BlockSpec((sb, hist, cd), lambda i: (i, 0, 0)),
                  pl.BlockSpec((width, cd), lambda i: (0, 0)), vec, vec, vec],
        out_specs=[pl.BlockSpec((rows, cd), lambda i: (i, 0)),
                   pl.BlockSpec((sb, hist, cd), lambda i: (i, 0, 0))],
        out_shape=[jax.ShapeDtypeStruct((n_seq * seq_len, cd), BF16),
                   jax.ShapeDtypeStruct((n_seq, hist, cd), F32)],
        scratch_shapes=[pltpu.VMEM((ext_rows, cd), F32), pltpu.VMEM((rows, cd), F32)],
        compiler_params=_params(("parallel",), 32),
        name="conv_sample",
    )(ug, buf, w, wb.reshape(1, cd), g.reshape(1, cd), b.reshape(1, cd))


def _attn_kernel(q_ref, k_ref, v_ref, o_ref, *, heads, seq_rows, scale):
    rows, ca = q_ref.shape
    nseq = rows // seq_rows
    n_mem = k_ref.shape[0] // nseq
    hd = ca // heads
    q = q_ref[...]
    if nseq > 1:
        shift = seq_rows.bit_length() - 1
        row_seq = lax.broadcasted_iota(jnp.int32, (rows, 1), 0) >> shift
    for h in range(heads):
        hs = slice(h * hd, (h + 1) * hd)
        out = None
        for s in range(nseq):
            ms = slice(s * n_mem, (s + 1) * n_mem)
            sc = _dot_nt(q[:, hs], k_ref[ms, hs].astype(BF16)) * scale
            e = jnp.exp(sc - jnp.max(sc, axis=-1, keepdims=True))
            p = e / jnp.sum(e, axis=-1, keepdims=True)
            o = _dot(p.astype(BF16), v_ref[ms, hs].astype(BF16))
            out = o if out is None else jnp.where(row_seq == s, o, out)
        o_ref[:, hs] = out.astype(o_ref.dtype)


def _attn(q, k, v, *, row0, n_seq, seq_len, n_mem, heads, name):
    ca = q.shape[1]
    if seq_len % SUBLANES == 0:
        rows, nseq = _pick(seq_len, 256, 16), 1
    else:
        assert SUBLANES % seq_len == 0
        rows, nseq = 16, 16 // seq_len
    assert row0 % rows == 0 and (n_seq * seq_len) % rows == 0
    blk0 = row0 // rows
    per_seq = max(seq_len // rows, 1)
    kv_spec = pl.BlockSpec((nseq * n_mem, ca), lambda i: (i // per_seq, 0))
    return pl.pallas_call(
        functools.partial(_attn_kernel, heads=heads, seq_rows=seq_len if nseq > 1 else rows,
                          scale=(ca // heads) ** -0.5),
        grid=(n_seq * seq_len // rows,),
        in_specs=[pl.BlockSpec((rows, ca), lambda i: (blk0 + i, 0)), kv_spec, kv_spec],
        out_specs=pl.BlockSpec((rows, ca), lambda i: (i, 0)),
        out_shape=jax.ShapeDtypeStruct((n_seq * seq_len, ca), BF16),
        compiler_params=_params(("parallel",), 32),
        name=name,
    )(q, k, v)


ROUTE_E0, ROUTE_E1, ROUTE_G0, ROUTE_G1 = 0, 1, 2, 3
GATHER_UNROLL = 8
HIGH_HALF = 0xFFFF0000


def _token_stride(chunks):
    return chunks + SUBLANES


def _chunk(c, t0, n, chunks):
    stride = _token_stride(chunks)
    return pl.ds(t0 * stride + c, n, stride=stride), slice(None)


def _token_copies(hbm_ref, hbm_token, buf_ref, t0, n_tokens, chunks, sem, to_hbm):
    stride = _token_stride(chunks)

    def pair(hbm_rows, buf_rows):
        return (buf_rows, hbm_rows) if to_hbm else (hbm_rows, buf_rows)

    def start():
        def body(j, carry):
            for k in range(GATHER_UNROLL):
                r = j * GATHER_UNROLL + k
                src, dst = pair(hbm_ref.at[pl.ds(hbm_token(r) * chunks, chunks), :],
                                buf_ref.at[pl.ds((t0 + r) * stride, chunks), :])
                pltpu.make_async_copy(src, dst, sem).start()
            return carry

        lax.fori_loop(0, n_tokens // GATHER_UNROLL, body, 0)

    def wait():
        src, dst = pair(hbm_ref.at[pl.ds(0, n_tokens * chunks), :],
                        buf_ref.at[pl.ds(t0 * stride, n_tokens * chunks), :])
        pltpu.make_async_copy(src, dst, sem).wait()

    return start, wait


def _router_kernel(x_ref, g_ref, w_ref, b_ref, o_ref, htok_hbm, tokbuf, sem, *, n_groups, per_group):
    tm, d = x_ref.shape
    words = d // (2 * LANES)
    h = _rmsnorm_rows(x_ref[...], g_ref[...])
    bits = pltpu.bitcast(h.astype(BF16).astype(F32), jnp.uint32)
    for c in range(words):
        low = bits[:, 2 * c * LANES:(2 * c + 1) * LANES]
        high = bits[:, (2 * c + 1) * LANES:(2 * c + 2) * LANES]
        tokbuf[_chunk(c, 0, tm, words)] = (low >> 16) | (high & jnp.uint32(HIGH_HALF))
    row0 = pl.program_id(0) * tm
    start_tok_copy, wait_tok_copy = _token_copies(htok_hbm, lambda r: row0 + r, tokbuf, 0, tm, words, sem,
                                                  to_hbm=True)
    start_tok_copy()
    h_hi, h_lo = _split_bf16(h, 2)

    w_hi, w_lo = _split_bf16(w_ref[...], 2)
    logits = _dot(h_hi, w_hi) + (_dot(h_hi, w_lo) + _dot(h_lo, w_hi)) + b_ref[...]
    n_experts = n_groups * per_group
    lane_i = lax.broadcasted_iota(jnp.int32, logits.shape, 1)
    lane = lane_i.astype(F32)
    neg = jnp.float32(-jnp.inf)

    def first_argmax(x, m):
        return jnp.min(jnp.where(x == m, lane, float(LANES)), axis=-1, keepdims=True)

    lg = jnp.where(jnp.logical_and(lane_i >= n_experts, lane_i < n_experts + n_groups), logits, neg)
    mg = jnp.max(lg, axis=-1, keepdims=True)
    pg_top = 1.0 / jnp.sum(jnp.exp(lg - mg), axis=-1, keepdims=True)
    gsel = first_argmax(lg, mg) - float(n_experts)
    shift = per_group.bit_length() - 1
    in_group = jnp.logical_and(lane_i < n_experts, (lane_i >> shift).astype(F32) == gsel)
    le = jnp.where(in_group, logits, neg)
    m0 = jnp.max(le, axis=-1, keepdims=True)
    z = jnp.sum(jnp.exp(le - m0), axis=-1, keepdims=True)
    e0 = first_argmax(le, m0)
    le1 = jnp.where(lane == e0, neg, le)
    m1 = jnp.max(le1, axis=-1, keepdims=True)
    e1 = first_argmax(le1, m1)
    p0 = 1.0 / z
    p1 = jnp.exp(m1 - m0) / z
    den = p0 + p1
    g0 = pg_top * (p0 / den)
    g1 = pg_top * (p1 / den)
    rec = jnp.where(lane_i == ROUTE_E0, e0, 0.0)
    rec = jnp.where(lane_i == ROUTE_E1, e1, rec)
    rec = jnp.where(lane_i == ROUTE_G0, g0, rec)
    rec = jnp.where(lane_i == ROUTE_G1, g1, rec)
    o_ref[...] = rec
    wait_tok_copy()


def _router(x, g, w_group, b_group, w_expert, b_expert):
    t, d = x.shape
    n_groups, n_experts = w_group.shape[1], w_expert.shape[1]
    per_group = n_experts // n_groups
    assert n_experts + n_groups <= LANES and per_group & (per_group - 1) == 0
    pad = lambda a: jnp.pad(a, ((0, 0), (0, LANES - a.shape[1])))
    w_both = pad(jnp.concatenate([w_expert, w_group], axis=1))
    b_both = pad(jnp.concatenate([b_expert, b_group]).reshape(1, -1))
    words = d // (2 * LANES)
    tm = _pick(t, 256, GATHER_UNROLL)
    return pl.pallas_call(
        functools.partial(_router_kernel, n_groups=n_groups, per_group=per_group),
        grid=(t // tm,),
        in_specs=[pl.BlockSpec((tm, d), lambda i: (i, 0)), pl.BlockSpec((1, d), lambda i: (0, 0)),
                  pl.BlockSpec((d, LANES), lambda i: (0, 0)), pl.BlockSpec((1, LANES), lambda i: (0, 0))],
        out_specs=[pl.BlockSpec((tm, LANES), lambda i: (i, 0)), pl.BlockSpec(memory_space=pl.ANY)],
        out_shape=[jax.ShapeDtypeStruct((t, LANES), F32), jax.ShapeDtypeStruct((t * words, LANES), jnp.uint32)],
        scratch_shapes=[pltpu.VMEM((tm * _token_stride(words), LANES), jnp.uint32), pltpu.SemaphoreType.DMA(())],
        compiler_params=_params(("parallel",), 32),
        name="router",
    )(x, g.reshape(1, d), w_both, b_both)


def _dispatch_plan(e_ids, n_experts, tile):
    t, k = e_ids.shape
    n_pairs = t * k
    n_tiles = n_pairs // tile + n_experts
    flat = e_ids.reshape(n_pairs)
    onehot = (flat[:, None] == jnp.arange(n_experts, dtype=jnp.int32)[None, :]).astype(jnp.int32)
    counts = jnp.sum(onehot, axis=0)
    tiles_per = (counts + tile - 1) // tile
    tile_end = jnp.cumsum(tiles_per)
    tile_start = tile_end - tiles_per
    rank = jnp.take_along_axis(jnp.cumsum(onehot, axis=0), flat[:, None], axis=1)[:, 0] - 1
    pos = tile_start[flat] * tile + rank
    token = jnp.arange(n_pairs, dtype=jnp.int32) // k
    src = jnp.zeros((n_tiles * tile,), jnp.int32).at[pos].set(token)
    n_used = tile_end[-1]
    tile_ids = jnp.minimum(jnp.arange(n_tiles, dtype=jnp.int32), n_used - 1)
    tile_expert = jnp.searchsorted(tile_end, tile_ids, side="right").astype(jnp.int32)
    return pos.reshape(t, k), src.reshape(n_tiles, 1, tile), tile_expert, n_used.reshape(1).astype(jnp.int32)


DOWN_COLS = 1024


def _experts_kernel(texp_ref, nused_ref, idx_ref, idx_next_ref, h_hbm, wg_ref, wu_ref, wd_ref,
                    y_hbm, xbuf, ybuf, hbuf, gsem, ysem):
    i = pl.program_id(0)
    n_used = nused_ref[0]
    tile, d = hbuf.shape
    chunks = d // LANES
    words = chunks // 2
    slot = lax.rem(i, 2)
    wg_bf, wu_bf, wd_bf = wg_ref.at[0], wu_ref.at[0], wd_ref.at[0]
    start_tokens, wait_tokens = _token_copies(h_hbm, lambda r: idx_ref[0, 0, r], xbuf, slot * tile, tile, words,
                                              gsem.at[slot], to_hbm=False)
    start_next_tokens, _ = _token_copies(h_hbm, lambda r: idx_next_ref[0, 0, r], xbuf, (1 - slot) * tile, tile,
                                         words, gsem.at[1 - slot], to_hbm=False)

    def y_copy(step):
        return _token_copies(y_hbm, lambda r: step * tile + r, ybuf, 0, tile, chunks, ysem, to_hbm=True)

    @pl.when(i < n_used)
    def _():
        pl.when(i == 0)(start_tokens)
        pl.when(i + 1 < n_used)(start_next_tokens)
        wait_tokens()
        for c in range(words):
            word = xbuf[_chunk(c, slot * tile, tile, words)]
            low = pltpu.bitcast(word << 16, F32)
            high = pltpu.bitcast(word & jnp.uint32(HIGH_HALF), F32)
            hbuf[:, 2 * c * LANES:(2 * c + 1) * LANES] = low.astype(BF16)
            hbuf[:, (2 * c + 1) * LANES:(2 * c + 2) * LANES] = high.astype(BF16)
        h = hbuf[...]
        a = _dot(h, wg_bf[...])
        u = _dot(h, wu_bf[...])
        hid = (_silu(a) * u).astype(BF16)

        @pl.when(i > 0)
        def _():
            y_copy(i - 1)[1]()

        for c0 in range(0, d, DOWN_COLS):
            y = _dot(hid, wd_bf[:, c0:c0 + DOWN_COLS])
            for c in range(DOWN_COLS // LANES):
                ybuf[_chunk(c0 // LANES + c, 0, tile, chunks)] = y[:, c * LANES:(c + 1) * LANES]

    @pl.when(i >= n_used)
    def _():
        y_copy(i - 1)[1]()

        @pl.when(i == n_used)
        def _():
            ybuf[...] = jnp.zeros(ybuf.shape, F32)

    y_copy(i)[0]()

    @pl.when(i == pl.num_programs(0) - 1)
    def _():
        y_copy(i)[1]()


def _experts(h_tok, src, tile_expert, n_used, wg, wu, wd):
    d = wg.shape[1]
    chunks = d // LANES
    words = chunks // 2
    n_tiles, _, tile = src.shape
    de = wg.shape[2]
    assert d % 512 == 0 and de % 64 == 0 and d % DOWN_COLS == 0 and tile % GATHER_UNROLL == 0
    idx_spec = lambda ahead: pl.BlockSpec(
        (1, 1, tile), lambda i, texp, nused: (jnp.minimum(i + ahead, nused[0] - 1), 0, 0),
        memory_space=pltpu.SMEM)
    hbm = pl.BlockSpec(memory_space=pl.ANY)
    grid_spec = pltpu.PrefetchScalarGridSpec(
        num_scalar_prefetch=2,
        grid=(n_tiles,),
        in_specs=[idx_spec(0), idx_spec(1), hbm,
                  pl.BlockSpec((1, d, de), lambda i, texp, nused: (texp[i], 0, 0)),
                  pl.BlockSpec((1, d, de), lambda i, texp, nused: (texp[i], 0, 0)),
                  pl.BlockSpec((1, de, d), lambda i, texp, nused: (texp[i], 0, 0))],
        out_specs=hbm,
        scratch_shapes=[pltpu.VMEM((2 * tile * _token_stride(words), LANES), jnp.uint32),
                        pltpu.VMEM((tile * _token_stride(chunks), LANES), F32), pltpu.VMEM((tile, d), BF16),
                        pltpu.SemaphoreType.DMA((2,)), pltpu.SemaphoreType.DMA(())],
    )
    weights_bytes = 2 * 3 * d * de * 2
    tiles_bytes = tile * (2 * _token_stride(words) + _token_stride(chunks)) * LANES * 4
    tiles_bytes += tile * d * 2 + tile * DOWN_COLS * 4 + 3 * tile * de * 4
    return pl.pallas_call(
        _experts_kernel,
        grid_spec=grid_spec,
        out_shape=jax.ShapeDtypeStruct((n_tiles * tile * chunks, LANES), F32),
        compiler_params=_params(("arbitrary",), (weights_bytes + tiles_bytes) // MIB + 3),
        name="experts",
    )(tile_expert, n_used, src, src, h_tok, wg, wu, wd)


def _combine_kernel(idx0_ref, idx1_ref, idx0_next_ref, idx1_next_ref, y_hbm, x_ref, route_ref, gf_ref, *refs,
                    n_first, final_norm):
    outs, (buf0, buf1, res_ref, sem) = refs[:-4], refs[-4:]
    i = pl.program_id(0)
    tm, d = x_ref.shape
    chunks = d // LANES
    slot = lax.rem(i, 2)
    def gather(idx_ref, buf, k, into_slot):
        return _token_copies(y_hbm, lambda r: idx_ref[0, 0, r], buf, into_slot * tm, tm, chunks,
                             sem.at[k, into_slot], to_hbm=False)

    gathers = [gather(idx0_ref, buf0, 0, slot), gather(idx1_ref, buf1, 1, slot)]
    gathers_next = [gather(idx0_next_ref, buf0, 0, 1 - slot), gather(idx1_next_ref, buf1, 1, 1 - slot)]

    @pl.when(i == 0)
    def _():
        for start, _ in gathers:
            start()

    @pl.when(i + 1 < pl.num_programs(0))
    def _():
        for start, _ in gathers_next:
            start()

    for _, wait in gathers:
        wait()
    g0 = route_ref[:, ROUTE_G0:ROUTE_G0 + 1]
    g1 = route_ref[:, ROUTE_G1:ROUTE_G1 + 1]
    ssq = jnp.zeros((tm, LANES), F32)
    for c in range(chunks):
        lanes = slice(c * LANES, (c + 1) * LANES)
        y0 = buf0[_chunk(c, slot * tm, tm, chunks)]
        y1 = buf1[_chunk(c, slot * tm, tm, chunks)]
        r = x_ref[:, lanes] + (g0 * y0 + g1 * y1)
        res_ref[:, lanes] = r
        ssq = ssq + r * r
    if final_norm:
        scale = lax.rsqrt(jnp.sum(ssq, axis=-1, keepdims=True) * (1.0 / d) + EPS)
        result = lambda: res_ref[...] * scale * gf_ref[...]
    else:
        result = lambda: res_ref[...]
    if len(outs) == 1:
        outs[0][...] = result()
    else:
        @pl.when(i < n_first)
        def _():
            outs[0][...] = result()

        @pl.when(i >= n_first)
        def _():
            outs[1][...] = result()


def _combine(x, y_sorted, pos, route, g_final, *, split_rows, final_norm):
    t, d = x.shape
    assert y_sorted.shape[1] == LANES
    tm = _pick(t if split_rows is None else math.gcd(split_rows, t - split_rows), 128, GATHER_UNROLL)
    n_blk = t // tm
    idx = [pos[:, k].reshape(n_blk, 1, tm) for k in range(TOP_K_IN_GROUP)]
    smem = pl.BlockSpec((1, 1, tm), lambda i: (i, 0, 0), memory_space=pltpu.SMEM)
    smem_next = pl.BlockSpec((1, 1, tm), lambda i: (jnp.minimum(i + 1, n_blk - 1), 0, 0), memory_space=pltpu.SMEM)
    row = pl.BlockSpec((tm, d), lambda i: (i, 0))
    if split_rows is None:
        n_first = n_blk
        out_specs = [row]
        out_shape = [jax.ShapeDtypeStruct((t, d), F32)]
    else:
        n_first = split_rows // tm
        out_specs = [pl.BlockSpec((tm, d), lambda i: (jnp.minimum(i, n_first - 1), 0)),
                     pl.BlockSpec((tm, d), lambda i: (jnp.maximum(i - n_first, 0), 0))]
        out_shape = [jax.ShapeDtypeStruct((split_rows, d), F32), jax.ShapeDtypeStruct((t - split_rows, d), F32)]
    return pl.pallas_call(
        functools.partial(_combine_kernel, n_first=n_first, final_norm=final_norm),
        grid=(n_blk,),
        in_specs=[smem, smem, smem_next, smem_next, pl.BlockSpec(memory_space=pl.ANY), row,
                  pl.BlockSpec((tm, LANES), lambda i: (i, 0)), pl.BlockSpec((1, d), lambda i: (0, 0))],
        out_specs=out_specs,
        out_shape=out_shape,
        scratch_shapes=[pltpu.VMEM((2 * tm * _token_stride(d // LANES), LANES), F32),
                        pltpu.VMEM((2 * tm * _token_stride(d // LANES), LANES), F32), pltpu.VMEM((tm, d), F32),
                        pltpu.SemaphoreType.DMA((TOP_K_IN_GROUP, 2))],
        compiler_params=_params(("arbitrary",), 32),
        name="combine",
    )(idx[0], idx[1], idx[0], idx[1], y_sorted, x, route, g_final.reshape(1, d))


def kernel(x_prompt, x_sample, mem_prompt, state_gla, state_conv, cache_mem_k, cache_mem_v, norm_mix_g, w_in, w_alpha_up, b_alpha, gla_norm_g, w_branch_a, conv_dw_w, conv_dw_b, conv_ln_g, conv_ln_b, w_branch_b, w_out, norm_ca_g, norm_mem_g, w_ca_q, w_ca_k, w_ca_v, w_ca_o, norm_ffn_g, w_router_group, b_router_group, w_router_expert, b_router_expert, w_exp_gate, w_exp_up, w_exp_down, norm_final_g):
    depth = w_in.shape[0]
    bp, tp, d = x_prompt.shape
    bs, ts, _ = x_sample.shape
    heads, dk, dv = state_gla.shape[2:]
    key, val = heads * dk, heads * dv
    rank = w_alpha_up.shape[1]
    cd = state_conv.shape[3]
    n_mem, ca_heads, ca_hd = cache_mem_k.shape[2:]
    ca = ca_heads * ca_hd
    n_experts = w_router_expert.shape[2]
    rows_p, rows_s = bp * tp, bs * ts
    off_a = 2 * key + 2 * val
    off_u = off_a + rank
    off_g = off_u + 2 * cd
    assert rank <= LANES and w_in.shape[2] == off_g + 2 * d
    bf = lambda a: a.astype(BF16)

    x = (x_prompt.reshape(rows_p, d), x_sample.reshape(rows_s, d))
    outs = dict(gla_p=[], conv_p=[], mk_p=[], mv_p=[], gla_s=[], conv_s=[])
    for l in range(depth):
        w_in_t = jnp.swapaxes(w_in, 1, 2)[l]
        w_qkvr = _cast_weight(w_in_t, 0, off_a)
        w_a = bf(jnp.pad(w_in[l, :, off_a:off_u], ((0, 0), (0, LANES - rank))))
        h = _rmsnorm(x, norm_mix_g[l], BF16)
        dense = (w_branch_a, w_branch_b, w_out, w_ca_q, w_ca_k, w_ca_v, w_ca_o)
        z, (w_ug, wa_bf, wb_bf, wo_bf, wcq_bf, wck_bf, wcv_bf, wco_bf) = _mm(
            [h], [w_qkvr], [0], _ep_identity, F32, w_rows=[True], tm_pref=WIDE_ROWS,
            side_cast=[(w_in_t, off_u, 2 * cd + 2 * d)] + [(w[l], 0, w.shape[1]) for w in dense], name="in_qkvr")
        a_low = _mm([h], [w_a], [0], _ep_identity, F32, name="in_alow")
        ug = _mm([h], [w_ug, w_ug], [0, 0], _ep_glu, F32, n=cd, w_col0=[0, cd], w_rows=[True, True],
                 tm_pref=WIDE_ROWS, name="in_glu")
        w_up = bf(jnp.pad(w_alpha_up[l], ((0, LANES - rank), (0, 0))))
        gla_args = dict(heads=heads, dk=dk, dv=dv)
        og_p, st_p = _gla(z, a_low, w_up, b_alpha[l], gla_norm_g[l], None, row0=0, n_seq=bp, seq_len=tp,
                          name="gla_prompt", **gla_args)
        og_s, st_s = _gla(z, a_low, w_up, b_alpha[l], gla_norm_g[l], state_gla[l], row0=rows_p, n_seq=bs,
                          seq_len=ts, name="gla_sample", **gla_args)
        conv_w = (conv_dw_w[l], conv_dw_b[l], conv_ln_g[l], conv_ln_b[l])
        c_p = _conv_prompt(ug, *conv_w, n_seq=bp, seq_len=tp)
        c_s, buf_s = _conv_sample(ug, state_conv[l], *conv_w, row0=rows_p, seq_len=ts)
        og = (og_p, og_s)
        c = (c_p, c_s)
        flat = lambda w: w[l].reshape(-1, w.shape[-1])
        expert_casts = [(flat(w), 0, w.shape[1] * w.shape[2]) for w in (w_exp_gate, w_exp_up, w_exp_down)]
        merged, (wg_bf, wu_bf, wd_bf) = _mm(
            [og, c, h], [w_ug, w_ug, wa_bf, wb_bf], [2, 2, 0, 1], _ep_gated_merge, BF16,
            n=d, w_col0=[2 * cd, 2 * cd + d, 0, 0], w_rows=[True, True, False, False], tn_pref=256,
            side_cast=expert_casts, name="merge")
        x = _mm([merged], [wo_bf], [0], _ep_residual, F32, extras=[x], tn_pref=1024, name="out_proj")
        q = _mm([x], [wcq_bf], [0], _ep_identity, BF16, norm_gain=norm_ca_g[l], name="ca_q")
        m = _rmsnorm(mem_prompt.reshape(bp * n_mem, d), norm_mem_g[l], BF16)
        mk = _mm([m], [wck_bf], [0], _ep_identity, F32, name="mem_k")
        mv = _mm([m], [wcv_bf], [0], _ep_identity, F32, name="mem_v")
        ao_p = _attn(q, mk, mv, row0=0, n_seq=bp, seq_len=tp, n_mem=n_mem, heads=ca_heads, name="attn_prompt")
        ao_s = _attn(q, cache_mem_k[l].reshape(bs * n_mem, ca), cache_mem_v[l].reshape(bs * n_mem, ca),
                     row0=rows_p, n_seq=bs, seq_len=ts, n_mem=n_mem, heads=ca_heads, name="attn_sample")
        ao = (ao_p, ao_s)
        x = _mm([ao], [wco_bf], [0], _ep_residual, F32, extras=[x], tn_pref=1024, name="ca_out")
        route, h_tok = _router(x, norm_ffn_g[l], w_router_group[l], b_router_group[l], w_router_expert[l],
                               b_router_expert[l])
        e_ids = route[:, ROUTE_E0:ROUTE_E1 + 1].astype(jnp.int32)
        tile = _pick(rows_p + rows_s, 256, SUBLANES)
        pos, src, tile_expert, n_used = _dispatch_plan(e_ids, n_experts, tile)
        y_sorted = _experts(h_tok, src, tile_expert, n_used, wg_bf.reshape(w_exp_gate.shape[1:]),
                            wu_bf.reshape(w_exp_up.shape[1:]), wd_bf.reshape(w_exp_down.shape[1:]))
        last = l == depth - 1
        res = _combine(x, y_sorted, pos, route, norm_final_g, split_rows=rows_p if last else None,
                       final_norm=last)
        if not last:
            x = res[0]
        outs["gla_p"].append(st_p)
        hist = conv_dw_w.shape[1] - 1
        outs["conv_p"].append(jnp.stack([ug[(b + 1) * tp - hist:(b + 1) * tp] for b in range(bp)]))
        outs["mk_p"].append(mk.reshape(bp, n_mem, ca_heads, ca_hd))
        outs["mv_p"].append(mv.reshape(bp, n_mem, ca_heads, ca_hd))
        outs["gla_s"].append(st_s)
        outs["conv_s"].append(buf_s)
    y_prompt = res[0].reshape(bp, tp, d)
    y_sample = res[1].reshape(bs, ts, d)
    return (y_prompt, y_sample, jnp.stack(outs["gla_p"]), jnp.stack(outs["conv_p"]), jnp.stack(outs["mk_p"]),
            jnp.stack(outs["mv_p"]), jnp.stack(outs["gla_s"]), jnp.stack(outs["conv_s"]))
```
